```python
import jax, jax.numpy as jnp
from jax import lax
import numpy as np

D_MODEL = 4096
BATCH = 2
SEQ = 4096
DEPTH = 2

N_A = (DEPTH + 1) // 2
N_B = DEPTH - N_A
EPS = 1e-6

MLA_NOPE = 128
MLA_ROPE = 64
MLA_V = 128
MLA_HEADS = D_MODEL // 128
MLA_Q_LORA = 1024
MLA_KV_LORA = 512
ROPE_THETA = 10000.0
Q_BLOCK = 128

SWA_HEAD_DIM = 64
SWA_HEADS = D_MODEL // SWA_HEAD_DIM
SWA_KV_HEADS = 8
WINDOW = 128

D_FF = ((8 * D_MODEL // 3) + 255) // 256 * 256
CONV_W = 3

kernel_name = "yoco_mla_swa_sink_convffn"


def rmsnorm(x, g):
    xf = x.astype(jnp.float32)
    y = xf * lax.rsqrt(jnp.mean(xf * xf, axis=-1, keepdims=True) + EPS)
    return (y * g.astype(jnp.float32)).astype(x.dtype)


def rope(x, pos):
    half = x.shape[-1] // 2
    freqs = ROPE_THETA ** (-jnp.arange(half, dtype=jnp.float32) / half)
    ang = pos.astype(jnp.float32)[..., None] * freqs
    ang = ang.reshape(ang.shape[:2] + (1,) * (x.ndim - 3) + (half,))
    cos, sin = jnp.cos(ang), jnp.sin(ang)
    xf = x.astype(jnp.float32)
    x1, x2 = xf[..., :half], xf[..., half:]
    return jnp.concatenate([x1 * cos - x2 * sin, x2 * cos + x1 * sin], axis=-1).astype(x.dtype)


def mla_attention(h, pos, w_dq, q_norm, w_uq, w_dkv, kv_norm, w_ukv, w_o):
    B, S, _ = h.shape
    H = MLA_HEADS
    cq = rmsnorm(h @ w_dq, q_norm)
    q = (cq @ w_uq).reshape(B, S, H, MLA_NOPE + MLA_ROPE)
    q = jnp.concatenate([q[..., :MLA_NOPE], rope(q[..., MLA_NOPE:], pos)], axis=-1)
    ckv = h @ w_dkv
    c = rmsnorm(ckv[..., :MLA_KV_LORA], kv_norm)
    k_rope = rope(ckv[..., MLA_KV_LORA:], pos)
    kv = (c @ w_ukv).reshape(B, S, H, MLA_NOPE + MLA_V)
    k = jnp.concatenate(
        [kv[..., :MLA_NOPE], jnp.broadcast_to(k_rope[:, :, None, :], (B, S, H, MLA_ROPE))], axis=-1)
    v = kv[..., MLA_NOPE:]
    scale = (MLA_NOPE + MLA_ROPE) ** -0.5
    nb = S // Q_BLOCK
    q_blocks = q.reshape(B, nb, Q_BLOCK, H, MLA_NOPE + MLA_ROPE).transpose(1, 0, 2, 3, 4)
    key_pos = jnp.arange(S)

    def one_block(args):
        qb, i = args
        s = jnp.einsum('bqhd,bkhd->bhqk', qb, k).astype(jnp.float32) * scale
        q_pos = i * Q_BLOCK + jnp.arange(Q_BLOCK)
        causal = key_pos[None, :] <= q_pos[:, None]
        s = jnp.where(causal[None, None], s, -jnp.inf)
        p = jax.nn.softmax(s, axis=-1).astype(v.dtype)
        return jnp.einsum('bhqk,bkhd->bqhd', p, v)

    o = lax.map(one_block, (q_blocks, jnp.arange(nb)))
    o = o.transpose(1, 0, 2, 3, 4).reshape(B, S, H * MLA_V)
    return o @ w_o


def band(x):
    B, S = x.shape[:2]
    nb = S // WINDOW
    xb = x.reshape((B, nb, WINDOW) + x.shape[2:])
    prev = jnp.pad(xb[:, :-1], ((0, 0), (1, 0), (0, 0), (0, 0), (0, 0)))
    return jnp.concatenate([prev, xb], axis=2)


def shared_kv(h, g, w_k, w_v):
    B, S, _ = h.shape
    hn = rmsnorm(h, g)
    k = (hn @ w_k).reshape(B, S, SWA_KV_HEADS, SWA_HEAD_DIM)
    v = (hn @ w_v).reshape(B, S, SWA_KV_HEADS, SWA_HEAD_DIM)
    return band(k), band(v)


def alibi_slopes(n):
    return 2.0 ** (-8.0 * jnp.arange(1, n + 1, dtype=jnp.float32) / n)


def swa_attention(h, k_band, v_band, w_q, sinks, w_o):
    B, S, _ = h.shape
    nb = S // WINDOW
    G = SWA_KV_HEADS
    R = SWA_HEADS // SWA_KV_HEADS
    q = (h @ w_q).reshape(B, nb, WINDOW, G, R, SWA_HEAD_DIM)
    s = jnp.einsum('bnqgrd,bnkgd->bngrqk', q, k_band).astype(jnp.float32) * (SWA_HEAD_DIM ** -0.5)
    qi = jnp.arange(WINDOW)[:, None]
    kj = jnp.arange(2 * WINDOW)[None, :]
    dist = qi + WINDOW - kj
    slopes = alibi_slopes(SWA_HEADS).reshape(G, R)
    s = s - slopes[:, :, None, None] * dist.astype(jnp.float32)
    key_pos = jnp.arange(nb)[:, None, None] * WINDOW - WINDOW + kj[None]
    valid = (dist[None] >= 0) & (dist[None] < WINDOW) & (key_pos >= 0)
    s = jnp.where(valid[None, :, None, None], s, -jnp.inf)
    sink_col = jnp.broadcast_to(sinks.astype(jnp.float32).reshape(G, R)[None, None, :, :, None, None],
                                (B, nb, G, R, WINDOW, 1))
    p = jax.nn.softmax(jnp.concatenate([s, sink_col], axis=-1), axis=-1)[..., :-1].astype(v_band.dtype)
    o = jnp.einsum('bngrqk,bnkgd->bnqgrd', p, v_band).reshape(B, S, SWA_HEADS * SWA_HEAD_DIM)
    return o @ w_o


def conv_ffn(h, w_in, conv_w, conv_b, w_out):
    S = h.shape[1]
    u = h @ w_in
    up = jnp.pad(u, ((0, 0), (CONV_W - 1, 0), (0, 0)))
    u = conv_b + sum(conv_w[j] * up[:, j:j + S] for j in range(CONV_W))
    gate, val = u[..., :D_FF], u[..., D_FF:]
    return (jax.nn.silu(gate) * val) @ w_out


def setup_inputs(seed: int = 0) -> dict:
    key = jax.random.key(seed)
    ks = jax.random.split(key, 32)
    f32 = jnp.float32

    def w(k, shape, fan_in):
        return jax.random.normal(k, shape, f32) * (fan_in ** -0.5)

    def gain(k, shape):
        return 1.0 + 0.05 * jax.random.normal(k, shape, f32)

    D = D_MODEL
    return {
        "x": jax.random.normal(ks[0], (BATCH, SEQ, D), f32),
        "positions": (jnp.arange(SEQ, dtype=jnp.int32)[None, :]
                      + jax.random.randint(ks[1], (BATCH, 1), 0, 1024, dtype=jnp.int32)),
        "norm_mix_pre": gain(ks[2], (DEPTH, D)),
        "norm_mix_post": gain(ks[3], (DEPTH, D)),
        "norm_ffn_pre": gain(ks[4], (DEPTH, D)),
        "norm_ffn_post": gain(ks[5], (DEPTH, D)),
        "mla_w_dq": w(ks[6], (N_A, D, MLA_Q_LORA), D),
        "mla_q_norm": gain(ks[7], (N_A, MLA_Q_LORA)),
        "mla_w_uq": w(ks[8], (N_A, MLA_Q_LORA, MLA_HEADS * (MLA_NOPE + MLA_ROPE)), MLA_Q_LORA),
        "mla_w_dkv": w(ks[9], (N_A, D, MLA_KV_LORA + MLA_ROPE), D),
        "mla_kv_norm": gain(ks[10], (N_A, MLA_KV_LORA)),
        "mla_w_ukv": w(ks[11], (N_A, MLA_KV_LORA, MLA_HEADS * (MLA_NOPE + MLA_V)), MLA_KV_LORA),
        "mla_w_o": w(ks[12], (N_A, MLA_HEADS * MLA_V, D), MLA_HEADS * MLA_V),
        "shared_kv_norm": gain(ks[13], (D,)),
        "swa_w_k": w(ks[14], (D, SWA_KV_HEADS * SWA_HEAD_DIM), D),
        "swa_w_v": w(ks[15], (D, SWA_KV_HEADS * SWA_HEAD_DIM), D),
        "swa_w_q": w(ks[16], (N_B, D, SWA_HEADS * SWA_HEAD_DIM), D),
        "swa_sinks": 0.5 * jax.random.normal(ks[17], (N_B, SWA_HEADS), f32),
        "swa_w_o": w(ks[18], (N_B, SWA_HEADS * SWA_HEAD_DIM, D), SWA_HEADS * SWA_HEAD_DIM),
        "ffn_w_in": w(ks[19], (DEPTH, D, 2 * D_FF), D),
        "ffn_conv_w": w(ks[20], (DEPTH, CONV_W, 2 * D_FF), CONV_W),
        "ffn_conv_b": 0.01 * jax.random.normal(ks[21], (DEPTH, 2 * D_FF), f32),
        "ffn_w_out": w(ks[22], (DEPTH, D_FF, D), D_FF),
    }


def reference(x, positions, norm_mix_pre, norm_mix_post, norm_ffn_pre, norm_ffn_post,
              mla_w_dq, mla_q_norm, mla_w_uq, mla_w_dkv, mla_kv_norm, mla_w_ukv, mla_w_o,
              shared_kv_norm, swa_w_k, swa_w_v, swa_w_q, swa_sinks, swa_w_o,
              ffn_w_in, ffn_conv_w, ffn_conv_b, ffn_w_out):
    h = x
    k_band = None
    v_band = None
    for l in range(DEPTH):
        hn = rmsnorm(h, norm_mix_pre[l])
        if l < N_A:
            a = mla_attention(hn, positions, mla_w_dq[l], mla_q_norm[l], mla_w_uq[l],
                              mla_w_dkv[l], mla_kv_norm[l], mla_w_ukv[l], mla_w_o[l])
        else:
            if l == N_A:
                k_band, v_band = shared_kv(h, shared_kv_norm, swa_w_k, swa_w_v)
            j = l - N_A
            a = swa_attention(hn, k_band, v_band, swa_w_q[j], swa_sinks[j], swa_w_o[j])
        h = h + rmsnorm(a, norm_mix_post[l])
        f = conv_ffn(rmsnorm(h, norm_ffn_pre[l]), ffn_w_in[l], ffn_conv_w[l], ffn_conv_b[l], ffn_w_out[l])
        h = h + rmsnorm(f, norm_ffn_post[l])
    return h
```

```python
import functools

import jax
import jax.numpy as jnp
from jax import lax
from jax.experimental import pallas as pl
from jax.experimental.pallas import tpu as pltpu

EPS = 1e-6
ROPE_THETA = 10000.0
MLA_NOPE = 128
MLA_ROPE = 64
MLA_V = 128
MLA_QK = MLA_NOPE + MLA_ROPE
SWA_HEAD_DIM = 64
SWA_KV_HEADS = 8
WINDOW = 128
CONV_W = 3

V7X_VMEM_BYTES = 64 * 1024 * 1024
VMEM_LIMIT_CAP = V7X_VMEM_BYTES - 8 * 1024 * 1024
SUBLANES = 8

BF16 = jnp.bfloat16
F32 = jnp.float32


def _nbytes(shape, dtype):
    n = 1
    for s in shape:
        n *= s
    return n * jnp.dtype(dtype).itemsize


def _params(semantics, blocks, extra=0):
    est = 2 * sum(_nbytes(s, d) for s, d in blocks) + extra
    limit = min(VMEM_LIMIT_CAP, max(est + est // 4, 16 * 1024 * 1024))
    return pltpu.CompilerParams(dimension_semantics=semantics, vmem_limit_bytes=limit)


def _rms(x, g):
    r = lax.rsqrt(jnp.mean(x * x, axis=-1, keepdims=True) + EPS)
    return (x * r) * g


def _swap_halves(r):
    half = r.shape[-1] // 2
    return jnp.concatenate([r[:, half:], r[:, :half]], axis=-1)


def _rope(r, cosf, sinf):
    return r * cosf + _swap_halves(r) * sinf


def _rope_table_kernel(pos_ref, freq_ref, cos_ref, sin_ref):
    ang = pos_ref[...].astype(F32) * freq_ref[...]
    half = ang.shape[-1] // 2
    lane = lax.broadcasted_iota(jnp.int32, ang.shape, 1)
    s = jnp.sin(ang)
    cos_ref[...] = jnp.cos(ang)
    sin_ref[...] = jnp.where(lane < half, -s, s)


def _rope_tables(pos_col, freqs2):
    m = pos_col.shape[0]
    d = freqs2.shape[1]
    tm = 1024
    return pl.pallas_call(
        _rope_table_kernel,
        out_shape=(jax.ShapeDtypeStruct((m, d), F32), jax.ShapeDtypeStruct((m, d), F32)),
        grid=(m // tm,),
        in_specs=[pl.BlockSpec((tm, 1), lambda i: (i, 0)), pl.BlockSpec((1, d), lambda i: (0, 0))],
        out_specs=(pl.BlockSpec((tm, d), lambda i: (i, 0)), pl.BlockSpec((tm, d), lambda i: (i, 0))),
        name="rope_tables",
    )(pos_col, freqs2)


def _mla_down_kernel(x_ref, g_ref, wdq_ref, qn_ref, wdkv_ref, kvn_ref, cos_ref, sin_ref,
                     cq_ref, c_ref, kr_ref):
    hn = _rms(x_ref[...], g_ref[...]).astype(BF16)
    cq = jnp.dot(hn, wdq_ref[...], preferred_element_type=F32)
    cq_ref[...] = _rms(cq, qn_ref[...]).astype(BF16)
    ckv = jnp.dot(hn, wdkv_ref[...], preferred_element_type=F32)
    lora = c_ref.shape[-1]
    c_ref[...] = _rms(ckv[:, :lora], kvn_ref[...]).astype(BF16)
    kr_ref[...] = _rope(ckv[:, lora:], cos_ref[...], sin_ref[...]).astype(BF16)


def _mla_down(x2, g, wdq, qn, wdkv, kvn, cosf, sinf):
    m, d = x2.shape
    nq = wdq.shape[1]
    nkv = wdkv.shape[1]
    lora = kvn.shape[1]
    rd = nkv - lora
    tm = 256
    row = lambda i: (i, 0)
    fixed = lambda i: (0, 0)
    blocks = [((tm, d), F32), ((d, nq), BF16), ((d, nkv), BF16), ((tm, nq), BF16), ((tm, lora), BF16)]
    return pl.pallas_call(
        _mla_down_kernel,
        out_shape=(jax.ShapeDtypeStruct((m, nq), BF16), jax.ShapeDtypeStruct((m, lora), BF16),
                   jax.ShapeDtypeStruct((m, rd), BF16)),
        grid=(m // tm,),
        in_specs=[pl.BlockSpec((tm, d), row), pl.BlockSpec((1, d), fixed),
                  pl.BlockSpec((d, nq), fixed), pl.BlockSpec((1, nq), fixed),
                  pl.BlockSpec((d, nkv), fixed), pl.BlockSpec((1, lora), fixed),
                  pl.BlockSpec((tm, rd), row), pl.BlockSpec((tm, rd), row)],
        out_specs=(pl.BlockSpec((tm, nq), row), pl.BlockSpec((tm, lora), row), pl.BlockSpec((tm, rd), row)),
        compiler_params=_params(("arbitrary",), blocks, extra=4 * _nbytes((tm, d), F32)),
        name="mla_down",
    )(x2, g, wdq, qn, wdkv, kvn, cosf, sinf)


def _mla_q_up_kernel(cq_ref, w_ref, cos_ref, sin_ref, q_ref, *, scale):
    res = jnp.dot(cq_ref[...], w_ref[0], preferred_element_type=F32)
    cosf = cos_ref[...]
    sinf = sin_ref[...]
    for hh in range(q_ref.shape[1]):
        base = hh * MLA_QK
        nope = res[:, base:base + MLA_NOPE]
        r = _rope(res[:, base + MLA_NOPE:base + MLA_QK], cosf, sinf)
        q_ref[0, hh, :, :MLA_NOPE] = (nope * scale).astype(BF16)
        q_ref[0, hh, :, MLA_NOPE:] = (r * scale).astype(BF16)


def _mla_q_up(cq, w3, cosf, sinf, batch, seq, scale):
    m, k = cq.shape
    groups, _, gn = w3.shape
    hpg = gn // MLA_QK
    heads = groups * hpg
    tm = 512
    nsb = seq // tm
    blocks = [((tm, k), BF16), ((k, gn), BF16), ((hpg, tm, MLA_QK), BF16)]
    return pl.pallas_call(
        functools.partial(_mla_q_up_kernel, scale=scale),
        out_shape=jax.ShapeDtypeStruct((batch, heads, seq, MLA_QK), BF16),
        grid=(m // tm, groups),
        in_specs=[pl.BlockSpec((tm, k), lambda i, g: (i, 0)),
                  pl.BlockSpec((1, k, gn), lambda i, g: (g, 0, 0)),
                  pl.BlockSpec((tm, MLA_ROPE), lambda i, g: (i, 0)),
                  pl.BlockSpec((tm, MLA_ROPE), lambda i, g: (i, 0))],
        out_specs=pl.BlockSpec((1, hpg, tm, MLA_QK), lambda i, g: (i // nsb, g, i % nsb, 0)),
        compiler_params=_params(("arbitrary", "arbitrary"), blocks, extra=2 * _nbytes((tm, gn), F32)),
        name="mla_q_up",
    )(cq, w3, cosf, sinf)


def _mla_kv_up_kernel(c_ref, w_ref, kr_ref, k_ref, v_ref):
    res = jnp.dot(c_ref[...], w_ref[0], preferred_element_type=F32)
    kr = kr_ref[...]
    per = MLA_NOPE + MLA_V
    for hh in range(k_ref.shape[1]):
        base = hh * per
        k_ref[0, hh, :, :MLA_NOPE] = res[:, base:base + MLA_NOPE].astype(BF16)
        k_ref[0, hh, :, MLA_NOPE:] = kr
        v_ref[0, hh] = res[:, base + MLA_NOPE:base + per].astype(BF16)


def _mla_kv_up(c, w3, kr, batch, seq):
    m, k = c.shape
    groups, _, gn = w3.shape
    per = MLA_NOPE + MLA_V
    hpg = gn // per
    heads = groups * hpg
    tm = 512
    nsb = seq // tm
    blocks = [((tm, k), BF16), ((k, gn), BF16), ((hpg, tm, MLA_QK), BF16), ((hpg, tm, MLA_V), BF16)]
    omap = lambda i, g: (i // nsb, g, i % nsb, 0)
    return pl.pallas_call(
        _mla_kv_up_kernel,
        out_shape=(jax.ShapeDtypeStruct((batch, heads, seq, MLA_QK), BF16),
                   jax.ShapeDtypeStruct((batch, heads, seq, MLA_V), BF16)),
        grid=(m // tm, groups),
        in_specs=[pl.BlockSpec((tm, k), lambda i, g: (i, 0)),
                  pl.BlockSpec((1, k, gn), lambda i, g: (g, 0, 0)),
                  pl.BlockSpec((tm, MLA_ROPE), lambda i, g: (i, 0))],
        out_specs=(pl.BlockSpec((1, hpg, tm, MLA_QK), omap), pl.BlockSpec((1, hpg, tm, MLA_V), omap)),
        compiler_params=_params(("arbitrary", "arbitrary"), blocks, extra=2 * _nbytes((tm, gn), F32)),
        name="mla_kv_up",
    )(c, w3, kr)


def _mla_attn_kernel(q_ref, k_ref, v_ref, o_ref, m_ref, l_ref, acc_ref, *, tk):
    i = pl.program_id(2)
    tq = q_ref.shape[2]
    q = q_ref[0, 0]
    m_ref[...] = jnp.full(m_ref.shape, -jnp.inf, F32)
    l_ref[...] = jnp.zeros(l_ref.shape, F32)
    acc_ref[...] = jnp.zeros(acc_ref.shape, F32)

    def step(j, masked):
        start = pl.multiple_of(j * tk, tk)
        k = k_ref[0, 0, pl.ds(start, tk), :]
        v = v_ref[0, 0, pl.ds(start, tk), :]
        s = lax.dot_general(q, k, (((1,), (1,)), ((), ())), preferred_element_type=F32)
        if masked:
            qi = lax.broadcasted_iota(jnp.int32, s.shape, 0)
            kj = lax.broadcasted_iota(jnp.int32, s.shape, 1)
            s = jnp.where(kj <= qi, s, -jnp.inf)
        m_prev = m_ref[...]
        m_new = jnp.maximum(m_prev, jnp.max(s, axis=-1, keepdims=True))
        alpha = jnp.exp(m_prev - m_new)
        p = jnp.exp(s - m_new)
        l_ref[...] = alpha * l_ref[...] + jnp.sum(p, axis=-1, keepdims=True)
        acc_ref[...] = alpha * acc_ref[...] + jnp.dot(p.astype(BF16), v, preferred_element_type=F32)
        m_ref[...] = m_new

    def body(j, carry):
        step(j, False)
        return carry

    lax.fori_loop(0, i, body, 0)
    step(i, True)
    o_ref[0] = (acc_ref[...] / l_ref[...]).astype(o_ref.dtype)


def _mla_attention(q, k, v):
    batch, heads, seq, dqk = q.shape
    dv = v.shape[-1]
    tq = 512
    blocks = [((tq, dqk), BF16), ((seq, dqk), BF16), ((seq, dv), BF16), ((tq, dv), BF16)]
    return pl.pallas_call(
        functools.partial(_mla_attn_kernel, tk=tq),
        out_shape=jax.ShapeDtypeStruct((batch, seq, heads * dv), BF16),
        grid=(batch, heads, seq // tq),
        in_specs=[pl.BlockSpec((1, 1, tq, dqk), lambda b, h, i: (b, h, i, 0)),
                  pl.BlockSpec((1, 1, seq, dqk), lambda b, h, i: (b, h, 0, 0)),
                  pl.BlockSpec((1, 1, seq, dv), lambda b, h, i: (b, h, 0, 0))],
        out_specs=pl.BlockSpec((1, tq, dv), lambda b, h, i: (b, i, h)),
        scratch_shapes=[pltpu.VMEM((tq, 1), F32), pltpu.VMEM((tq, 1), F32), pltpu.VMEM((tq, dv), F32)],
        compiler_params=_params(("arbitrary", "arbitrary", "arbitrary"), blocks,
                                extra=6 * _nbytes((tq, tq), F32)),
        name="mla_attention",
    )(q, k, v)


def _matmul_kernel(x_ref, w_ref, o_ref, *, scale):
    acc = jnp.dot(x_ref[...], w_ref[...], preferred_element_type=F32)
    if scale is not None:
        acc = acc * scale
    o_ref[...] = acc.astype(o_ref.dtype)


def _matmul(x, w, out_dtype, tm, tn, scale=None, name="matmul"):
    m, k = x.shape
    n = w.shape[1]
    blocks = [((tm, k), x.dtype), ((k, tn), w.dtype), ((tm, tn), out_dtype)]
    return pl.pallas_call(
        functools.partial(_matmul_kernel, scale=scale),
        out_shape=jax.ShapeDtypeStruct((m, n), out_dtype),
        grid=(m // tm, n // tn),
        in_specs=[pl.BlockSpec((tm, k), lambda i, j: (i, 0)), pl.BlockSpec((k, tn), lambda i, j: (0, j))],
        out_specs=pl.BlockSpec((tm, tn), lambda i, j: (i, j)),
        compiler_params=_params(("arbitrary", "arbitrary"), blocks, extra=2 * _nbytes((tm, tn), F32)),
        name=name,
    )(x, w)


def _residual_norm_kernel(*refs, n_next):
    a_ref, h_ref, gp_ref = refs[:3]
    gn_refs = refs[3:3 + n_next]
    out_refs = refs[3 + n_next:]
    h_new = h_ref[...] + _rms(a_ref[...], gp_ref[...])
    out_refs[0][...] = h_new
    for gn_ref, o_ref in zip(gn_refs, out_refs[1:]):
        o_ref[...] = _rms(h_new, gn_ref[...]).astype(o_ref.dtype)


def _residual_norm(a, h, g_post, g_next):
    m, d = a.shape
    n_next = len(g_next)
    tm = 256
    row = lambda i: (i, 0)
    fixed = lambda i: (0, 0)
    out_shape = [jax.ShapeDtypeStruct((m, d), F32)] + [jax.ShapeDtypeStruct((m, d), BF16)] * n_next
    out_specs = [pl.BlockSpec((tm, d), row)] * (1 + n_next)
    blocks = [((tm, d), F32)] * (3 + n_next)
    return pl.pallas_call(
        functools.partial(_residual_norm_kernel, n_next=n_next),
        out_shape=tuple(out_shape),
        grid=(m // tm,),
        in_specs=[pl.BlockSpec((tm, d), row), pl.BlockSpec((tm, d), row)]
        + [pl.BlockSpec((1, d), fixed)] * (1 + n_next),
        out_specs=tuple(out_specs),
        compiler_params=_params(("arbitrary",), blocks),
        name="residual_norm",
    )(a, h, g_post, *g_next)


def _shift_rows(u, prev, s):
    rolled = pltpu.roll(u, s, 0)
    top_prev = pltpu.roll(prev, s, 0)
    rid = lax.broadcasted_iota(jnp.int32, prev.shape, 0)
    top = jnp.where(rid < s, top_prev, rolled[:SUBLANES])
    return jnp.concatenate([top, rolled[SUBLANES:]], axis=0)


def _ffn_up_kernel(x_ref, wg_ref, wv_ref, cwg_ref, cwv_ref, cbg_ref, cbv_ref, o_ref, cg_ref, cv_ref,
                   *, blocks_per_seq):
    i = pl.program_id(0)
    j = pl.program_id(1)
    x = x_ref[...]
    tm = x.shape[0]
    seq_start = (i % blocks_per_seq) == 0

    def conv(u, carry_ref, cw_ref, cb_ref):
        prev = jnp.where(seq_start, 0.0, carry_ref[j])
        carry_ref[j] = u[tm - SUBLANES:, :]
        cw = cw_ref[...]
        acc = cw[0:1, :] * _shift_rows(u, prev, 2)
        acc = acc + cw[1:2, :] * _shift_rows(u, prev, 1)
        acc = acc + cw[2:3, :] * u
        return cb_ref[...] + acc

    gate = conv(jnp.dot(x, wg_ref[...], preferred_element_type=F32), cg_ref, cwg_ref, cbg_ref)
    val = conv(jnp.dot(x, wv_ref[...], preferred_element_type=F32), cv_ref, cwv_ref, cbv_ref)
    act = gate * (1.0 / (1.0 + jnp.exp(-gate))) * val
    o_ref[...] = act.astype(o_ref.dtype)


def _ffn_up(hn, w_in, conv_w, conv_b, seq):
    m, d = hn.shape
    d_ff = w_in.shape[1] // 2
    tn = 256
    nj = d_ff // tn
    tm = 1024
    blocks = [((tm, d), BF16), ((d, tn), BF16), ((d, tn), BF16), ((tm, tn), BF16)]
    gate_col = lambda i, j: (0, j)
    val_col = lambda i, j: (0, j + nj)
    return pl.pallas_call(
        functools.partial(_ffn_up_kernel, blocks_per_seq=seq // tm),
        out_shape=jax.ShapeDtypeStruct((m, d_ff), BF16),
        grid=(m // tm, nj),
        in_specs=[pl.BlockSpec((tm, d), lambda i, j: (i, 0)),
                  pl.BlockSpec((d, tn), gate_col), pl.BlockSpec((d, tn), val_col),
                  pl.BlockSpec((CONV_W, tn), gate_col), pl.BlockSpec((CONV_W, tn), val_col),
                  pl.BlockSpec((1, tn), gate_col), pl.BlockSpec((1, tn), val_col)],
        out_specs=pl.BlockSpec((tm, tn), lambda i, j: (i, j)),
        scratch_shapes=[pltpu.VMEM((nj, SUBLANES, tn), F32), pltpu.VMEM((nj, SUBLANES, tn), F32)],
        compiler_params=_params(("arbitrary", "arbitrary"), blocks, extra=12 * _nbytes((tm, tn), F32)),
        name="ffn_up",
    )(hn, w_in, w_in, conv_w, conv_w, conv_b, conv_b)


def _swa_kv_kernel(x_ref, w_ref, o_ref):
    res = jnp.dot(x_ref[...], w_ref[...], preferred_element_type=F32)
    for t in range(o_ref.shape[0]):
        o_ref[t] = res[:, t * SWA_HEAD_DIM:(t + 1) * SWA_HEAD_DIM].astype(o_ref.dtype)


def _swa_kv(x, w):
    m, k = x.shape
    n = w.shape[1]
    slabs = n // SWA_HEAD_DIM
    tm = 512
    blocks = [((tm, k), BF16), ((k, n), BF16), ((slabs, tm, 128), BF16)]
    return pl.pallas_call(
        _swa_kv_kernel,
        out_shape=jax.ShapeDtypeStruct((slabs, m, SWA_HEAD_DIM), BF16),
        grid=(m // tm,),
        in_specs=[pl.BlockSpec((tm, k), lambda i: (i, 0)), pl.BlockSpec((k, n), lambda i: (0, 0))],
        out_specs=pl.BlockSpec((slabs, tm, SWA_HEAD_DIM), lambda i: (0, i, 0)),
        compiler_params=_params(("arbitrary",), blocks, extra=2 * _nbytes((tm, n), F32)),
        name="swa_kv",
    )(x, w)


def _swa_attn_kernel(slope_ref, sink_ref, q_ref, kp_ref, kc_ref, vp_ref, vc_ref, o_ref, *, rep):
    n = pl.program_id(1)
    g = pl.program_id(2)
    w = q_ref.shape[0]
    hd = SWA_HEAD_DIM
    q = q_ref[...]
    qs = jnp.concatenate([q[:, r * hd:(r + 1) * hd] for r in range(rep)], axis=0)
    kb = jnp.concatenate([kp_ref[0], kc_ref[0]], axis=0)
    vb = jnp.concatenate([vp_ref[0], vc_ref[0]], axis=0)
    s_all = lax.dot_general(qs, kb, (((1,), (1,)), ((), ())), preferred_element_type=F32)
    qi = lax.broadcasted_iota(jnp.int32, (w, 2 * w), 0)
    kj = lax.broadcasted_iota(jnp.int32, (w, 2 * w), 1)
    dist = qi + w - kj
    valid = (dist >= 0) & (dist < w) & ((kj >= w) | (n > 0))
    distf = dist.astype(F32)
    ps = []
    for r in range(rep):
        head = g * rep + r
        s = s_all[r * w:(r + 1) * w] - slope_ref[head] * distf
        s = jnp.where(valid, s, -jnp.inf)
        sink = sink_ref[head]
        mx = jnp.maximum(jnp.max(s, axis=-1, keepdims=True), sink)
        e = jnp.exp(s - mx)
        denom = jnp.sum(e, axis=-1, keepdims=True) + jnp.exp(sink - mx)
        ps.append((e / denom).astype(BF16))
    p_all = jnp.concatenate(ps, axis=0)
    o_all = jnp.dot(p_all, vb, preferred_element_type=F32)
    o = jnp.concatenate([o_all[r * w:(r + 1) * w] for r in range(rep)], axis=1)
    o_ref[...] = o.astype(o_ref.dtype)


def _swa_attention(q, kv, slopes, sinks, batch, seq):
    m, dq = q.shape
    groups = SWA_KV_HEADS
    rep = dq // (groups * SWA_HEAD_DIM)
    w = WINDOW
    nb = seq // w
    gw = rep * SWA_HEAD_DIM
    prev = lambda b, n, g: (g, b * nb + jnp.maximum(n - 1, 0), 0)
    cur = lambda b, n, g: (g, b * nb + n, 0)
    vprev = lambda b, n, g: (groups + g, b * nb + jnp.maximum(n - 1, 0), 0)
    vcur = lambda b, n, g: (groups + g, b * nb + n, 0)
    smem = pl.BlockSpec(memory_space=pltpu.SMEM)
    kvspec = lambda f: pl.BlockSpec((1, w, SWA_HEAD_DIM), f)
    return pl.pallas_call(
        functools.partial(_swa_attn_kernel, rep=rep),
        out_shape=jax.ShapeDtypeStruct((m, dq), BF16),
        grid=(batch, nb, groups),
        in_specs=[smem, smem, pl.BlockSpec((w, gw), lambda b, n, g: (b * nb + n, g)),
                  kvspec(prev), kvspec(cur), kvspec(vprev), kvspec(vcur)],
        out_specs=pl.BlockSpec((w, gw), lambda b, n, g: (b * nb + n, g)),
        name="swa_attention",
    )(slopes, sinks, q, kv, kv, kv, kv)


def kernel(x, positions, norm_mix_pre, norm_mix_post, norm_ffn_pre, norm_ffn_post, mla_w_dq, mla_q_norm,
           mla_w_uq, mla_w_dkv, mla_kv_norm, mla_w_ukv, mla_w_o, shared_kv_norm, swa_w_k, swa_w_v, swa_w_q,
           swa_sinks, swa_w_o, ffn_w_in, ffn_conv_w, ffn_conv_b, ffn_w_out):
    batch, seq, d = x.shape
    m = batch * seq
    depth = norm_mix_pre.shape[0]
    n_a = mla_w_dq.shape[0]
    heads = mla_w_o.shape[1] // MLA_V
    swa_heads = swa_w_q.shape[2] // SWA_HEAD_DIM
    hpg = 4

    half = MLA_ROPE // 2
    freqs = ROPE_THETA ** (-jnp.arange(half, dtype=F32) / half)
    freqs2 = jnp.concatenate([freqs, freqs])[None, :]
    cosf, sinf = _rope_tables(positions.reshape(m, 1), freqs2)
    slopes = 2.0 ** (-8.0 * jnp.arange(1, swa_heads + 1, dtype=F32) / swa_heads)

    row = lambda v: v.reshape(1, -1)
    h = x.reshape(m, d)
    assert depth == 2 and n_a == 1, "wired for one MLA layer followed by one sliding-window layer"

    def ffn(l, h, hn_ffn, g_next):
        act = _ffn_up(hn_ffn, ffn_w_in[l].astype(BF16), ffn_conv_w[l], row(ffn_conv_b[l]), seq)
        f = _matmul(act, ffn_w_out[l].astype(BF16), F32, 512, 256, name="ffn_down")
        return _residual_norm(f, h, row(norm_ffn_post[l]), g_next)

    w_uq = mla_w_uq[0].astype(BF16)
    w_uq3 = w_uq.reshape(w_uq.shape[0], heads // hpg, hpg * MLA_QK).transpose(1, 0, 2)
    w_ukv = mla_w_ukv[0].astype(BF16)
    w_ukv3 = w_ukv.reshape(w_ukv.shape[0], heads // hpg, hpg * (MLA_NOPE + MLA_V)).transpose(1, 0, 2)
    cq, c, kr = _mla_down(h, row(norm_mix_pre[0]), mla_w_dq[0].astype(BF16), row(mla_q_norm[0]),
                          mla_w_dkv[0].astype(BF16), row(mla_kv_norm[0]), cosf, sinf)
    q = _mla_q_up(cq, w_uq3, cosf, sinf, batch, seq, MLA_QK ** -0.5)
    k, v = _mla_kv_up(c, w_ukv3, kr, batch, seq)
    o = _mla_attention(q, k, v).reshape(m, heads * MLA_V)
    a = _matmul(o, mla_w_o[0].astype(BF16), F32, 1024, 512, name="mla_out")
    h, hn_ffn = _residual_norm(a, h, row(norm_mix_post[0]), [row(norm_ffn_pre[0])])
    h, hn_mix, hn_kv = ffn(0, h, hn_ffn, [row(norm_mix_pre[1]), row(shared_kv_norm)])

    w_kv = jnp.concatenate([swa_w_k, swa_w_v], axis=1).astype(BF16)
    kv_shared = _swa_kv(hn_kv, w_kv)
    q = _matmul(hn_mix, swa_w_q[0].astype(BF16), BF16, 1024, 512, scale=SWA_HEAD_DIM ** -0.5, name="swa_q")
    o = _swa_attention(q, kv_shared, slopes, swa_sinks[0], batch, seq)
    a = _matmul(o, swa_w_o[0].astype(BF16), F32, 1024, 512, name="swa_out")
    h, hn_ffn = _residual_norm(a, h, row(norm_mix_post[1]), [row(norm_ffn_pre[1])])
    (h,) = ffn(1, h, hn_ffn, [])
    return h.reshape(batch, seq, d)
```

```python
import functools

import jax
import jax.numpy as jnp
from jax import lax
from jax.experimental import pallas as pl
from jax.experimental.pallas import tpu as pltpu

EPS = 1e-6
LOG2_E = 1.4426950408889634
ROPE_THETA = 10000.0
MLA_NOPE = 128
MLA_ROPE = 64
MLA_V = 128
MLA_QK = MLA_NOPE + MLA_ROPE
SWA_HEAD_DIM = 64
SWA_KV_HEADS = 8
WINDOW = 128
CONV_W = 3

V7X_VMEM_BYTES = 64 * 1024 * 1024
VMEM_LIMIT_CAP = V7X_VMEM_BYTES - 8 * 1024 * 1024
SUBLANES = 8

BF16 = jnp.bfloat16
F32 = jnp.float32


def _nbytes(shape, dtype):
    n = 1
    for s in shape:
        n *= s
    return n * jnp.dtype(dtype).itemsize


def _params(semantics, blocks, extra=0):
    est = 2 * sum(_nbytes(s, d) for s, d in blocks) + extra
    limit = min(VMEM_LIMIT_CAP, max(est + est // 4, 16 * 1024 * 1024))
    return pltpu.CompilerParams(dimension_semantics=semantics, vmem_limit_bytes=limit)


def _rms(x, g):
    r = lax.rsqrt(jnp.mean(x * x, axis=-1, keepdims=True) + EPS)
    return (x * r) * g


def _swap_halves(r):
    half = r.shape[-1] // 2
    return jnp.concatenate([r[:, half:], r[:, :half]], axis=-1)


def _rope(r, cosf, sinf):
    return r * cosf + _swap_halves(r) * sinf


def _rope_table_kernel(pos_ref, freq_ref, cos_ref, sin_ref):
    ang = pos_ref[...].astype(F32) * freq_ref[...]
    half = ang.shape[-1] // 2
    lane = lax.broadcasted_iota(jnp.int32, ang.shape, 1)
    s = jnp.sin(ang)
    cos_ref[...] = jnp.cos(ang)
    sin_ref[...] = jnp.where(lane < half, -s, s)


def _rope_tables(pos_col, freqs2):
    m = pos_col.shape[0]
    d = freqs2.shape[1]
    tm = 1024
    return pl.pallas_call(
        _rope_table_kernel,
        out_shape=(jax.ShapeDtypeStruct((m, d), F32), jax.ShapeDtypeStruct((m, d), F32)),
        grid=(m // tm,),
        in_specs=[pl.BlockSpec((tm, 1), lambda i: (i, 0)), pl.BlockSpec((1, d), lambda i: (0, 0))],
        out_specs=(pl.BlockSpec((tm, d), lambda i: (i, 0)), pl.BlockSpec((tm, d), lambda i: (i, 0))),
        name="rope_tables",
    )(pos_col, freqs2)


def _mla_down_kernel(x_ref, g_ref, wdq_ref, qn_ref, wdkv_ref, kvn_ref, cos_ref, sin_ref,
                     cq_ref, c_ref, kr_ref):
    hn = _rms(x_ref[...], g_ref[...]).astype(BF16)
    cq = jnp.dot(hn, wdq_ref[...], preferred_element_type=F32)
    cq_ref[...] = _rms(cq, qn_ref[...]).astype(BF16)
    ckv = jnp.dot(hn, wdkv_ref[...], preferred_element_type=F32)
    lora = c_ref.shape[-1]
    c_ref[...] = _rms(ckv[:, :lora], kvn_ref[...]).astype(BF16)
    kr_ref[...] = _rope(ckv[:, lora:], cos_ref[...], sin_ref[...]).astype(BF16)


def _mla_down(x2, g, wdq, qn, wdkv, kvn, cosf, sinf):
    m, d = x2.shape
    nq = wdq.shape[1]
    nkv = wdkv.shape[1]
    lora = kvn.shape[1]
    rd = nkv - lora
    tm = 256
    row = lambda i: (i, 0)
    fixed = lambda i: (0, 0)
    blocks = [((tm, d), F32), ((d, nq), BF16), ((d, nkv), BF16), ((tm, nq), BF16), ((tm, lora), BF16)]
    return pl.pallas_call(
        _mla_down_kernel,
        out_shape=(jax.ShapeDtypeStruct((m, nq), BF16), jax.ShapeDtypeStruct((m, lora), BF16),
                   jax.ShapeDtypeStruct((m, rd), BF16)),
        grid=(m // tm,),
        in_specs=[pl.BlockSpec((tm, d), row), pl.BlockSpec((1, d), fixed),
                  pl.BlockSpec((d, nq), fixed), pl.BlockSpec((1, nq), fixed),
                  pl.BlockSpec((d, nkv), fixed), pl.BlockSpec((1, lora), fixed),
                  pl.BlockSpec((tm, rd), row), pl.BlockSpec((tm, rd), row)],
        out_specs=(pl.BlockSpec((tm, nq), row), pl.BlockSpec((tm, lora), row), pl.BlockSpec((tm, rd), row)),
        compiler_params=_params(("arbitrary",), blocks, extra=4 * _nbytes((tm, d), F32)),
        name="mla_down",
    )(x2, g, wdq, qn, wdkv, kvn, cosf, sinf)


def _mla_q_up_kernel(cq_ref, w_ref, cos_ref, sin_ref, q_ref, *, scale):
    res = jnp.dot(cq_ref[...], w_ref[0], preferred_element_type=F32)
    cosf = cos_ref[...]
    sinf = sin_ref[...]
    for hh in range(q_ref.shape[1]):
        base = hh * MLA_QK
        nope = res[:, base:base + MLA_NOPE]
        r = _rope(res[:, base + MLA_NOPE:base + MLA_QK], cosf, sinf)
        q_ref[0, hh, :, :MLA_NOPE] = (nope * scale).astype(BF16)
        q_ref[0, hh, :, MLA_NOPE:] = (r * scale).astype(BF16)


def _mla_q_up(cq, w3, cosf, sinf, batch, seq, scale):
    m, k = cq.shape
    groups, _, gn = w3.shape
    hpg = gn // MLA_QK
    heads = groups * hpg
    tm = 512
    nsb = seq // tm
    blocks = [((tm, k), BF16), ((k, gn), BF16), ((hpg, tm, MLA_QK), BF16)]
    return pl.pallas_call(
        functools.partial(_mla_q_up_kernel, scale=scale),
        out_shape=jax.ShapeDtypeStruct((batch, heads, seq, MLA_QK), BF16),
        grid=(m // tm, groups),
        in_specs=[pl.BlockSpec((tm, k), lambda i, g: (i, 0)),
                  pl.BlockSpec((1, k, gn), lambda i, g: (g, 0, 0)),
                  pl.BlockSpec((tm, MLA_ROPE), lambda i, g: (i, 0)),
                  pl.BlockSpec((tm, MLA_ROPE), lambda i, g: (i, 0))],
        out_specs=pl.BlockSpec((1, hpg, tm, MLA_QK), lambda i, g: (i // nsb, g, i % nsb, 0)),
        compiler_params=_params(("arbitrary", "arbitrary"), blocks, extra=2 * _nbytes((tm, gn), F32)),
        name="mla_q_up",
    )(cq, w3, cosf, sinf)


def _mla_kv_up_kernel(c_ref, w_ref, kr_ref, k_ref, vt_ref):
    res = jnp.dot(c_ref[...], w_ref[0], preferred_element_type=F32)
    kr = kr_ref[...]
    per = MLA_NOPE + MLA_V
    for hh in range(k_ref.shape[1]):
        base = hh * per
        k_ref[0, hh, :, :MLA_NOPE] = res[:, base:base + MLA_NOPE].astype(BF16)
        k_ref[0, hh, :, MLA_NOPE:] = kr
        vt_ref[0, hh] = res[:, base + MLA_NOPE:base + per].T.astype(BF16)


def _mla_kv_up(c, w3, kr, batch, seq):
    m, k = c.shape
    groups, _, gn = w3.shape
    per = MLA_NOPE + MLA_V
    hpg = gn // per
    heads = groups * hpg
    tm = 512
    nsb = seq // tm
    blocks = [((tm, k), BF16), ((k, gn), BF16), ((hpg, tm, MLA_QK), BF16), ((hpg, tm, MLA_V), BF16)]
    omap = lambda i, g: (i // nsb, g, i % nsb, 0)
    vmap = lambda i, g: (i // nsb, g, 0, i % nsb)
    return pl.pallas_call(
        _mla_kv_up_kernel,
        out_shape=(jax.ShapeDtypeStruct((batch, heads, seq, MLA_QK), BF16),
                   jax.ShapeDtypeStruct((batch, heads, MLA_V, seq), BF16)),
        grid=(m // tm, groups),
        in_specs=[pl.BlockSpec((tm, k), lambda i, g: (i, 0)),
                  pl.BlockSpec((1, k, gn), lambda i, g: (g, 0, 0)),
                  pl.BlockSpec((tm, MLA_ROPE), lambda i, g: (i, 0))],
        out_specs=(pl.BlockSpec((1, hpg, tm, MLA_QK), omap), pl.BlockSpec((1, hpg, MLA_V, tm), vmap)),
        compiler_params=_params(("arbitrary", "arbitrary"), blocks, extra=2 * _nbytes((tm, gn), F32)),
        name="mla_kv_up",
    )(c, w3, kr)


def _mla_attn_kernel(q_ref, k_ref, vt_ref, o_ref, sa_ref, sb_ref, m_ref, l_ref, acc_ref, *, tk):
    i = pl.program_id(2)
    hp = q_ref.shape[1]
    dv = acc_ref.shape[1]
    m_ref[...] = jnp.full(m_ref.shape, -jnp.inf, F32)
    l_ref[...] = jnp.zeros(l_ref.shape, F32)
    acc_ref[...] = jnp.zeros(acc_ref.shape, F32)

    def scores(j, s_ref):
        start = pl.multiple_of(j * tk, tk)
        for hh in range(hp):
            k = k_ref[0, hh, pl.ds(start, tk), :]
            s_ref[hh] = lax.dot_general(k, q_ref[0, hh], (((1,), (1,)), ((), ())),
                                        preferred_element_type=F32)

    def update(j, s_ref, masked):
        start = pl.multiple_of(j * tk, tk)
        for hh in range(hp):
            st = s_ref[hh]
            if masked:
                kj = lax.broadcasted_iota(jnp.int32, st.shape, 0)
                qi = lax.broadcasted_iota(jnp.int32, st.shape, 1)
                st = jnp.where(kj <= qi, st, -jnp.inf)
            vt = vt_ref[0, hh, :, pl.ds(start, tk)]
            m_prev = m_ref[hh]
            m_new = jnp.maximum(m_prev, jnp.max(st, axis=0, keepdims=True))
            alpha = jnp.exp2(m_prev - m_new)
            pt = jnp.exp2(st - m_new)
            l_ref[hh] = alpha * l_ref[hh] + jnp.sum(pt, axis=0, keepdims=True)
            acc_ref[hh] = alpha * acc_ref[hh] + jnp.dot(vt, pt.astype(BF16), preferred_element_type=F32)
            m_ref[hh] = m_new

    def finish():
        for hh in range(hp):
            o = acc_ref[hh] / l_ref[hh]
            o_ref[0, :, hh * dv:(hh + 1) * dv] = o.T.astype(o_ref.dtype)

    scores(0, sa_ref)

    def pair(jj, carry):
        j = 2 * jj
        scores(j + 1, sb_ref)
        update(j, sa_ref, False)
        scores(j + 2, sa_ref)
        update(j + 1, sb_ref, False)
        return carry

    lax.fori_loop(0, i // 2, pair, 0)

    @pl.when(i % 2 == 0)
    def _():
        update(i, sa_ref, True)
        finish()

    @pl.when(i % 2 == 1)
    def _():
        scores(i, sb_ref)
        update(i - 1, sa_ref, False)
        update(i, sb_ref, True)
        finish()


def _mla_attention(q, k, vt, hp):
    batch, heads, seq, dqk = q.shape
    dv = vt.shape[2]
    tq = 512
    blocks = [((hp, tq, 256), BF16), ((hp, seq, 256), BF16), ((hp, dv, seq), BF16), ((tq, hp * dv), BF16)]
    return pl.pallas_call(
        functools.partial(_mla_attn_kernel, tk=tq),
        out_shape=jax.ShapeDtypeStruct((batch, seq, heads * dv), BF16),
        grid=(batch, heads // hp, seq // tq),
        in_specs=[pl.BlockSpec((1, hp, tq, dqk), lambda b, h, i: (b, h, i, 0)),
                  pl.BlockSpec((1, hp, seq, dqk), lambda b, h, i: (b, h, 0, 0)),
                  pl.BlockSpec((1, hp, dv, seq), lambda b, h, i: (b, h, 0, 0))],
        out_specs=pl.BlockSpec((1, tq, hp * dv), lambda b, h, i: (b, i, h)),
        scratch_shapes=[pltpu.VMEM((hp, tq, tq), F32), pltpu.VMEM((hp, tq, tq), F32),
                        pltpu.VMEM((hp, 1, tq), F32), pltpu.VMEM((hp, 1, tq), F32),
                        pltpu.VMEM((hp, dv, tq), F32)],
        compiler_params=_params(("arbitrary", "arbitrary", "arbitrary"), blocks,
                                extra=8 * hp * _nbytes((tq, tq), F32)),
        name="mla_attention",
    )(q, k, vt)


def _matmul_kernel(x_ref, w_ref, o_ref, *, scale):
    acc = jnp.dot(x_ref[...], w_ref[...], preferred_element_type=F32)
    if scale is not None:
        acc = acc * scale
    o_ref[...] = acc.astype(o_ref.dtype)


def _matmul(x, w, out_dtype, tm, tn, scale=None, name="matmul"):
    m, k = x.shape
    n = w.shape[1]
    blocks = [((tm, k), x.dtype), ((k, tn), w.dtype), ((tm, tn), out_dtype)]
    return pl.pallas_call(
        functools.partial(_matmul_kernel, scale=scale),
        out_shape=jax.ShapeDtypeStruct((m, n), out_dtype),
        grid=(m // tm, n // tn),
        in_specs=[pl.BlockSpec((tm, k), lambda i, j: (i, 0)), pl.BlockSpec((k, tn), lambda i, j: (0, j))],
        out_specs=pl.BlockSpec((tm, tn), lambda i, j: (i, j)),
        compiler_params=_params(("arbitrary", "arbitrary"), blocks, extra=2 * _nbytes((tm, tn), F32)),
        name=name,
    )(x, w)


def _residual_norm_kernel(*refs, n_next):
    a_ref, h_ref, gp_ref = refs[:3]
    gn_refs = refs[3:3 + n_next]
    out_refs = refs[3 + n_next:]
    h_new = h_ref[...] + _rms(a_ref[...], gp_ref[...])
    out_refs[0][...] = h_new
    for gn_ref, o_ref in zip(gn_refs, out_refs[1:]):
        o_ref[...] = _rms(h_new, gn_ref[...]).astype(o_ref.dtype)


def _residual_norm(a, h, g_post, g_next):
    m, d = a.shape
    n_next = len(g_next)
    tm = 256
    row = lambda i: (i, 0)
    fixed = lambda i: (0, 0)
    out_shape = [jax.ShapeDtypeStruct((m, d), F32)] + [jax.ShapeDtypeStruct((m, d), BF16)] * n_next
    out_specs = [pl.BlockSpec((tm, d), row)] * (1 + n_next)
    blocks = [((tm, d), F32)] * (3 + n_next)
    return pl.pallas_call(
        functools.partial(_residual_norm_kernel, n_next=n_next),
        out_shape=tuple(out_shape),
        grid=(m // tm,),
        in_specs=[pl.BlockSpec((tm, d), row), pl.BlockSpec((tm, d), row)]
        + [pl.BlockSpec((1, d), fixed)] * (1 + n_next),
        out_specs=tuple(out_specs),
        compiler_params=_params(("arbitrary",), blocks),
        name="residual_norm",
    )(a, h, g_post, *g_next)


def _ffn_up_kernel(x_ref, wg_ref, wv_ref, cwg_ref, cwv_ref, cbg_ref, cbv_ref, o_ref, w_ref, *u_refs,
                   blocks_per_seq):
    i = pl.program_id(1)
    tn = wg_ref.shape[1]
    halo = SUBLANES
    nchunks = len(u_refs)
    chunk = u_refs[0].shape[0] - halo

    @pl.when(i == 0)
    def _():
        w_ref[:, :tn] = wg_ref[...].astype(BF16)
        w_ref[:, tn:] = wv_ref[...].astype(BF16)

    seq_start = (i % blocks_per_seq) == 0

    @pl.when(seq_start)
    def _():
        u_refs[0][0:halo, :] = jnp.zeros((halo, 2 * tn), F32)

    @pl.when(jnp.logical_not(seq_start))
    def _():
        u_refs[0][0:halo, :] = u_refs[-1][chunk:chunk + halo, :]

    def conv(u_ref, cols, cw_ref, cb_ref):
        acc = cw_ref[0:1, :] * u_ref[halo - 2:halo - 2 + chunk, cols]
        acc = acc + cw_ref[1:2, :] * u_ref[halo - 1:halo - 1 + chunk, cols]
        acc = acc + cw_ref[2:3, :] * u_ref[halo:halo + chunk, cols]
        return cb_ref[...] + acc

    def project(c):
        u = jnp.dot(x_ref[c * chunk:(c + 1) * chunk, :], w_ref[...], preferred_element_type=F32)
        u_refs[c][halo:halo + chunk, :] = u
        if c + 1 < nchunks:
            u_refs[c + 1][0:halo, :] = u[chunk - halo:, :]

    project(0)
    for c in range(nchunks):
        if c + 1 < nchunks:
            project(c + 1)
        gate = conv(u_refs[c], slice(0, tn), cwg_ref, cbg_ref)
        val = conv(u_refs[c], slice(tn, 2 * tn), cwv_ref, cbv_ref)
        act = gate * (1.0 / (1.0 + jnp.exp(-gate))) * val
        o_ref[c * chunk:(c + 1) * chunk, :] = act.astype(o_ref.dtype)


def _ffn_up(hn, w_in_all, layer, conv_w, conv_b, seq):
    m, d = hn.shape
    d_ff = w_in_all.shape[2] // 2
    tn = 256
    nj = d_ff // tn
    tm = 1024
    blocks = [((tm, d), BF16), ((d, tn), F32), ((d, tn), F32), ((tm, tn), BF16)]
    chunk = 256
    scratch = _nbytes((d, 2 * tn), BF16) + _nbytes((SUBLANES + tm, 2 * tn), F32)
    gate_col = lambda j, i: (0, j)
    val_col = lambda j, i: (0, j + nj)
    return pl.pallas_call(
        functools.partial(_ffn_up_kernel, blocks_per_seq=seq // tm),
        out_shape=jax.ShapeDtypeStruct((m, d_ff), BF16),
        grid=(nj, m // tm),
        in_specs=[pl.BlockSpec((tm, d), lambda j, i: (i, 0)),
                  pl.BlockSpec((None, d, tn), lambda j, i: (layer, 0, j)),
                  pl.BlockSpec((None, d, tn), lambda j, i: (layer, 0, j + nj)),
                  pl.BlockSpec((CONV_W, tn), gate_col), pl.BlockSpec((CONV_W, tn), val_col),
                  pl.BlockSpec((1, tn), gate_col), pl.BlockSpec((1, tn), val_col)],
        out_specs=pl.BlockSpec((tm, tn), lambda j, i: (i, j)),
        scratch_shapes=[pltpu.VMEM((d, 2 * tn), BF16)]
        + [pltpu.VMEM((SUBLANES + chunk, 2 * tn), F32)] * (tm // chunk),
        compiler_params=_params(("arbitrary", "arbitrary"), blocks, extra=scratch + 8 * _nbytes((chunk, 2 * tn), F32)),
        name="ffn_up",
    )(hn, w_in_all, w_in_all, conv_w, conv_w, conv_b, conv_b)


def _swa_kv_kernel(x_ref, w_ref, o_ref):
    res = jnp.dot(x_ref[...], w_ref[...], preferred_element_type=F32)
    for t in range(o_ref.shape[0]):
        o_ref[t] = res[:, t * SWA_HEAD_DIM:(t + 1) * SWA_HEAD_DIM].astype(o_ref.dtype)


def _swa_kv(x, w):
    m, k = x.shape
    n = w.shape[1]
    slabs = n // SWA_HEAD_DIM
    tm = 512
    blocks = [((tm, k), BF16), ((k, n), BF16), ((slabs, tm, 128), BF16)]
    return pl.pallas_call(
        _swa_kv_kernel,
        out_shape=jax.ShapeDtypeStruct((slabs, m, SWA_HEAD_DIM), BF16),
        grid=(m // tm,),
        in_specs=[pl.BlockSpec((tm, k), lambda i: (i, 0)), pl.BlockSpec((k, n), lambda i: (0, 0))],
        out_specs=pl.BlockSpec((slabs, tm, SWA_HEAD_DIM), lambda i: (0, i, 0)),
        compiler_params=_params(("arbitrary",), blocks, extra=2 * _nbytes((tm, n), F32)),
        name="swa_kv",
    )(x, w)


def _swa_attn_kernel(slope_ref, sink_ref, q_ref, kp_ref, kc_ref, vp_ref, vc_ref, o_ref, *, rep):
    n = pl.program_id(1)
    g = pl.program_id(2)
    w = q_ref.shape[0]
    hd = SWA_HEAD_DIM
    q = q_ref[...]
    qs = jnp.concatenate([q[:, r * hd:(r + 1) * hd] for r in range(rep)], axis=0)
    kb = jnp.concatenate([kp_ref[0], kc_ref[0]], axis=0)
    vb = jnp.concatenate([vp_ref[0], vc_ref[0]], axis=0)
    s_all = lax.dot_general(qs, kb, (((1,), (1,)), ((), ())), preferred_element_type=F32)
    qi = lax.broadcasted_iota(jnp.int32, (w, 2 * w), 0)
    kj = lax.broadcasted_iota(jnp.int32, (w, 2 * w), 1)
    dist = qi + w - kj
    valid = (dist >= 0) & (dist < w) & ((kj >= w) | (n > 0))
    distf = dist.astype(F32)
    ps = []
    for r in range(rep):
        head = g * rep + r
        s = s_all[r * w:(r + 1) * w] - slope_ref[head] * distf
        s = jnp.where(valid, s, -jnp.inf)
        sink = sink_ref[head]
        mx = jnp.maximum(jnp.max(s, axis=-1, keepdims=True), sink)
        e = jnp.exp(s - mx)
        denom = jnp.sum(e, axis=-1, keepdims=True) + jnp.exp(sink - mx)
        ps.append((e / denom).astype(BF16))
    p_all = jnp.concatenate(ps, axis=0)
    o_all = jnp.dot(p_all, vb, preferred_element_type=F32)
    o = jnp.concatenate([o_all[r * w:(r + 1) * w] for r in range(rep)], axis=1)
    o_ref[...] = o.astype(o_ref.dtype)


def _swa_attention(q, kv, slopes, sinks, batch, seq):
    m, dq = q.shape
    groups = SWA_KV_HEADS
    rep = dq // (groups * SWA_HEAD_DIM)
    w = WINDOW
    nb = seq // w
    gw = rep * SWA_HEAD_DIM
    prev = lambda b, n, g: (g, b * nb + jnp.maximum(n - 1, 0), 0)
    cur = lambda b, n, g: (g, b * nb + n, 0)
    vprev = lambda b, n, g: (groups + g, b * nb + jnp.maximum(n - 1, 0), 0)
    vcur = lambda b, n, g: (groups + g, b * nb + n, 0)
    smem = pl.BlockSpec(memory_space=pltpu.SMEM)
    kvspec = lambda f: pl.BlockSpec((1, w, SWA_HEAD_DIM), f)
    return pl.pallas_call(
        functools.partial(_swa_attn_kernel, rep=rep),
        out_shape=jax.ShapeDtypeStruct((m, dq), BF16),
        grid=(batch, nb, groups),
        in_specs=[smem, smem, pl.BlockSpec((w, gw), lambda b, n, g: (b * nb + n, g)),
                  kvspec(prev), kvspec(cur), kvspec(vprev), kvspec(vcur)],
        out_specs=pl.BlockSpec((w, gw), lambda b, n, g: (b * nb + n, g)),
        name="swa_attention",
    )(slopes, sinks, q, kv, kv, kv, kv)


def kernel(x, positions, norm_mix_pre, norm_mix_post, norm_ffn_pre, norm_ffn_post, mla_w_dq, mla_q_norm,
           mla_w_uq, mla_w_dkv, mla_kv_norm, mla_w_ukv, mla_w_o, shared_kv_norm, swa_w_k, swa_w_v, swa_w_q,
           swa_sinks, swa_w_o, ffn_w_in, ffn_conv_w, ffn_conv_b, ffn_w_out):
    batch, seq, d = x.shape
    m = batch * seq
    depth = norm_mix_pre.shape[0]
    n_a = mla_w_dq.shape[0]
    heads = mla_w_o.shape[1] // MLA_V
    swa_heads = swa_w_q.shape[2] // SWA_HEAD_DIM
    hpg = 4

    half = MLA_ROPE // 2
    freqs = ROPE_THETA ** (-jnp.arange(half, dtype=F32) / half)
    freqs2 = jnp.concatenate([freqs, freqs])[None, :]
    cosf, sinf = _rope_tables(positions.reshape(m, 1), freqs2)
    slopes = 2.0 ** (-8.0 * jnp.arange(1, swa_heads + 1, dtype=F32) / swa_heads)

    row = lambda v: v.reshape(1, -1)
    h = x.reshape(m, d)
    assert depth == 2 and n_a == 1, "wired for one MLA layer followed by one sliding-window layer"

    def ffn(l, h, hn_ffn, g_next):
        act = _ffn_up(hn_ffn, ffn_w_in, l, ffn_conv_w[l], row(ffn_conv_b[l]), seq)
        f = _matmul(act, ffn_w_out[l].astype(BF16), F32, 512, 256, name="ffn_down")
        return _residual_norm(f, h, row(norm_ffn_post[l]), g_next)

    w_uq = mla_w_uq[0].astype(BF16)
    w_uq3 = w_uq.reshape(w_uq.shape[0], heads // hpg, hpg * MLA_QK).transpose(1, 0, 2)
    w_ukv = mla_w_ukv[0].astype(BF16)
    w_ukv3 = w_ukv.reshape(w_ukv.shape[0], heads // hpg, hpg * (MLA_NOPE + MLA_V)).transpose(1, 0, 2)
    cq, c, kr = _mla_down(h, row(norm_mix_pre[0]), mla_w_dq[0].astype(BF16), row(mla_q_norm[0]),
                          mla_w_dkv[0].astype(BF16), row(mla_kv_norm[0]), cosf, sinf)
    q = _mla_q_up(cq, w_uq3, cosf, sinf, batch, seq, MLA_QK ** -0.5 * LOG2_E)
    k, vt = _mla_kv_up(c, w_ukv3, kr, batch, seq)
    o = _mla_attention(q, k, vt, hp=2).reshape(m, heads * MLA_V)
    a = _matmul(o, mla_w_o[0].astype(BF16), F32, 1024, 512, name="mla_out")
    h, hn_ffn = _residual_norm(a, h, row(norm_mix_post[0]), [row(norm_ffn_pre[0])])
    h, hn_mix, hn_kv = ffn(0, h, hn_ffn, [row(norm_mix_pre[1]), row(shared_kv_norm)])

    w_kv = jnp.concatenate([swa_w_k, swa_w_v], axis=1).astype(BF16)
    kv_shared = _swa_kv(hn_kv, w_kv)
    q = _matmul(hn_mix, swa_w_q[0].astype(BF16), BF16, 1024, 512, scale=SWA_HEAD_DIM ** -0.5, name="swa_q")
    o = _swa_attention(q, kv_shared, slopes, swa_sinks[0], batch, seq)
    a = _matmul(o, swa_w_o[0].astype(BF16), F32, 1024, 512, name="swa_out")
    h, hn_ffn = _residual_norm(a, h, row(norm_mix_post[1]), [row(norm_ffn_pre[1])])
    (h,) = ffn(1, h, hn_ffn, [])
    return h.reshape(batch, seq, d)
```

```python
import functools

import jax
import jax.numpy as jnp
from jax import lax
from jax.experimental import pallas as pl
from jax.experimental.pallas import tpu as pltpu

EPS = 1e-6
LOG2_E = 1.4426950408889634
ROPE_THETA = 10000.0
MLA_NOPE = 128
MLA_ROPE = 64
MLA_V = 128
MLA_QK = MLA_NOPE + MLA_ROPE
SWA_HEAD_DIM = 64
SWA_KV_HEADS = 8
WINDOW = 128
CONV_W = 3

V7X_VMEM_BYTES = 64 * 1024 * 1024
VMEM_LIMIT_CAP = V7X_VMEM_BYTES - 8 * 1024 * 1024
SUBLANES = 8

BF16 = jnp.bfloat16
F32 = jnp.float32


def _nbytes(shape, dtype):
    n = 1
    for s in shape:
        n *= s
    return n * jnp.dtype(dtype).itemsize


def _params(semantics, blocks, extra=0):
    est = 2 * sum(_nbytes(s, d) for s, d in blocks) + extra
    limit = min(VMEM_LIMIT_CAP, max(est + est // 4, 16 * 1024 * 1024))
    return pltpu.CompilerParams(dimension_semantics=semantics, vmem_limit_bytes=limit)


def _rms(x, g):
    r = lax.rsqrt(jnp.mean(x * x, axis=-1, keepdims=True) + EPS)
    return (x * r) * g


def _swap_halves(r):
    half = r.shape[-1] // 2
    return jnp.concatenate([r[:, half:], r[:, :half]], axis=-1)


def _rope(r, cosf, sinf):
    return r * cosf + _swap_halves(r) * sinf


def _rope_table_kernel(pos_ref, freq_ref, cos_ref, sin_ref):
    ang = pos_ref[...].astype(F32) * freq_ref[...]
    half = ang.shape[-1] // 2
    lane = lax.broadcasted_iota(jnp.int32, ang.shape, 1)
    s = jnp.sin(ang)
    cos_ref[...] = jnp.cos(ang)
    sin_ref[...] = jnp.where(lane < half, -s, s)


def _rope_tables(pos_col, freqs2):
    m = pos_col.shape[0]
    d = freqs2.shape[1]
    tm = 1024
    return pl.pallas_call(
        _rope_table_kernel,
        out_shape=(jax.ShapeDtypeStruct((m, d), F32), jax.ShapeDtypeStruct((m, d), F32)),
        grid=(m // tm,),
        in_specs=[pl.BlockSpec((tm, 1), lambda i: (i, 0)), pl.BlockSpec((1, d), lambda i: (0, 0))],
        out_specs=(pl.BlockSpec((tm, d), lambda i: (i, 0)), pl.BlockSpec((tm, d), lambda i: (i, 0))),
        name="rope_tables",
    )(pos_col, freqs2)


def _mla_down_kernel(x_ref, g_ref, wdq_ref, qn_ref, wdkv_ref, kvn_ref, cos_ref, sin_ref,
                     cq_ref, c_ref, kr_ref):
    hn = _rms(x_ref[...], g_ref[...]).astype(BF16)
    cq = jnp.dot(hn, wdq_ref[...], preferred_element_type=F32)
    cq_ref[...] = _rms(cq, qn_ref[...]).astype(BF16)
    ckv = jnp.dot(hn, wdkv_ref[...], preferred_element_type=F32)
    lora = c_ref.shape[-1]
    c_ref[...] = _rms(ckv[:, :lora], kvn_ref[...]).astype(BF16)
    kr_ref[...] = _rope(ckv[:, lora:], cos_ref[...], sin_ref[...]).astype(BF16)


def _mla_down(x2, g, wdq, qn, wdkv, kvn, cosf, sinf):
    m, d = x2.shape
    nq = wdq.shape[1]
    nkv = wdkv.shape[1]
    lora = kvn.shape[1]
    rd = nkv - lora
    tm = 256
    row = lambda i: (i, 0)
    fixed = lambda i: (0, 0)
    blocks = [((tm, d), F32), ((d, nq), BF16), ((d, nkv), BF16), ((tm, nq), BF16), ((tm, lora), BF16)]
    return pl.pallas_call(
        _mla_down_kernel,
        out_shape=(jax.ShapeDtypeStruct((m, nq), BF16), jax.ShapeDtypeStruct((m, lora), BF16),
                   jax.ShapeDtypeStruct((m, rd), BF16)),
        grid=(m // tm,),
        in_specs=[pl.BlockSpec((tm, d), row), pl.BlockSpec((1, d), fixed),
                  pl.BlockSpec((d, nq), fixed), pl.BlockSpec((1, nq), fixed),
                  pl.BlockSpec((d, nkv), fixed), pl.BlockSpec((1, lora), fixed),
                  pl.BlockSpec((tm, rd), row), pl.BlockSpec((tm, rd), row)],
        out_specs=(pl.BlockSpec((tm, nq), row), pl.BlockSpec((tm, lora), row), pl.BlockSpec((tm, rd), row)),
        compiler_params=_params(("arbitrary",), blocks, extra=4 * _nbytes((tm, d), F32)),
        name="mla_down",
    )(x2, g, wdq, qn, wdkv, kvn, cosf, sinf)


def _mla_q_up_kernel(cq_ref, w_ref, cos_ref, sin_ref, q_ref, *, scale):
    res = jnp.dot(cq_ref[...], w_ref[0], preferred_element_type=F32)
    cosf = cos_ref[...]
    sinf = sin_ref[...]
    for hh in range(q_ref.shape[1]):
        base = hh * MLA_QK
        nope = res[:, base:base + MLA_NOPE]
        r = _rope(res[:, base + MLA_NOPE:base + MLA_QK], cosf, sinf)
        q_ref[0, hh, :, :MLA_NOPE] = (nope * scale).astype(BF16)
        q_ref[0, hh, :, MLA_NOPE:] = (r * scale).astype(BF16)


def _mla_q_up(cq, w3, cosf, sinf, batch, seq, scale):
    m, k = cq.shape
    groups, _, gn = w3.shape
    hpg = gn // MLA_QK
    heads = groups * hpg
    tm = 512
    nsb = seq // tm
    blocks = [((tm, k), BF16), ((k, gn), BF16), ((hpg, tm, MLA_QK), BF16)]
    return pl.pallas_call(
        functools.partial(_mla_q_up_kernel, scale=scale),
        out_shape=jax.ShapeDtypeStruct((batch, heads, seq, MLA_QK), BF16),
        grid=(m // tm, groups),
        in_specs=[pl.BlockSpec((tm, k), lambda i, g: (i, 0)),
                  pl.BlockSpec((1, k, gn), lambda i, g: (g, 0, 0)),
                  pl.BlockSpec((tm, MLA_ROPE), lambda i, g: (i, 0)),
                  pl.BlockSpec((tm, MLA_ROPE), lambda i, g: (i, 0))],
        out_specs=pl.BlockSpec((1, hpg, tm, MLA_QK), lambda i, g: (i // nsb, g, i % nsb, 0)),
        compiler_params=_params(("arbitrary", "arbitrary"), blocks, extra=2 * _nbytes((tm, gn), F32)),
        name="mla_q_up",
    )(cq, w3, cosf, sinf)


def _mla_kv_up_kernel(c_ref, w_ref, kr_ref, k_ref, vt_ref):
    res = jnp.dot(c_ref[...], w_ref[0], preferred_element_type=F32)
    kr = kr_ref[...]
    per = MLA_NOPE + MLA_V
    for hh in range(k_ref.shape[1]):
        base = hh * per
        k_ref[0, hh, :, :MLA_NOPE] = res[:, base:base + MLA_NOPE].astype(BF16)
        k_ref[0, hh, :, MLA_NOPE:] = kr
        vt_ref[0, hh] = res[:, base + MLA_NOPE:base + per].T.astype(BF16)


def _mla_kv_up(c, w3, kr, batch, seq):
    m, k = c.shape
    groups, _, gn = w3.shape
    per = MLA_NOPE + MLA_V
    hpg = gn // per
    heads = groups * hpg
    tm = 512
    nsb = seq // tm
    blocks = [((tm, k), BF16), ((k, gn), BF16), ((hpg, tm, MLA_QK), BF16), ((hpg, tm, MLA_V), BF16)]
    omap = lambda i, g: (i // nsb, g, i % nsb, 0)
    vmap = lambda i, g: (i // nsb, g, 0, i % nsb)
    return pl.pallas_call(
        _mla_kv_up_kernel,
        out_shape=(jax.ShapeDtypeStruct((batch, heads, seq, MLA_QK), BF16),
                   jax.ShapeDtypeStruct((batch, heads, MLA_V, seq), BF16)),
        grid=(m // tm, groups),
        in_specs=[pl.BlockSpec((tm, k), lambda i, g: (i, 0)),
                  pl.BlockSpec((1, k, gn), lambda i, g: (g, 0, 0)),
                  pl.BlockSpec((tm, MLA_ROPE), lambda i, g: (i, 0))],
        out_specs=(pl.BlockSpec((1, hpg, tm, MLA_QK), omap), pl.BlockSpec((1, hpg, MLA_V, tm), vmap)),
        compiler_params=_params(("arbitrary", "arbitrary"), blocks, extra=2 * _nbytes((tm, gn), F32)),
        name="mla_kv_up",
    )(c, w3, kr)


def _mla_attn_kernel(q_ref, k_ref, vt_ref, o_ref, sa_ref, sb_ref, m_ref, l_ref, acc_ref, *, tk):
    i = pl.program_id(2)
    hp = q_ref.shape[1]
    dv = acc_ref.shape[1]
    m_ref[...] = jnp.full(m_ref.shape, -jnp.inf, F32)
    l_ref[...] = jnp.zeros(l_ref.shape, F32)
    acc_ref[...] = jnp.zeros(acc_ref.shape, F32)

    def scores(j, s_ref):
        start = pl.multiple_of(j * tk, tk)
        for hh in range(hp):
            k = k_ref[0, hh, pl.ds(start, tk), :]
            s_ref[hh] = lax.dot_general(k, q_ref[0, hh], (((1,), (1,)), ((), ())),
                                        preferred_element_type=F32)

    def update(j, s_ref, masked):
        start = pl.multiple_of(j * tk, tk)
        for hh in range(hp):
            st = s_ref[hh]
            if masked:
                kj = lax.broadcasted_iota(jnp.int32, st.shape, 0)
                qi = lax.broadcasted_iota(jnp.int32, st.shape, 1)
                st = jnp.where(kj <= qi, st, -jnp.inf)
            vt = vt_ref[0, hh, :, pl.ds(start, tk)]
            m_prev = m_ref[hh]
            m_new = jnp.maximum(m_prev, jnp.max(st, axis=0, keepdims=True))
            alpha = jnp.exp2(m_prev - m_new)
            pt = jnp.exp2(st - m_new)
            l_ref[hh] = alpha * l_ref[hh] + jnp.sum(pt, axis=0, keepdims=True)
            acc_ref[hh] = alpha * acc_ref[hh] + jnp.dot(vt, pt.astype(BF16), preferred_element_type=F32)
            m_ref[hh] = m_new

    def finish():
        for hh in range(hp):
            o = acc_ref[hh] / l_ref[hh]
            o_ref[0, :, hh * dv:(hh + 1) * dv] = o.T.astype(o_ref.dtype)

    scores(0, sa_ref)

    def pair(jj, carry):
        j = 2 * jj
        scores(j + 1, sb_ref)
        update(j, sa_ref, False)
        scores(j + 2, sa_ref)
        update(j + 1, sb_ref, False)
        return carry

    lax.fori_loop(0, i // 2, pair, 0)

    @pl.when(i % 2 == 0)
    def _():
        update(i, sa_ref, True)
        finish()

    @pl.when(i % 2 == 1)
    def _():
        scores(i, sb_ref)
        update(i - 1, sa_ref, False)
        update(i, sb_ref, True)
        finish()


def _mla_attention(q, k, vt, hp):
    batch, heads, seq, dqk = q.shape
    dv = vt.shape[2]
    tq = 512
    blocks = [((hp, tq, 256), BF16), ((hp, seq, 256), BF16), ((hp, dv, seq), BF16), ((tq, hp * dv), BF16)]
    return pl.pallas_call(
        functools.partial(_mla_attn_kernel, tk=tq),
        out_shape=jax.ShapeDtypeStruct((batch, seq, heads * dv), BF16),
        grid=(batch, heads // hp, seq // tq),
        in_specs=[pl.BlockSpec((1, hp, tq, dqk), lambda b, h, i: (b, h, i, 0)),
                  pl.BlockSpec((1, hp, seq, dqk), lambda b, h, i: (b, h, 0, 0)),
                  pl.BlockSpec((1, hp, dv, seq), lambda b, h, i: (b, h, 0, 0))],
        out_specs=pl.BlockSpec((1, tq, hp * dv), lambda b, h, i: (b, i, h)),
        scratch_shapes=[pltpu.VMEM((hp, tq, tq), F32), pltpu.VMEM((hp, tq, tq), F32),
                        pltpu.VMEM((hp, 1, tq), F32), pltpu.VMEM((hp, 1, tq), F32),
                        pltpu.VMEM((hp, dv, tq), F32)],
        compiler_params=_params(("arbitrary", "arbitrary", "arbitrary"), blocks,
                                extra=8 * hp * _nbytes((tq, tq), F32)),
        name="mla_attention",
    )(q, k, vt)


def _matmul_kernel(x_ref, w_ref, o_ref, *, scale):
    acc = jnp.dot(x_ref[...], w_ref[...], preferred_element_type=F32)
    if scale is not None:
        acc = acc * scale
    o_ref[...] = acc.astype(o_ref.dtype)


def _matmul(x, w, out_dtype, tm, tn, scale=None, name="matmul"):
    m, k = x.shape
    n = w.shape[1]
    blocks = [((tm, k), x.dtype), ((k, tn), w.dtype), ((tm, tn), out_dtype)]
    return pl.pallas_call(
        functools.partial(_matmul_kernel, scale=scale),
        out_shape=jax.ShapeDtypeStruct((m, n), out_dtype),
        grid=(m // tm, n // tn),
        in_specs=[pl.BlockSpec((tm, k), lambda i, j: (i, 0)), pl.BlockSpec((k, tn), lambda i, j: (0, j))],
        out_specs=pl.BlockSpec((tm, tn), lambda i, j: (i, j)),
        compiler_params=_params(("arbitrary", "arbitrary"), blocks, extra=2 * _nbytes((tm, tn), F32)),
        name=name,
    )(x, w)


def _residual_norm_kernel(*refs, n_next):
    a_ref, h_ref, gp_ref = refs[:3]
    gn_refs = refs[3:3 + n_next]
    out_refs = refs[3 + n_next:]
    h_new = h_ref[...] + _rms(a_ref[...], gp_ref[...])
    out_refs[0][...] = h_new
    for gn_ref, o_ref in zip(gn_refs, out_refs[1:]):
        o_ref[...] = _rms(h_new, gn_ref[...]).astype(o_ref.dtype)


def _residual_norm(a, h, g_post, g_next):
    m, d = a.shape
    n_next = len(g_next)
    tm = 256
    row = lambda i: (i, 0)
    fixed = lambda i: (0, 0)
    out_shape = [jax.ShapeDtypeStruct((m, d), F32)] + [jax.ShapeDtypeStruct((m, d), BF16)] * n_next
    out_specs = [pl.BlockSpec((tm, d), row)] * (1 + n_next)
    blocks = [((tm, d), F32)] * (3 + n_next)
    return pl.pallas_call(
        functools.partial(_residual_norm_kernel, n_next=n_next),
        out_shape=tuple(out_shape),
        grid=(m // tm,),
        in_specs=[pl.BlockSpec((tm, d), row), pl.BlockSpec((tm, d), row)]
        + [pl.BlockSpec((1, d), fixed)] * (1 + n_next),
        out_specs=tuple(out_specs),
        compiler_params=_params(("arbitrary",), blocks),
        name="residual_norm",
    )(a, h, g_post, *g_next)


def _ffn_up_kernel(x_ref, wg_ref, wv_ref, cwg_ref, cwv_ref, cbg_ref, cbv_ref, wdn_ref, o_ref, wdn_bf_ref,
                   w_ref, *u_refs, blocks_per_seq):
    i = pl.program_id(1)
    tn = wg_ref.shape[1]
    halo = SUBLANES
    nchunks = len(u_refs)
    chunk = u_refs[0].shape[0] - halo

    @pl.when(i == 0)
    def _():
        w_ref[:, :tn] = wg_ref[...].astype(BF16)
        w_ref[:, tn:] = wv_ref[...].astype(BF16)
        wdn_bf_ref[...] = wdn_ref[...].astype(BF16)

    seq_start = (i % blocks_per_seq) == 0

    @pl.when(seq_start)
    def _():
        u_refs[0][0:halo, :] = jnp.zeros((halo, 2 * tn), F32)

    @pl.when(jnp.logical_not(seq_start))
    def _():
        u_refs[0][0:halo, :] = u_refs[-1][chunk:chunk + halo, :]

    def conv(u_ref, cols, cw_ref, cb_ref):
        acc = cw_ref[0:1, :] * u_ref[halo - 2:halo - 2 + chunk, cols]
        acc = acc + cw_ref[1:2, :] * u_ref[halo - 1:halo - 1 + chunk, cols]
        acc = acc + cw_ref[2:3, :] * u_ref[halo:halo + chunk, cols]
        return cb_ref[...] + acc

    def project(c):
        u = jnp.dot(x_ref[c * chunk:(c + 1) * chunk, :], w_ref[...], preferred_element_type=F32)
        u_refs[c][halo:halo + chunk, :] = u
        if c + 1 < nchunks:
            u_refs[c + 1][0:halo, :] = u[chunk - halo:, :]

    project(0)
    for c in range(nchunks):
        if c + 1 < nchunks:
            project(c + 1)
        gate = conv(u_refs[c], slice(0, tn), cwg_ref, cbg_ref)
        val = conv(u_refs[c], slice(tn, 2 * tn), cwv_ref, cbv_ref)
        act = gate * (1.0 / (1.0 + jnp.exp(-gate))) * val
        o_ref[c * chunk:(c + 1) * chunk, :] = act.astype(o_ref.dtype)


def _ffn_up(hn, w_in_all, w_out_all, layer, conv_w, conv_b, seq):
    m, d = hn.shape
    d_ff = w_in_all.shape[2] // 2
    d_out = w_out_all.shape[2]
    tn = 256
    nj = d_ff // tn
    tm = 1024
    blocks = [((tm, d), BF16), ((d, tn), F32), ((d, tn), F32), ((tm, tn), BF16),
              ((tn, d_out), F32), ((tn, d_out), BF16)]
    chunk = 256
    scratch = _nbytes((d, 2 * tn), BF16) + _nbytes((SUBLANES + tm, 2 * tn), F32)
    gate_col = lambda j, i: (0, j)
    val_col = lambda j, i: (0, j + nj)
    return pl.pallas_call(
        functools.partial(_ffn_up_kernel, blocks_per_seq=seq // tm),
        out_shape=(jax.ShapeDtypeStruct((m, d_ff), BF16), jax.ShapeDtypeStruct((d_ff, d_out), BF16)),
        grid=(nj, m // tm),
        in_specs=[pl.BlockSpec((tm, d), lambda j, i: (i, 0)),
                  pl.BlockSpec((None, d, tn), lambda j, i: (layer, 0, j)),
                  pl.BlockSpec((None, d, tn), lambda j, i: (layer, 0, j + nj)),
                  pl.BlockSpec((CONV_W, tn), gate_col), pl.BlockSpec((CONV_W, tn), val_col),
                  pl.BlockSpec((1, tn), gate_col), pl.BlockSpec((1, tn), val_col),
                  pl.BlockSpec((None, tn, d_out), lambda j, i: (layer, j, 0))],
        out_specs=(pl.BlockSpec((tm, tn), lambda j, i: (i, j)), pl.BlockSpec((tn, d_out), lambda j, i: (j, 0))),
        scratch_shapes=[pltpu.VMEM((d, 2 * tn), BF16)]
        + [pltpu.VMEM((SUBLANES + chunk, 2 * tn), F32)] * (tm // chunk),
        compiler_params=_params(("arbitrary", "arbitrary"), blocks, extra=scratch + 8 * _nbytes((chunk, 2 * tn), F32)),
        name="ffn_up",
    )(hn, w_in_all, w_in_all, conv_w, conv_w, conv_b, conv_b, w_out_all)


def _swa_kv_kernel(x_ref, w_ref, k_ref, vt_ref):
    res = jnp.dot(x_ref[...], w_ref[...], preferred_element_type=F32)
    groups = k_ref.shape[0]
    hd = SWA_HEAD_DIM
    for t in range(groups):
        k_ref[t] = res[:, t * hd:(t + 1) * hd].astype(k_ref.dtype)
    vt = res[:, groups * hd:].T
    for t in range(groups):
        vt_ref[t] = vt[t * hd:(t + 1) * hd, :].astype(vt_ref.dtype)


def _swa_kv(x, w):
    m, k = x.shape
    n = w.shape[1]
    groups = n // (2 * SWA_HEAD_DIM)
    tm = 512
    blocks = [((tm, k), BF16), ((k, n), BF16), ((groups, tm, 128), BF16), ((groups, SWA_HEAD_DIM, tm), BF16)]
    return pl.pallas_call(
        _swa_kv_kernel,
        out_shape=(jax.ShapeDtypeStruct((groups, m, SWA_HEAD_DIM), BF16),
                   jax.ShapeDtypeStruct((groups, SWA_HEAD_DIM, m), BF16)),
        grid=(m // tm,),
        in_specs=[pl.BlockSpec((tm, k), lambda i: (i, 0)), pl.BlockSpec((k, n), lambda i: (0, 0))],
        out_specs=(pl.BlockSpec((groups, tm, SWA_HEAD_DIM), lambda i: (0, i, 0)),
                   pl.BlockSpec((groups, SWA_HEAD_DIM, tm), lambda i: (0, 0, i))),
        compiler_params=_params(("arbitrary",), blocks, extra=3 * _nbytes((tm, n), F32)),
        name="swa_kv",
    )(x, w)


def _swa_attn_kernel(bias_ref, sink_ref, q_ref, kp_ref, kc_ref, vtp_ref, vtc_ref, o_ref, *, rep):
    n = pl.program_id(2)
    w = WINDOW
    hd = SWA_HEAD_DIM
    nq = q_ref.shape[0] // w
    kfull = jnp.concatenate([kp_ref[0], kc_ref[0]], axis=0)
    vtfull = jnp.concatenate([vtp_ref[0], vtc_ref[0]], axis=1)
    sink = sink_ref[0]
    first = jnp.minimum(n, 1)
    for c in range(nq):
        qt = q_ref[c * w:(c + 1) * w, :].astype(F32).T
        qt = jnp.concatenate([qt[r * hd:(r + 1) * hd, :] for r in range(rep)], axis=1).astype(BF16)
        bias = bias_ref[0, first] if c == 0 else bias_ref[0, 1]
        st = jnp.dot(kfull[c * w:(c + 2) * w], qt, preferred_element_type=F32) + bias
        mx = jnp.maximum(jnp.max(st, axis=0, keepdims=True), sink)
        e = jnp.exp2(st - mx)
        denom = jnp.sum(e, axis=0, keepdims=True) + jnp.exp2(sink - mx)
        ot = jnp.dot(vtfull[:, c * w:(c + 2) * w], e.astype(BF16), preferred_element_type=F32)
        ot = ot * (1.0 / denom)
        o = jnp.concatenate([ot[:, r * w:(r + 1) * w] for r in range(rep)], axis=0)
        o_ref[c * w:(c + 1) * w, :] = o.T.astype(o_ref.dtype)


def _swa_attention(q, k, vt, bias, sink_lanes, batch, seq):
    m, dq = q.shape
    groups = k.shape[0]
    rep = dq // (groups * SWA_HEAD_DIM)
    w = WINDOW
    nq = 4
    tq = nq * w
    nb = seq // tq
    gw = rep * SWA_HEAD_DIM
    prev_blk = lambda b, n: jnp.maximum((b * nb + n) * nq - 1, b * nb * nq)
    blocks = [((2, 2 * w, rep * w), F32), ((tq, gw), BF16), ((tq, gw), BF16)]
    return pl.pallas_call(
        functools.partial(_swa_attn_kernel, rep=rep),
        out_shape=jax.ShapeDtypeStruct((m, dq), BF16),
        grid=(groups, batch, nb),
        in_specs=[pl.BlockSpec((1, 2, 2 * w, rep * w), lambda g, b, n: (g, 0, 0, 0)),
                  pl.BlockSpec((1, 1, rep * w), lambda g, b, n: (g, 0, 0)),
                  pl.BlockSpec((tq, gw), lambda g, b, n: (b * nb + n, g)),
                  pl.BlockSpec((1, w, SWA_HEAD_DIM), lambda g, b, n: (g, prev_blk(b, n), 0)),
                  pl.BlockSpec((1, tq, SWA_HEAD_DIM), lambda g, b, n: (g, b * nb + n, 0)),
                  pl.BlockSpec((1, SWA_HEAD_DIM, w), lambda g, b, n: (g, 0, prev_blk(b, n))),
                  pl.BlockSpec((1, SWA_HEAD_DIM, tq), lambda g, b, n: (g, 0, b * nb + n))],
        out_specs=pl.BlockSpec((tq, gw), lambda g, b, n: (b * nb + n, g)),
        compiler_params=_params(("arbitrary", "arbitrary", "arbitrary"), blocks,
                                extra=6 * nq * _nbytes((2 * w, rep * w), F32)),
        name="swa_attention",
    )(bias, sink_lanes, q, k, k, vt, vt)


def _swa_bias(swa_heads, groups):
    w = WINDOW
    rep = swa_heads // groups
    slopes = 2.0 ** (-8.0 * jnp.arange(1, swa_heads + 1, dtype=F32) / swa_heads)
    kj = jnp.arange(2 * w)[:, None]
    qi = jnp.arange(w)[None, :]
    dist = qi + w - kj
    in_window = (dist >= 0) & (dist < w)
    valid = jnp.stack([in_window & (kj >= w), in_window])
    alibi = slopes.reshape(groups, 1, 1, rep, 1) * dist.astype(F32)[None, None, :, None, :]
    bias = jnp.where(valid[None, :, :, None, :], -alibi * LOG2_E, -jnp.inf)
    return bias.reshape(groups, 2, 2 * w, rep * w)


def kernel(x, positions, norm_mix_pre, norm_mix_post, norm_ffn_pre, norm_ffn_post, mla_w_dq, mla_q_norm,
           mla_w_uq, mla_w_dkv, mla_kv_norm, mla_w_ukv, mla_w_o, shared_kv_norm, swa_w_k, swa_w_v, swa_w_q,
           swa_sinks, swa_w_o, ffn_w_in, ffn_conv_w, ffn_conv_b, ffn_w_out):
    batch, seq, d = x.shape
    m = batch * seq
    depth = norm_mix_pre.shape[0]
    n_a = mla_w_dq.shape[0]
    heads = mla_w_o.shape[1] // MLA_V
    swa_heads = swa_w_q.shape[2] // SWA_HEAD_DIM
    hpg = 4

    half = MLA_ROPE // 2
    freqs = ROPE_THETA ** (-jnp.arange(half, dtype=F32) / half)
    freqs2 = jnp.concatenate([freqs, freqs])[None, :]
    cosf, sinf = _rope_tables(positions.reshape(m, 1), freqs2)

    row = lambda v: v.reshape(1, -1)
    h = x.reshape(m, d)
    assert depth == 2 and n_a == 1, "wired for one MLA layer followed by one sliding-window layer"

    def ffn(l, h, hn_ffn, g_next):
        act, w_out = _ffn_up(hn_ffn, ffn_w_in, ffn_w_out, l, ffn_conv_w[l], row(ffn_conv_b[l]), seq)
        f = _matmul(act, w_out, F32, 512, 256, name="ffn_down")
        return _residual_norm(f, h, row(norm_ffn_post[l]), g_next)

    w_uq = mla_w_uq[0].astype(BF16)
    w_uq3 = w_uq.reshape(w_uq.shape[0], heads // hpg, hpg * MLA_QK).transpose(1, 0, 2)
    w_ukv = mla_w_ukv[0].astype(BF16)
    w_ukv3 = w_ukv.reshape(w_ukv.shape[0], heads // hpg, hpg * (MLA_NOPE + MLA_V)).transpose(1, 0, 2)
    cq, c, kr = _mla_down(h, row(norm_mix_pre[0]), mla_w_dq[0].astype(BF16), row(mla_q_norm[0]),
                          mla_w_dkv[0].astype(BF16), row(mla_kv_norm[0]), cosf, sinf)
    q = _mla_q_up(cq, w_uq3, cosf, sinf, batch, seq, MLA_QK ** -0.5 * LOG2_E)
    k, vt = _mla_kv_up(c, w_ukv3, kr, batch, seq)
    o = _mla_attention(q, k, vt, hp=4).reshape(m, heads * MLA_V)
    a = _matmul(o, mla_w_o[0].astype(BF16), F32, 1024, 512, name="mla_out")
    h, hn_ffn = _residual_norm(a, h, row(norm_mix_post[0]), [row(norm_ffn_pre[0])])
    h, hn_mix, hn_kv = ffn(0, h, hn_ffn, [row(norm_mix_pre[1]), row(shared_kv_norm)])

    w_kv = jnp.concatenate([swa_w_k, swa_w_v], axis=1).astype(BF16)
    k_shared, vt_shared = _swa_kv(hn_kv, w_kv)
    groups = k_shared.shape[0]
    sink_lanes = jnp.repeat(swa_sinks[0].reshape(groups, 1, swa_heads // groups) * LOG2_E, WINDOW, axis=2)
    q = _matmul(hn_mix, swa_w_q[0].astype(BF16), BF16, 1024, 512,
                scale=SWA_HEAD_DIM ** -0.5 * LOG2_E, name="swa_q")
    o = _swa_attention(q, k_shared, vt_shared, _swa_bias(swa_heads, groups), sink_lanes, batch, seq)
    a = _matmul(o, swa_w_o[0].astype(BF16), F32, 1024, 512, name="swa_out")
    h, hn_ffn = _residual_norm(a, h, row(norm_mix_post[1]), [row(norm_ffn_pre[1])])
    (h,) = ffn(1, h, hn_ffn, [])
    return h.reshape(batch, seq, d)
```

```python
import functools

import jax
import jax.numpy as jnp
from jax import lax
from jax.experimental import pallas as pl
from jax.experimental.pallas import tpu as pltpu

EPS = 1e-6
LOG2_E = 1.4426950408889634
ROPE_THETA = 10000.0
MLA_NOPE = 128
MLA_ROPE = 64
MLA_V = 128
MLA_QK = MLA_NOPE + MLA_ROPE
SWA_HEAD_DIM = 64
SWA_KV_HEADS = 8
WINDOW = 128
CONV_W = 3

V7X_VMEM_BYTES = 64 * 1024 * 1024
VMEM_LIMIT_CAP = V7X_VMEM_BYTES - 8 * 1024 * 1024
SUBLANES = 8

BF16 = jnp.bfloat16
F32 = jnp.float32


def _nbytes(shape, dtype):
    n = 1
    for s in shape:
        n *= s
    return n * jnp.dtype(dtype).itemsize


def _params(semantics, blocks, extra=0):
    est = 2 * sum(_nbytes(s, d) for s, d in blocks) + extra
    limit = min(VMEM_LIMIT_CAP, max(est + est // 4, 16 * 1024 * 1024))
    return pltpu.CompilerParams(dimension_semantics=semantics, vmem_limit_bytes=limit)


def _rms(x, g):
    r = lax.rsqrt(jnp.mean(x * x, axis=-1, keepdims=True) + EPS)
    return (x * r) * g


def _swap_halves(r):
    half = r.shape[-1] // 2
    return jnp.concatenate([r[:, half:], r[:, :half]], axis=-1)


def _rope(r, cosf, sinf):
    return r * cosf + _swap_halves(r) * sinf


def _rope_table_kernel(pos_ref, freq_ref, cos_ref, sin_ref):
    ang = pos_ref[...].astype(F32) * freq_ref[...]
    half = ang.shape[-1] // 2
    lane = lax.broadcasted_iota(jnp.int32, ang.shape, 1)
    s = jnp.sin(ang)
    cos_ref[...] = jnp.cos(ang)
    sin_ref[...] = jnp.where(lane < half, -s, s)


def _rope_tables(pos_col, freqs2):
    m = pos_col.shape[0]
    d = freqs2.shape[1]
    tm = 1024
    return pl.pallas_call(
        _rope_table_kernel,
        out_shape=(jax.ShapeDtypeStruct((m, d), F32), jax.ShapeDtypeStruct((m, d), F32)),
        grid=(m // tm,),
        in_specs=[pl.BlockSpec((tm, 1), lambda i: (i, 0)), pl.BlockSpec((1, d), lambda i: (0, 0))],
        out_specs=(pl.BlockSpec((tm, d), lambda i: (i, 0)), pl.BlockSpec((tm, d), lambda i: (i, 0))),
        name="rope_tables",
    )(pos_col, freqs2)


def _mla_down_kernel(x_ref, g_ref, wdq_ref, qn_ref, wdkv_ref, kvn_ref, cos_ref, sin_ref,
                     cq_ref, c_ref, kr_ref):
    hn = _rms(x_ref[...], g_ref[...]).astype(BF16)
    cq = jnp.dot(hn, wdq_ref[...], preferred_element_type=F32)
    cq_ref[...] = _rms(cq, qn_ref[...]).astype(BF16)
    ckv = jnp.dot(hn, wdkv_ref[...], preferred_element_type=F32)
    lora = c_ref.shape[-1]
    c_ref[...] = _rms(ckv[:, :lora], kvn_ref[...]).astype(BF16)
    kr_ref[...] = _rope(ckv[:, lora:], cos_ref[...], sin_ref[...]).astype(BF16)


def _mla_down(x2, g, wdq, qn, wdkv, kvn, cosf, sinf):
    m, d = x2.shape
    nq = wdq.shape[1]
    nkv = wdkv.shape[1]
    lora = kvn.shape[1]
    rd = nkv - lora
    tm = 256
    row = lambda i: (i, 0)
    fixed = lambda i: (0, 0)
    blocks = [((tm, d), F32), ((d, nq), BF16), ((d, nkv), BF16), ((tm, nq), BF16), ((tm, lora), BF16)]
    return pl.pallas_call(
        _mla_down_kernel,
        out_shape=(jax.ShapeDtypeStruct((m, nq), BF16), jax.ShapeDtypeStruct((m, lora), BF16),
                   jax.ShapeDtypeStruct((m, rd), BF16)),
        grid=(m // tm,),
        in_specs=[pl.BlockSpec((tm, d), row), pl.BlockSpec((1, d), fixed),
                  pl.BlockSpec((d, nq), fixed), pl.BlockSpec((1, nq), fixed),
                  pl.BlockSpec((d, nkv), fixed), pl.BlockSpec((1, lora), fixed),
                  pl.BlockSpec((tm, rd), row), pl.BlockSpec((tm, rd), row)],
        out_specs=(pl.BlockSpec((tm, nq), row), pl.BlockSpec((tm, lora), row), pl.BlockSpec((tm, rd), row)),
        compiler_params=_params(("arbitrary",), blocks, extra=4 * _nbytes((tm, d), F32)),
        name="mla_down",
    )(x2, g, wdq, qn, wdkv, kvn, cosf, sinf)


def _mla_q_up_kernel(cq_ref, w_ref, cos_ref, sin_ref, q_ref, *, scale):
    res = jnp.dot(cq_ref[...], w_ref[0], preferred_element_type=F32)
    cosf = cos_ref[...]
    sinf = sin_ref[...]
    for hh in range(q_ref.shape[1]):
        base = hh * MLA_QK
        nope = res[:, base:base + MLA_NOPE]
        r = _rope(res[:, base + MLA_NOPE:base + MLA_QK], cosf, sinf)
        q_ref[0, hh, :, :MLA_NOPE] = (nope * scale).astype(BF16)
        q_ref[0, hh, :, MLA_NOPE:] = (r * scale).astype(BF16)


def _mla_q_up(cq, w3, cosf, sinf, batch, seq, scale):
    m, k = cq.shape
    groups, _, gn = w3.shape
    hpg = gn // MLA_QK
    heads = groups * hpg
    tm = 512
    nsb = seq // tm
    blocks = [((tm, k), BF16), ((k, gn), BF16), ((hpg, tm, MLA_QK), BF16)]
    return pl.pallas_call(
        functools.partial(_mla_q_up_kernel, scale=scale),
        out_shape=jax.ShapeDtypeStruct((batch, heads, seq, MLA_QK), BF16),
        grid=(m // tm, groups),
        in_specs=[pl.BlockSpec((tm, k), lambda i, g: (i, 0)),
                  pl.BlockSpec((1, k, gn), lambda i, g: (g, 0, 0)),
                  pl.BlockSpec((tm, MLA_ROPE), lambda i, g: (i, 0)),
                  pl.BlockSpec((tm, MLA_ROPE), lambda i, g: (i, 0))],
        out_specs=pl.BlockSpec((1, hpg, tm, MLA_QK), lambda i, g: (i // nsb, g, i % nsb, 0)),
        compiler_params=_params(("arbitrary", "arbitrary"), blocks, extra=2 * _nbytes((tm, gn), F32)),
        name="mla_q_up",
    )(cq, w3, cosf, sinf)


def _mla_kv_up_kernel(c_ref, w_ref, kr_ref, k_ref, vt_ref):
    res = jnp.dot(c_ref[...], w_ref[0], preferred_element_type=F32)
    kr = kr_ref[...]
    per = MLA_NOPE + MLA_V
    for hh in range(k_ref.shape[1]):
        base = hh * per
        k_ref[0, hh, :, :MLA_NOPE] = res[:, base:base + MLA_NOPE].astype(BF16)
        k_ref[0, hh, :, MLA_NOPE:] = kr
        vt_ref[0, hh] = res[:, base + MLA_NOPE:base + per].T.astype(BF16)


def _mla_kv_up(c, w3, kr, batch, seq):
    m, k = c.shape
    groups, _, gn = w3.shape
    per = MLA_NOPE + MLA_V
    hpg = gn // per
    heads = groups * hpg
    tm = 512
    nsb = seq // tm
    blocks = [((tm, k), BF16), ((k, gn), BF16), ((hpg, tm, MLA_QK), BF16), ((hpg, tm, MLA_V), BF16)]
    omap = lambda i, g: (i // nsb, g, i % nsb, 0)
    vmap = lambda i, g: (i // nsb, g, 0, i % nsb)
    return pl.pallas_call(
        _mla_kv_up_kernel,
        out_shape=(jax.ShapeDtypeStruct((batch, heads, seq, MLA_QK), BF16),
                   jax.ShapeDtypeStruct((batch, heads, MLA_V, seq), BF16)),
        grid=(m // tm, groups),
        in_specs=[pl.BlockSpec((tm, k), lambda i, g: (i, 0)),
                  pl.BlockSpec((1, k, gn), lambda i, g: (g, 0, 0)),
                  pl.BlockSpec((tm, MLA_ROPE), lambda i, g: (i, 0))],
        out_specs=(pl.BlockSpec((1, hpg, tm, MLA_QK), omap), pl.BlockSpec((1, hpg, MLA_V, tm), vmap)),
        compiler_params=_params(("arbitrary", "arbitrary"), blocks, extra=2 * _nbytes((tm, gn), F32)),
        name="mla_kv_up",
    )(c, w3, kr)


def _mla_attn_kernel(q_ref, k_ref, vt_ref, o_ref, sa_ref, sb_ref, m_ref, l_ref, acc_ref, *, tk):
    i = pl.program_id(2)
    hp = q_ref.shape[1]
    dv = acc_ref.shape[1]
    m_ref[...] = jnp.full(m_ref.shape, -jnp.inf, F32)
    l_ref[...] = jnp.zeros(l_ref.shape, F32)
    acc_ref[...] = jnp.zeros(acc_ref.shape, F32)

    def scores(j, s_ref):
        start = pl.multiple_of(j * tk, tk)
        for hh in range(hp):
            k = k_ref[0, hh, pl.ds(start, tk), :]
            s_ref[hh] = lax.dot_general(k, q_ref[0, hh], (((1,), (1,)), ((), ())),
                                        preferred_element_type=F32)

    def update(j, s_ref, masked):
        start = pl.multiple_of(j * tk, tk)
        for hh in range(hp):
            st = s_ref[hh]
            if masked:
                kj = lax.broadcasted_iota(jnp.int32, st.shape, 0)
                qi = lax.broadcasted_iota(jnp.int32, st.shape, 1)
                st = jnp.where(kj <= qi, st, -jnp.inf)
            vt = vt_ref[0, hh, :, pl.ds(start, tk)]
            m_prev = m_ref[hh]
            m_new = jnp.maximum(m_prev, jnp.max(st, axis=0, keepdims=True))
            alpha = jnp.exp2(m_prev - m_new)
            pt = jnp.exp2(st - m_new)
            l_ref[hh] = alpha * l_ref[hh] + jnp.sum(pt, axis=0, keepdims=True)
            acc_ref[hh] = alpha * acc_ref[hh] + jnp.dot(vt, pt.astype(BF16), preferred_element_type=F32)
            m_ref[hh] = m_new

    def finish():
        for hh in range(hp):
            o = acc_ref[hh] / l_ref[hh]
            o_ref[0, :, hh * dv:(hh + 1) * dv] = o.T.astype(o_ref.dtype)

    scores(0, sa_ref)

    def pair(jj, carry):
        j = 2 * jj
        scores(j + 1, sb_ref)
        update(j, sa_ref, False)
        scores(j + 2, sa_ref)
        update(j + 1, sb_ref, False)
        return carry

    lax.fori_loop(0, i // 2, pair, 0)

    @pl.when(i % 2 == 0)
    def _():
        update(i, sa_ref, True)
        finish()

    @pl.when(i % 2 == 1)
    def _():
        scores(i, sb_ref)
        update(i - 1, sa_ref, False)
        update(i, sb_ref, True)
        finish()


def _mla_attention(q, k, vt, hp):
    batch, heads, seq, dqk = q.shape
    dv = vt.shape[2]
    tq = 512
    blocks = [((hp, tq, 256), BF16), ((hp, seq, 256), BF16), ((hp, dv, seq), BF16), ((tq, hp * dv), BF16)]
    return pl.pallas_call(
        functools.partial(_mla_attn_kernel, tk=tq),
        out_shape=jax.ShapeDtypeStruct((batch, seq, heads * dv), BF16),
        grid=(batch, heads // hp, seq // tq),
        in_specs=[pl.BlockSpec((1, hp, tq, dqk), lambda b, h, i: (b, h, i, 0)),
                  pl.BlockSpec((1, hp, seq, dqk), lambda b, h, i: (b, h, 0, 0)),
                  pl.BlockSpec((1, hp, dv, seq), lambda b, h, i: (b, h, 0, 0))],
        out_specs=pl.BlockSpec((1, tq, hp * dv), lambda b, h, i: (b, i, h)),
        scratch_shapes=[pltpu.VMEM((hp, tq, tq), F32), pltpu.VMEM((hp, tq, tq), F32),
                        pltpu.VMEM((hp, 1, tq), F32), pltpu.VMEM((hp, 1, tq), F32),
                        pltpu.VMEM((hp, dv, tq), F32)],
        compiler_params=_params(("arbitrary", "arbitrary", "arbitrary"), blocks,
                                extra=8 * hp * _nbytes((tq, tq), F32)),
        name="mla_attention",
    )(q, k, vt)


def _matmul_kernel(x_ref, w_ref, o_ref, *, scale):
    acc = lax.dot_general(x_ref[...], w_ref[...], (((1,), (0,)), ((), ())), preferred_element_type=F32)
    if scale is not None:
        acc = acc * scale
    o_ref[...] = acc.astype(o_ref.dtype)


def _matmul(x, w, out_dtype, tm, tn, scale=None, name="matmul"):
    m, k = x.shape
    n = w.shape[1]
    blocks = [((tm, k), x.dtype), ((k, tn), w.dtype), ((tm, tn), out_dtype)]
    return pl.pallas_call(
        functools.partial(_matmul_kernel, scale=scale),
        out_shape=jax.ShapeDtypeStruct((m, n), out_dtype),
        grid=(m // tm, n // tn),
        in_specs=[pl.BlockSpec((tm, k), lambda i, j: (i, 0)), pl.BlockSpec((k, tn), lambda i, j: (0, j))],
        out_specs=pl.BlockSpec((tm, tn), lambda i, j: (i, j)),
        compiler_params=_params(("arbitrary", "arbitrary"), blocks, extra=2 * _nbytes((tm, tn), F32)),
        name=name,
    )(x, w)


def _residual_norm_kernel(*refs, n_next):
    a_ref, h_ref, gp_ref = refs[:3]
    gn_refs = refs[3:3 + n_next]
    out_refs = refs[3 + n_next:]
    h_new = h_ref[...] + _rms(a_ref[...], gp_ref[...])
    out_refs[0][...] = h_new
    for gn_ref, o_ref in zip(gn_refs, out_refs[1:]):
        o_ref[...] = _rms(h_new, gn_ref[...]).astype(o_ref.dtype)


def _residual_norm(a, h, g_post, g_next, next_dtypes=()):
    m, d = a.shape
    n_next = len(g_next)
    assert len(next_dtypes) == n_next
    tm = 256
    row = lambda i: (i, 0)
    fixed = lambda i: (0, 0)
    out_shape = [jax.ShapeDtypeStruct((m, d), F32)] + [jax.ShapeDtypeStruct((m, d), t) for t in next_dtypes]
    out_specs = [pl.BlockSpec((tm, d), row)] * (1 + n_next)
    blocks = [((tm, d), F32)] * (3 + n_next)
    return pl.pallas_call(
        functools.partial(_residual_norm_kernel, n_next=n_next),
        out_shape=tuple(out_shape),
        grid=(m // tm,),
        in_specs=[pl.BlockSpec((tm, d), row), pl.BlockSpec((tm, d), row)]
        + [pl.BlockSpec((1, d), fixed)] * (1 + n_next),
        out_specs=tuple(out_specs),
        compiler_params=_params(("arbitrary",), blocks),
        name="residual_norm",
    )(a, h, g_post, *g_next)


def _ffn_up_kernel(x_ref, wg_ref, wv_ref, cwg_ref, cwv_ref, cbg_ref, cbv_ref, o_ref, carry_ref, *u_refs,
                   blocks_per_seq):
    i = pl.program_id(0)
    j = pl.program_id(1)
    tn = wg_ref.shape[1]
    halo = SUBLANES
    nchunks = len(u_refs)
    chunk = u_refs[0].shape[0] - halo
    seq_start = (i % blocks_per_seq) == 0

    @pl.when(seq_start)
    def _():
        u_refs[0][0:halo, :] = jnp.zeros((halo, 2 * tn), F32)

    @pl.when(jnp.logical_not(seq_start))
    def _():
        u_refs[0][0:halo, :] = carry_ref[j]

    def conv(u_ref, cols, cw_ref, cb_ref):
        acc = cw_ref[0:1, :] * u_ref[halo - 2:halo - 2 + chunk, cols]
        acc = acc + cw_ref[1:2, :] * u_ref[halo - 1:halo - 1 + chunk, cols]
        acc = acc + cw_ref[2:3, :] * u_ref[halo:halo + chunk, cols]
        return cb_ref[...] + acc

    def project(c):
        w = jnp.concatenate([wg_ref[...], wv_ref[...]], axis=1)
        u = lax.dot_general(x_ref[c * chunk:(c + 1) * chunk, :], w, (((1,), (0,)), ((), ())),
                            preferred_element_type=F32)
        u_refs[c][halo:halo + chunk, :] = u
        if c + 1 < nchunks:
            u_refs[c + 1][0:halo, :] = u[chunk - halo:, :]
        else:
            carry_ref[j] = u[chunk - halo:, :]

    project(0)
    for c in range(nchunks):
        if c + 1 < nchunks:
            project(c + 1)
        gate = conv(u_refs[c], slice(0, tn), cwg_ref, cbg_ref)
        val = conv(u_refs[c], slice(tn, 2 * tn), cwv_ref, cbv_ref)
        act = gate * (1.0 / (1.0 + jnp.exp(-gate))) * val
        o_ref[c * chunk:(c + 1) * chunk, :] = act.astype(o_ref.dtype)


def _ffn_up(hn, w_in, conv_w, conv_b, seq):
    m, d = hn.shape
    d_ff = w_in.shape[1] // 2
    tn = 256
    nj = d_ff // tn
    tm = 1024
    chunk = 256
    blocks = [((tm // 2, d), F32), ((d, tn), BF16), ((d, tn), BF16), ((tm, tn), BF16)]
    scratch = _nbytes((nj, SUBLANES, 2 * tn), F32) + _nbytes((SUBLANES + tm, 2 * tn), F32)
    gate_col = lambda i, j: (0, j)
    val_col = lambda i, j: (0, j + nj)
    return pl.pallas_call(
        functools.partial(_ffn_up_kernel, blocks_per_seq=seq // tm),
        out_shape=jax.ShapeDtypeStruct((m, d_ff), BF16),
        grid=(m // tm, nj),
        in_specs=[pl.BlockSpec((tm, d), lambda i, j: (i, 0), pipeline_mode=pl.Buffered(1)),
                  pl.BlockSpec((d, tn), gate_col), pl.BlockSpec((d, tn), val_col),
                  pl.BlockSpec((CONV_W, tn), gate_col), pl.BlockSpec((CONV_W, tn), val_col),
                  pl.BlockSpec((1, tn), gate_col), pl.BlockSpec((1, tn), val_col)],
        out_specs=pl.BlockSpec((tm, tn), lambda i, j: (i, j)),
        scratch_shapes=[pltpu.VMEM((nj, SUBLANES, 2 * tn), F32)]
        + [pltpu.VMEM((SUBLANES + chunk, 2 * tn), F32)] * (tm // chunk),
        compiler_params=_params(("arbitrary", "arbitrary"), blocks, extra=scratch + 8 * _nbytes((chunk, 2 * tn), F32)),
        name="ffn_up",
    )(hn, w_in, w_in, conv_w, conv_w, conv_b, conv_b)


def _swa_kv_kernel(x_ref, w_ref, k_ref, vt_ref):
    res = jnp.dot(x_ref[...], w_ref[...], preferred_element_type=F32)
    groups = k_ref.shape[0]
    hd = SWA_HEAD_DIM
    for t in range(groups):
        k_ref[t] = res[:, t * hd:(t + 1) * hd].astype(k_ref.dtype)
    vt = res[:, groups * hd:].T
    for t in range(groups):
        vt_ref[t] = vt[t * hd:(t + 1) * hd, :].astype(vt_ref.dtype)


def _swa_kv(x, w):
    m, k = x.shape
    n = w.shape[1]
    groups = n // (2 * SWA_HEAD_DIM)
    tm = 512
    blocks = [((tm, k), BF16), ((k, n), BF16), ((groups, tm, 128), BF16), ((groups, SWA_HEAD_DIM, tm), BF16)]
    return pl.pallas_call(
        _swa_kv_kernel,
        out_shape=(jax.ShapeDtypeStruct((groups, m, SWA_HEAD_DIM), BF16),
                   jax.ShapeDtypeStruct((groups, SWA_HEAD_DIM, m), BF16)),
        grid=(m // tm,),
        in_specs=[pl.BlockSpec((tm, k), lambda i: (i, 0)), pl.BlockSpec((k, n), lambda i: (0, 0))],
        out_specs=(pl.BlockSpec((groups, tm, SWA_HEAD_DIM), lambda i: (0, i, 0)),
                   pl.BlockSpec((groups, SWA_HEAD_DIM, tm), lambda i: (0, 0, i))),
        compiler_params=_params(("arbitrary",), blocks, extra=3 * _nbytes((tm, n), F32)),
        name="swa_kv",
    )(x, w)


def _swa_attn_kernel(bias_ref, sink_ref, q_ref, kp_ref, kc_ref, vtp_ref, vtc_ref, o_ref, *, rep):
    n = pl.program_id(2)
    w = WINDOW
    hd = SWA_HEAD_DIM
    nq = q_ref.shape[0] // w
    kfull = jnp.concatenate([kp_ref[0], kc_ref[0]], axis=0)
    vtfull = jnp.concatenate([vtp_ref[0], vtc_ref[0]], axis=1)
    sink = sink_ref[0]
    first = jnp.minimum(n, 1)
    for c in range(nq):
        qt = q_ref[c * w:(c + 1) * w, :].astype(F32).T
        qt = jnp.concatenate([qt[r * hd:(r + 1) * hd, :] for r in range(rep)], axis=1).astype(BF16)
        bias = bias_ref[0, first] if c == 0 else bias_ref[0, 1]
        st = jnp.dot(kfull[c * w:(c + 2) * w], qt, preferred_element_type=F32) + bias
        mx = jnp.maximum(jnp.max(st, axis=0, keepdims=True), sink)
        e = jnp.exp2(st - mx)
        denom = jnp.sum(e, axis=0, keepdims=True) + jnp.exp2(sink - mx)
        ot = jnp.dot(vtfull[:, c * w:(c + 2) * w], e.astype(BF16), preferred_element_type=F32)
        ot = ot * (1.0 / denom)
        o = jnp.concatenate([ot[:, r * w:(r + 1) * w] for r in range(rep)], axis=0)
        o_ref[c * w:(c + 1) * w, :] = o.T.astype(o_ref.dtype)


def _swa_attention(q, k, vt, bias, sink_lanes, batch, seq):
    m, dq = q.shape
    groups = k.shape[0]
    rep = dq // (groups * SWA_HEAD_DIM)
    w = WINDOW
    nq = 4
    tq = nq * w
    nb = seq // tq
    gw = rep * SWA_HEAD_DIM
    prev_blk = lambda b, n: jnp.maximum((b * nb + n) * nq - 1, b * nb * nq)
    blocks = [((2, 2 * w, rep * w), F32), ((tq, gw), BF16), ((tq, gw), BF16)]
    return pl.pallas_call(
        functools.partial(_swa_attn_kernel, rep=rep),
        out_shape=jax.ShapeDtypeStruct((m, dq), BF16),
        grid=(groups, batch, nb),
        in_specs=[pl.BlockSpec((1, 2, 2 * w, rep * w), lambda g, b, n: (g, 0, 0, 0)),
                  pl.BlockSpec((1, 1, rep * w), lambda g, b, n: (g, 0, 0)),
                  pl.BlockSpec((tq, gw), lambda g, b, n: (b * nb + n, g)),
                  pl.BlockSpec((1, w, SWA_HEAD_DIM), lambda g, b, n: (g, prev_blk(b, n), 0)),
                  pl.BlockSpec((1, tq, SWA_HEAD_DIM), lambda g, b, n: (g, b * nb + n, 0)),
                  pl.BlockSpec((1, SWA_HEAD_DIM, w), lambda g, b, n: (g, 0, prev_blk(b, n))),
                  pl.BlockSpec((1, SWA_HEAD_DIM, tq), lambda g, b, n: (g, 0, b * nb + n))],
        out_specs=pl.BlockSpec((tq, gw), lambda g, b, n: (b * nb + n, g)),
        compiler_params=_params(("arbitrary", "arbitrary", "arbitrary"), blocks,
                                extra=6 * nq * _nbytes((2 * w, rep * w), F32)),
        name="swa_attention",
    )(bias, sink_lanes, q, k, k, vt, vt)


def _swa_bias(swa_heads, groups):
    w = WINDOW
    rep = swa_heads // groups
    slopes = 2.0 ** (-8.0 * jnp.arange(1, swa_heads + 1, dtype=F32) / swa_heads)
    kj = jnp.arange(2 * w)[:, None]
    qi = jnp.arange(w)[None, :]
    dist = qi + w - kj
    in_window = (dist >= 0) & (dist < w)
    valid = jnp.stack([in_window & (kj >= w), in_window])
    alibi = slopes.reshape(groups, 1, 1, rep, 1) * dist.astype(F32)[None, None, :, None, :]
    bias = jnp.where(valid[None, :, :, None, :], -alibi * LOG2_E, -jnp.inf)
    return bias.reshape(groups, 2, 2 * w, rep * w)


def kernel(x, positions, norm_mix_pre, norm_mix_post, norm_ffn_pre, norm_ffn_post, mla_w_dq, mla_q_norm,
           mla_w_uq, mla_w_dkv, mla_kv_norm, mla_w_ukv, mla_w_o, shared_kv_norm, swa_w_k, swa_w_v, swa_w_q,
           swa_sinks, swa_w_o, ffn_w_in, ffn_conv_w, ffn_conv_b, ffn_w_out):
    batch, seq, d = x.shape
    m = batch * seq
    depth = norm_mix_pre.shape[0]
    n_a = mla_w_dq.shape[0]
    heads = mla_w_o.shape[1] // MLA_V
    swa_heads = swa_w_q.shape[2] // SWA_HEAD_DIM
    hpg = 4

    half = MLA_ROPE // 2
    freqs = ROPE_THETA ** (-jnp.arange(half, dtype=F32) / half)
    freqs2 = jnp.concatenate([freqs, freqs])[None, :]
    cosf, sinf = _rope_tables(positions.reshape(m, 1), freqs2)

    row = lambda v: v.reshape(1, -1)
    h = x.reshape(m, d)
    assert depth == 2 and n_a == 1, "wired for one MLA layer followed by one sliding-window layer"

    def ffn(l, h, hn_ffn, g_next, next_dtypes):
        act = _ffn_up(hn_ffn, ffn_w_in[l].astype(BF16), ffn_conv_w[l], row(ffn_conv_b[l]), seq)
        f = _matmul(act, ffn_w_out[l].astype(BF16), F32, 512, 256, name="ffn_down")
        return _residual_norm(f, h, row(norm_ffn_post[l]), g_next, next_dtypes)

    w_uq = mla_w_uq[0].astype(BF16)
    w_uq3 = w_uq.reshape(w_uq.shape[0], heads // hpg, hpg * MLA_QK).transpose(1, 0, 2)
    w_ukv = mla_w_ukv[0].astype(BF16)
    w_ukv3 = w_ukv.reshape(w_ukv.shape[0], heads // hpg, hpg * (MLA_NOPE + MLA_V)).transpose(1, 0, 2)
    cq, c, kr = _mla_down(h, row(norm_mix_pre[0]), mla_w_dq[0].astype(BF16), row(mla_q_norm[0]),
                          mla_w_dkv[0].astype(BF16), row(mla_kv_norm[0]), cosf, sinf)
    q = _mla_q_up(cq, w_uq3, cosf, sinf, batch, seq, MLA_QK ** -0.5 * LOG2_E)
    k, vt = _mla_kv_up(c, w_ukv3, kr, batch, seq)
    o = _mla_attention(q, k, vt, hp=4).reshape(m, heads * MLA_V)
    a = _matmul(o, mla_w_o[0].astype(BF16), F32, 1024, 512, name="mla_out")
    h, hn_ffn = _residual_norm(a, h, row(norm_mix_post[0]), [row(norm_ffn_pre[0])], (F32,))
    h, hn_mix, hn_kv = ffn(0, h, hn_ffn, [row(norm_mix_pre[1]), row(shared_kv_norm)], (BF16, BF16))

    w_kv = jnp.concatenate([swa_w_k, swa_w_v], axis=1).astype(BF16)
    k_shared, vt_shared = _swa_kv(hn_kv, w_kv)
    groups = k_shared.shape[0]
    sink_lanes = jnp.repeat(swa_sinks[0].reshape(groups, 1, swa_heads // groups) * LOG2_E, WINDOW, axis=2)
    q = _matmul(hn_mix, swa_w_q[0].astype(BF16), BF16, 1024, 512,
                scale=SWA_HEAD_DIM ** -0.5 * LOG2_E, name="swa_q")
    o = _swa_attention(q, k_shared, vt_shared, _swa_bias(swa_heads, groups), sink_lanes, batch, seq)
    a = _matmul(o, swa_w_o[0].astype(BF16), F32, 1024, 512, name="swa_out")
    h, hn_ffn = _residual_norm(a, h, row(norm_mix_post[1]), [row(norm_ffn_pre[1])], (F32,))
    (h,) = ffn(1, h, hn_ffn, [], ())
    return h.reshape(batch, seq, d)
```

```python
import functools

import jax
import jax.numpy as jnp
from jax import lax
from jax.experimental import pallas as pl
from jax.experimental.pallas import tpu as pltpu

EPS = 1e-6
LOG2_E = 1.4426950408889634
ROPE_THETA = 10000.0
MLA_NOPE = 128
MLA_ROPE = 64
MLA_V = 128
MLA_QK = MLA_NOPE + MLA_ROPE
SWA_HEAD_DIM = 64
SWA_KV_HEADS = 8
WINDOW = 128
CONV_W = 3

V7X_VMEM_BYTES = 64 * 1024 * 1024
VMEM_LIMIT_CAP = V7X_VMEM_BYTES - 8 * 1024 * 1024
SUBLANES = 8
PERM_UNIT = 256
PERM_GROUPS = PERM_UNIT // SUBLANES

BF16 = jnp.bfloat16
F32 = jnp.float32


def _nbytes(shape, dtype):
    n = 1
    for s in shape:
        n *= s
    return n * jnp.dtype(dtype).itemsize


def _params(semantics, blocks, extra=0):
    est = 2 * sum(_nbytes(s, d) for s, d in blocks) + extra
    limit = min(VMEM_LIMIT_CAP, max(est + est // 4, 16 * 1024 * 1024))
    return pltpu.CompilerParams(dimension_semantics=semantics, vmem_limit_bytes=limit)


def _rms(x, g):
    r = lax.rsqrt(jnp.mean(x * x, axis=-1, keepdims=True) + EPS)
    return (x * r) * g


def _swap_halves(r):
    half = r.shape[-1] // 2
    return jnp.concatenate([r[:, half:], r[:, :half]], axis=-1)


def _rope(r, cosf, sinf):
    return r * cosf + _swap_halves(r) * sinf


def _rope_table_kernel(pos_ref, freq_ref, cos_ref, sin_ref):
    ang = pos_ref[...].astype(F32) * freq_ref[...]
    half = ang.shape[-1] // 2
    lane = lax.broadcasted_iota(jnp.int32, ang.shape, 1)
    s = jnp.sin(ang)
    cos_ref[...] = jnp.cos(ang)
    sin_ref[...] = jnp.where(lane < half, -s, s)


def _rope_tables(pos_col, freqs2):
    m = pos_col.shape[0]
    d = freqs2.shape[1]
    tm = 1024
    return pl.pallas_call(
        _rope_table_kernel,
        out_shape=(jax.ShapeDtypeStruct((m, d), F32), jax.ShapeDtypeStruct((m, d), F32)),
        grid=(m // tm,),
        in_specs=[pl.BlockSpec((tm, 1), lambda i: (i, 0)), pl.BlockSpec((1, d), lambda i: (0, 0))],
        out_specs=(pl.BlockSpec((tm, d), lambda i: (i, 0)), pl.BlockSpec((tm, d), lambda i: (i, 0))),
        name="rope_tables",
    )(pos_col, freqs2)


def _mla_down_kernel(x_ref, g_ref, wdq_ref, qn_ref, wdkv_ref, kvn_ref, cos_ref, sin_ref,
                     cq_ref, c_ref, kr_ref):
    hn = _rms(x_ref[...], g_ref[...]).astype(BF16)
    cq = jnp.dot(hn, wdq_ref[...], preferred_element_type=F32)
    cq_ref[...] = _rms(cq, qn_ref[...]).astype(BF16)
    ckv = jnp.dot(hn, wdkv_ref[...], preferred_element_type=F32)
    lora = c_ref.shape[-1]
    c_ref[...] = _rms(ckv[:, :lora], kvn_ref[...]).astype(BF16)
    kr_ref[...] = _rope(ckv[:, lora:], cos_ref[...], sin_ref[...]).astype(BF16)


def _mla_down(x2, g, wdq, qn, wdkv, kvn, cosf, sinf):
    m, d = x2.shape
    nq = wdq.shape[1]
    nkv = wdkv.shape[1]
    lora = kvn.shape[1]
    rd = nkv - lora
    tm = 256
    row = lambda i: (i, 0)
    fixed = lambda i: (0, 0)
    blocks = [((tm, d), F32), ((d, nq), BF16), ((d, nkv), BF16), ((tm, nq), BF16), ((tm, lora), BF16)]
    return pl.pallas_call(
        _mla_down_kernel,
        out_shape=(jax.ShapeDtypeStruct((m, nq), BF16), jax.ShapeDtypeStruct((m, lora), BF16),
                   jax.ShapeDtypeStruct((m, rd), BF16)),
        grid=(m // tm,),
        in_specs=[pl.BlockSpec((tm, d), row), pl.BlockSpec((1, d), fixed),
                  pl.BlockSpec((d, nq), fixed), pl.BlockSpec((1, nq), fixed),
                  pl.BlockSpec((d, nkv), fixed), pl.BlockSpec((1, lora), fixed),
                  pl.BlockSpec((tm, rd), row), pl.BlockSpec((tm, rd), row)],
        out_specs=(pl.BlockSpec((tm, nq), row), pl.BlockSpec((tm, lora), row), pl.BlockSpec((tm, rd), row)),
        compiler_params=_params(("arbitrary",), blocks, extra=4 * _nbytes((tm, d), F32)),
        name="mla_down",
    )(x2, g, wdq, qn, wdkv, kvn, cosf, sinf)


def _mla_q_up_kernel(cq_ref, w_ref, cos_ref, sin_ref, q_ref, *, scale):
    res = jnp.dot(cq_ref[...], w_ref[0], preferred_element_type=F32)
    cosf = cos_ref[...]
    sinf = sin_ref[...]
    for hh in range(q_ref.shape[1]):
        base = hh * MLA_QK
        nope = res[:, base:base + MLA_NOPE]
        r = _rope(res[:, base + MLA_NOPE:base + MLA_QK], cosf, sinf)
        q_ref[0, hh, :, :MLA_NOPE] = (nope * scale).astype(BF16)
        q_ref[0, hh, :, MLA_NOPE:] = (r * scale).astype(BF16)


def _mla_q_up(cq, w3, cosf, sinf, batch, seq, scale):
    m, k = cq.shape
    groups, _, gn = w3.shape
    hpg = gn // MLA_QK
    heads = groups * hpg
    tm = 512
    nsb = seq // tm
    blocks = [((tm, k), BF16), ((k, gn), BF16), ((hpg, tm, MLA_QK), BF16)]
    return pl.pallas_call(
        functools.partial(_mla_q_up_kernel, scale=scale),
        out_shape=jax.ShapeDtypeStruct((batch, heads, seq, MLA_QK), BF16),
        grid=(m // tm, groups),
        in_specs=[pl.BlockSpec((tm, k), lambda i, g: (i, 0)),
                  pl.BlockSpec((1, k, gn), lambda i, g: (g, 0, 0)),
                  pl.BlockSpec((tm, MLA_ROPE), lambda i, g: (i, 0)),
                  pl.BlockSpec((tm, MLA_ROPE), lambda i, g: (i, 0))],
        out_specs=pl.BlockSpec((1, hpg, tm, MLA_QK), lambda i, g: (i // nsb, g, i % nsb, 0)),
        compiler_params=_params(("arbitrary", "arbitrary"), blocks, extra=2 * _nbytes((tm, gn), F32)),
        name="mla_q_up",
    )(cq, w3, cosf, sinf)


def _mla_kv_up_kernel(c_ref, w_ref, kr_ref, k_ref, vt_ref):
    res = jnp.dot(c_ref[...], w_ref[0], preferred_element_type=F32)
    kr = kr_ref[...]
    per = MLA_NOPE + MLA_V
    for hh in range(k_ref.shape[1]):
        base = hh * per
        k_ref[0, hh, :, :MLA_NOPE] = res[:, base:base + MLA_NOPE].astype(BF16)
        k_ref[0, hh, :, MLA_NOPE:] = kr
        vt_ref[0, hh] = res[:, base + MLA_NOPE:base + per].T.astype(BF16)


def _mla_kv_up(c, w3, kr, batch, seq):
    m, k = c.shape
    groups, _, gn = w3.shape
    per = MLA_NOPE + MLA_V
    hpg = gn // per
    heads = groups * hpg
    tm = 512
    nsb = seq // tm
    blocks = [((tm, k), BF16), ((k, gn), BF16), ((hpg, tm, MLA_QK), BF16), ((hpg, tm, MLA_V), BF16)]
    omap = lambda i, g: (i // nsb, g, i % nsb, 0)
    vmap = lambda i, g: (i // nsb, g, 0, i % nsb)
    return pl.pallas_call(
        _mla_kv_up_kernel,
        out_shape=(jax.ShapeDtypeStruct((batch, heads, seq, MLA_QK), BF16),
                   jax.ShapeDtypeStruct((batch, heads, MLA_V, seq), BF16)),
        grid=(m // tm, groups),
        in_specs=[pl.BlockSpec((tm, k), lambda i, g: (i, 0)),
                  pl.BlockSpec((1, k, gn), lambda i, g: (g, 0, 0)),
                  pl.BlockSpec((tm, MLA_ROPE), lambda i, g: (i, 0))],
        out_specs=(pl.BlockSpec((1, hpg, tm, MLA_QK), omap), pl.BlockSpec((1, hpg, MLA_V, tm), vmap)),
        compiler_params=_params(("arbitrary", "arbitrary"), blocks, extra=2 * _nbytes((tm, gn), F32)),
        name="mla_kv_up",
    )(c, w3, kr)


def _mla_attn_kernel(q_ref, k_ref, vt_ref, o_ref, sa_ref, sb_ref, m_ref, l_ref, acc_ref, *, tk):
    i = pl.program_id(2)
    hp = q_ref.shape[1]
    dv = acc_ref.shape[1]
    m_ref[...] = jnp.full(m_ref.shape, -jnp.inf, F32)
    l_ref[...] = jnp.zeros(l_ref.shape, F32)
    acc_ref[...] = jnp.zeros(acc_ref.shape, F32)

    def scores(j, s_ref):
        start = pl.multiple_of(j * tk, tk)
        for hh in range(hp):
            k = k_ref[0, hh, pl.ds(start, tk), :]
            s_ref[hh] = lax.dot_general(k, q_ref[0, hh], (((1,), (1,)), ((), ())),
                                        preferred_element_type=F32)

    def update(j, s_ref, masked):
        start = pl.multiple_of(j * tk, tk)
        for hh in range(hp):
            st = s_ref[hh]
            if masked:
                kj = lax.broadcasted_iota(jnp.int32, st.shape, 0)
                qi = lax.broadcasted_iota(jnp.int32, st.shape, 1)
                st = jnp.where(kj <= qi, st, -jnp.inf)
            vt = vt_ref[0, hh, :, pl.ds(start, tk)]
            m_prev = m_ref[hh]
            m_new = jnp.maximum(m_prev, jnp.max(st, axis=0, keepdims=True))
            alpha = jnp.exp2(m_prev - m_new)
            pt = jnp.exp2(st - m_new)
            l_ref[hh] = alpha * l_ref[hh] + jnp.sum(pt, axis=0, keepdims=True)
            acc_ref[hh] = alpha * acc_ref[hh] + jnp.dot(vt, pt.astype(BF16), preferred_element_type=F32)
            m_ref[hh] = m_new

    def finish():
        for hh in range(hp):
            o = acc_ref[hh] / l_ref[hh]
            o_ref[0, :, hh * dv:(hh + 1) * dv] = o.T.astype(o_ref.dtype)

    scores(0, sa_ref)

    def pair(jj, carry):
        j = 2 * jj
        scores(j + 1, sb_ref)
        update(j, sa_ref, False)
        scores(j + 2, sa_ref)
        update(j + 1, sb_ref, False)
        return carry

    lax.fori_loop(0, i // 2, pair, 0)

    @pl.when(i % 2 == 0)
    def _():
        update(i, sa_ref, True)
        finish()

    @pl.when(i % 2 == 1)
    def _():
        scores(i, sb_ref)
        update(i - 1, sa_ref, False)
        update(i, sb_ref, True)
        finish()


def _mla_attention(q, k, vt, hp):
    batch, heads, seq, dqk = q.shape
    dv = vt.shape[2]
    tq = 512
    blocks = [((hp, tq, 256), BF16), ((hp, seq, 256), BF16), ((hp, dv, seq), BF16), ((tq, hp * dv), BF16)]
    return pl.pallas_call(
        functools.partial(_mla_attn_kernel, tk=tq),
        out_shape=jax.ShapeDtypeStruct((batch, seq, heads * dv), BF16),
        grid=(batch, heads // hp, seq // tq),
        in_specs=[pl.BlockSpec((1, hp, tq, dqk), lambda b, h, i: (b, h, i, 0)),
                  pl.BlockSpec((1, hp, seq, dqk), lambda b, h, i: (b, h, 0, 0)),
                  pl.BlockSpec((1, hp, dv, seq), lambda b, h, i: (b, h, 0, 0))],
        out_specs=pl.BlockSpec((1, tq, hp * dv), lambda b, h, i: (b, i, h)),
        scratch_shapes=[pltpu.VMEM((hp, tq, tq), F32), pltpu.VMEM((hp, tq, tq), F32),
                        pltpu.VMEM((hp, 1, tq), F32), pltpu.VMEM((hp, 1, tq), F32),
                        pltpu.VMEM((hp, dv, tq), F32)],
        compiler_params=_params(("arbitrary", "arbitrary", "arbitrary"), blocks,
                                extra=8 * hp * _nbytes((tq, tq), F32)),
        name="mla_attention",
    )(q, k, vt)


def _matmul_kernel(x_ref, w_ref, o_ref, *, scale):
    acc = jnp.dot(x_ref[...], w_ref[...], preferred_element_type=F32)
    if scale is not None:
        acc = acc * scale
    o_ref[...] = acc.astype(o_ref.dtype)


def _matmul(x, w, out_dtype, tm, tn, scale=None, name="matmul"):
    m, k = x.shape
    n = w.shape[1]
    blocks = [((tm, k), x.dtype), ((k, tn), w.dtype), ((tm, tn), out_dtype)]
    return pl.pallas_call(
        functools.partial(_matmul_kernel, scale=scale),
        out_shape=jax.ShapeDtypeStruct((m, n), out_dtype),
        grid=(m // tm, n // tn),
        in_specs=[pl.BlockSpec((tm, k), lambda i, j: (i, 0)), pl.BlockSpec((k, tn), lambda i, j: (0, j))],
        out_specs=pl.BlockSpec((tm, tn), lambda i, j: (i, j)),
        compiler_params=_params(("arbitrary", "arbitrary"), blocks, extra=2 * _nbytes((tm, tn), F32)),
        name=name,
    )(x, w)


def _residual_norm_kernel(*refs, n_next, interleave):
    a_ref, h_ref, gp_ref = refs[:3]
    gn_refs = refs[3:3 + n_next]
    perm_ref = refs[3 + n_next] if interleave else None
    out_refs = refs[3 + n_next + (1 if interleave else 0):]
    h_new = h_ref[...] + _rms(a_ref[...], gp_ref[...])
    out_refs[0][...] = h_new
    for gn_ref, o_ref in zip(gn_refs, out_refs[1:]):
        hn = _rms(h_new, gn_ref[...]).astype(o_ref.dtype)
        if interleave:
            hn = jnp.dot(perm_ref[...], hn, preferred_element_type=F32).astype(o_ref.dtype)
        o_ref[...] = hn


def _residual_norm(a, h, g_post, g_next, perm=None):
    m, d = a.shape
    n_next = len(g_next)
    tm = PERM_UNIT
    row = lambda i: (i, 0)
    fixed = lambda i: (0, 0)
    interleave = perm is not None
    out_shape = [jax.ShapeDtypeStruct((m, d), F32)] + [jax.ShapeDtypeStruct((m, d), BF16)] * n_next
    out_specs = [pl.BlockSpec((tm, d), row)] * (1 + n_next)
    blocks = [((tm, d), F32)] * (4 + n_next)
    perm_specs = [pl.BlockSpec((tm, tm), fixed)] if interleave else []
    perm_args = [perm] if interleave else []
    return pl.pallas_call(
        functools.partial(_residual_norm_kernel, n_next=n_next, interleave=interleave),
        out_shape=tuple(out_shape),
        grid=(m // tm,),
        in_specs=[pl.BlockSpec((tm, d), row), pl.BlockSpec((tm, d), row)]
        + [pl.BlockSpec((1, d), fixed)] * (1 + n_next) + perm_specs,
        out_specs=tuple(out_specs),
        compiler_params=_params(("arbitrary",), blocks),
        name="residual_norm",
    )(a, h, g_post, *g_next, *perm_args)


def _interleave_permutation():
    r = jnp.arange(PERM_UNIT)
    t = (r % SUBLANES) * PERM_GROUPS + r // SUBLANES
    return (t[:, None] == jnp.arange(PERM_UNIT)[None, :]).astype(BF16)


def _ffn_up_kernel(x_ref, wg_ref, wv_ref, cwg_ref, cwv_ref, cbg_ref, cbv_ref, wdn_ref, pt_ref, o_ref,
                   wdn_bf_ref, w_ref, carry_ref, *, blocks_per_seq):
    i = pl.program_id(1)
    tn = wg_ref.shape[1]
    tm = x_ref.shape[0]
    grp = SUBLANES

    @pl.when(i == 0)
    def _():
        w_ref[:, :tn] = wg_ref[...].astype(BF16)
        w_ref[:, tn:] = wv_ref[...].astype(BF16)
        wdn_bf_ref[...] = wdn_ref[...].astype(BF16)

    @pl.when((i % blocks_per_seq) == 0)
    def _():
        carry_ref[...] = jnp.zeros(carry_ref.shape, F32)

    cw = jnp.concatenate([cwg_ref[...], cwv_ref[...]], axis=1)
    cb = jnp.concatenate([cbg_ref[...], cbv_ref[...]], axis=1)
    first_sublane = lax.broadcasted_iota(jnp.int32, (grp, 2 * tn), 0) == 0
    tail = carry_ref[...]

    def project(c):
        return jnp.dot(x_ref[c * PERM_UNIT:(c + 1) * PERM_UNIT, :], w_ref[...], preferred_element_type=F32)

    nunits = tm // PERM_UNIT
    u_next = project(0)
    for c in range(nunits):
        rows = slice(c * PERM_UNIT, (c + 1) * PERM_UNIT)
        u = u_next
        if c + 1 < nunits:
            u_next = project(c + 1)
        new_tail = u[PERM_UNIT - 2 * grp:, :]
        wrapped = jnp.where(jnp.concatenate([first_sublane, first_sublane], axis=0),
                            jnp.concatenate([pltpu.roll(tail[:grp], 1, 0), pltpu.roll(tail[grp:], 1, 0)], axis=0),
                            jnp.concatenate([pltpu.roll(new_tail[:grp], 1, 0), pltpu.roll(new_tail[grp:], 1, 0)],
                                            axis=0))
        ext = jnp.concatenate([wrapped, u], axis=0)
        acc = cw[0:1, :] * ext[0:PERM_UNIT] + cw[1:2, :] * ext[grp:grp + PERM_UNIT] + cw[2:3, :] * u
        conv = cb + acc
        gate, val = conv[:, :tn], conv[:, tn:]
        act = (gate * (1.0 / (1.0 + jnp.exp(-gate))) * val).astype(o_ref.dtype)
        o_ref[rows, :] = jnp.dot(pt_ref[...], act, preferred_element_type=F32).astype(o_ref.dtype)
        tail = new_tail
    carry_ref[...] = tail


def _ffn_up(hn, w_in_all, w_out_all, layer, conv_w, conv_b, perm_t, seq):
    m, d = hn.shape
    d_ff = w_in_all.shape[2] // 2
    d_out = w_out_all.shape[2]
    tn = 256
    nj = d_ff // tn
    tm = 1024
    blocks = [((tm, d), BF16), ((d, tn), F32), ((d, tn), F32), ((tm, tn), BF16),
              ((tn, d_out), F32), ((tn, d_out), BF16)]
    scratch = _nbytes((d, 2 * tn), BF16)
    gate_col = lambda j, i: (0, j)
    val_col = lambda j, i: (0, j + nj)
    return pl.pallas_call(
        functools.partial(_ffn_up_kernel, blocks_per_seq=seq // tm),
        out_shape=(jax.ShapeDtypeStruct((m, d_ff), BF16), jax.ShapeDtypeStruct((d_ff, d_out), BF16)),
        grid=(nj, m // tm),
        in_specs=[pl.BlockSpec((tm, d), lambda j, i: (i, 0)),
                  pl.BlockSpec((None, d, tn), lambda j, i: (layer, 0, j)),
                  pl.BlockSpec((None, d, tn), lambda j, i: (layer, 0, j + nj)),
                  pl.BlockSpec((CONV_W, tn), gate_col), pl.BlockSpec((CONV_W, tn), val_col),
                  pl.BlockSpec((1, tn), gate_col), pl.BlockSpec((1, tn), val_col),
                  pl.BlockSpec((None, tn, d_out), lambda j, i: (layer, j, 0)),
                  pl.BlockSpec((PERM_UNIT, PERM_UNIT), lambda j, i: (0, 0))],
        out_specs=(pl.BlockSpec((tm, tn), lambda j, i: (i, j)), pl.BlockSpec((tn, d_out), lambda j, i: (j, 0))),
        scratch_shapes=[pltpu.VMEM((d, 2 * tn), BF16), pltpu.VMEM((2 * SUBLANES, 2 * tn), F32)],
        compiler_params=_params(("arbitrary", "arbitrary"), blocks,
                                extra=scratch + 10 * _nbytes((PERM_UNIT, 2 * tn), F32)),
        name="ffn_up",
    )(hn, w_in_all, w_in_all, conv_w, conv_w, conv_b, conv_b, w_out_all, perm_t)


def _swa_kv_kernel(x_ref, w_ref, k_ref, vt_ref):
    res = jnp.dot(x_ref[...], w_ref[...], preferred_element_type=F32)
    groups = k_ref.shape[0]
    hd = SWA_HEAD_DIM
    for t in range(groups):
        k_ref[t] = res[:, t * hd:(t + 1) * hd].astype(k_ref.dtype)
    vt = res[:, groups * hd:].T
    for t in range(groups):
        vt_ref[t] = vt[t * hd:(t + 1) * hd, :].astype(vt_ref.dtype)


def _swa_kv(x, w):
    m, k = x.shape
    n = w.shape[1]
    groups = n // (2 * SWA_HEAD_DIM)
    tm = 512
    blocks = [((tm, k), BF16), ((k, n), BF16), ((groups, tm, 128), BF16), ((groups, SWA_HEAD_DIM, tm), BF16)]
    return pl.pallas_call(
        _swa_kv_kernel,
        out_shape=(jax.ShapeDtypeStruct((groups, m, SWA_HEAD_DIM), BF16),
                   jax.ShapeDtypeStruct((groups, SWA_HEAD_DIM, m), BF16)),
        grid=(m // tm,),
        in_specs=[pl.BlockSpec((tm, k), lambda i: (i, 0)), pl.BlockSpec((k, n), lambda i: (0, 0))],
        out_specs=(pl.BlockSpec((groups, tm, SWA_HEAD_DIM), lambda i: (0, i, 0)),
                   pl.BlockSpec((groups, SWA_HEAD_DIM, tm), lambda i: (0, 0, i))),
        compiler_params=_params(("arbitrary",), blocks, extra=3 * _nbytes((tm, n), F32)),
        name="swa_kv",
    )(x, w)


def _swa_attn_kernel(bias_ref, sink_ref, q_ref, kp_ref, kc_ref, vtp_ref, vtc_ref, o_ref, *, rep):
    n = pl.program_id(2)
    w = WINDOW
    hd = SWA_HEAD_DIM
    nq = q_ref.shape[0] // w
    kfull = jnp.concatenate([kp_ref[0], kc_ref[0]], axis=0)
    vtfull = jnp.concatenate([vtp_ref[0], vtc_ref[0]], axis=1)
    sink = sink_ref[0]
    first = jnp.minimum(n, 1)
    for c in range(nq):
        qt = q_ref[c * w:(c + 1) * w, :].astype(F32).T
        qt = jnp.concatenate([qt[r * hd:(r + 1) * hd, :] for r in range(rep)], axis=1).astype(BF16)
        bias = bias_ref[0, first] if c == 0 else bias_ref[0, 1]
        st = jnp.dot(kfull[c * w:(c + 2) * w], qt, preferred_element_type=F32) + bias
        mx = jnp.maximum(jnp.max(st, axis=0, keepdims=True), sink)
        e = jnp.exp2(st - mx)
        denom = jnp.sum(e, axis=0, keepdims=True) + jnp.exp2(sink - mx)
        ot = jnp.dot(vtfull[:, c * w:(c + 2) * w], e.astype(BF16), preferred_element_type=F32)
        ot = ot * (1.0 / denom)
        o = jnp.concatenate([ot[:, r * w:(r + 1) * w] for r in range(rep)], axis=0)
        o_ref[c * w:(c + 1) * w, :] = o.T.astype(o_ref.dtype)


def _swa_attention(q, k, vt, bias, sink_lanes, batch, seq):
    m, dq = q.shape
    groups = k.shape[0]
    rep = dq // (groups * SWA_HEAD_DIM)
    w = WINDOW
    nq = 4
    tq = nq * w
    nb = seq // tq
    gw = rep * SWA_HEAD_DIM
    prev_blk = lambda b, n: jnp.maximum((b * nb + n) * nq - 1, b * nb * nq)
    blocks = [((2, 2 * w, rep * w), F32), ((tq, gw), BF16), ((tq, gw), BF16)]
    return pl.pallas_call(
        functools.partial(_swa_attn_kernel, rep=rep),
        out_shape=jax.ShapeDtypeStruct((m, dq), BF16),
        grid=(groups, batch, nb),
        in_specs=[pl.BlockSpec((1, 2, 2 * w, rep * w), lambda g, b, n: (g, 0, 0, 0)),
                  pl.BlockSpec((1, 1, rep * w), lambda g, b, n: (g, 0, 0)),
                  pl.BlockSpec((tq, gw), lambda g, b, n: (b * nb + n, g)),
                  pl.BlockSpec((1, w, SWA_HEAD_DIM), lambda g, b, n: (g, prev_blk(b, n), 0)),
                  pl.BlockSpec((1, tq, SWA_HEAD_DIM), lambda g, b, n: (g, b * nb + n, 0)),
                  pl.BlockSpec((1, SWA_HEAD_DIM, w), lambda g, b, n: (g, 0, prev_blk(b, n))),
                  pl.BlockSpec((1, SWA_HEAD_DIM, tq), lambda g, b, n: (g, 0, b * nb + n))],
        out_specs=pl.BlockSpec((tq, gw), lambda g, b, n: (b * nb + n, g)),
        compiler_params=_params(("arbitrary", "arbitrary", "arbitrary"), blocks,
                                extra=6 * nq * _nbytes((2 * w, rep * w), F32)),
        name="swa_attention",
    )(bias, sink_lanes, q, k, k, vt, vt)


def _swa_bias(swa_heads, groups):
    w = WINDOW
    rep = swa_heads // groups
    slopes = 2.0 ** (-8.0 * jnp.arange(1, swa_heads + 1, dtype=F32) / swa_heads)
    kj = jnp.arange(2 * w)[:, None]
    qi = jnp.arange(w)[None, :]
    dist = qi + w - kj
    in_window = (dist >= 0) & (dist < w)
    valid = jnp.stack([in_window & (kj >= w), in_window])
    alibi = slopes.reshape(groups, 1, 1, rep, 1) * dist.astype(F32)[None, None, :, None, :]
    bias = jnp.where(valid[None, :, :, None, :], -alibi * LOG2_E, -jnp.inf)
    return bias.reshape(groups, 2, 2 * w, rep * w)


def kernel(x, positions, norm_mix_pre, norm_mix_post, norm_ffn_pre, norm_ffn_post, mla_w_dq, mla_q_norm,
           mla_w_uq, mla_w_dkv, mla_kv_norm, mla_w_ukv, mla_w_o, shared_kv_norm, swa_w_k, swa_w_v, swa_w_q,
           swa_sinks, swa_w_o, ffn_w_in, ffn_conv_w, ffn_conv_b, ffn_w_out):
    batch, seq, d = x.shape
    m = batch * seq
    depth = norm_mix_pre.shape[0]
    n_a = mla_w_dq.shape[0]
    heads = mla_w_o.shape[1] // MLA_V
    swa_heads = swa_w_q.shape[2] // SWA_HEAD_DIM
    hpg = 4

    half = MLA_ROPE // 2
    freqs = ROPE_THETA ** (-jnp.arange(half, dtype=F32) / half)
    freqs2 = jnp.concatenate([freqs, freqs])[None, :]
    cosf, sinf = _rope_tables(positions.reshape(m, 1), freqs2)

    row = lambda v: v.reshape(1, -1)
    h = x.reshape(m, d)
    assert depth == 2 and n_a == 1, "wired for one MLA layer followed by one sliding-window layer"

    perm = _interleave_permutation()

    def ffn(l, h, hn_ffn, g_next):
        act, w_out = _ffn_up(hn_ffn, ffn_w_in, ffn_w_out, l, ffn_conv_w[l], row(ffn_conv_b[l]), perm.T, seq)
        f = _matmul(act, w_out, F32, 512, 512, name="ffn_down")
        return _residual_norm(f, h, row(norm_ffn_post[l]), g_next)

    w_uq = mla_w_uq[0].astype(BF16)
    w_uq3 = w_uq.reshape(w_uq.shape[0], heads // hpg, hpg * MLA_QK).transpose(1, 0, 2)
    w_ukv = mla_w_ukv[0].astype(BF16)
    w_ukv3 = w_ukv.reshape(w_ukv.shape[0], heads // hpg, hpg * (MLA_NOPE + MLA_V)).transpose(1, 0, 2)
    cq, c, kr = _mla_down(h, row(norm_mix_pre[0]), mla_w_dq[0].astype(BF16), row(mla_q_norm[0]),
                          mla_w_dkv[0].astype(BF16), row(mla_kv_norm[0]), cosf, sinf)
    q = _mla_q_up(cq, w_uq3, cosf, sinf, batch, seq, MLA_QK ** -0.5 * LOG2_E)
    k, vt = _mla_kv_up(c, w_ukv3, kr, batch, seq)
    o = _mla_attention(q, k, vt, hp=4).reshape(m, heads * MLA_V)
    a = _matmul(o, mla_w_o[0].astype(BF16), F32, 1024, 512, name="mla_out")
    h, hn_ffn = _residual_norm(a, h, row(norm_mix_post[0]), [row(norm_ffn_pre[0])], perm=perm)
    h, hn_mix, hn_kv = ffn(0, h, hn_ffn, [row(norm_mix_pre[1]), row(shared_kv_norm)])

    w_kv = jnp.concatenate([swa_w_k, swa_w_v], axis=1).astype(BF16)
    k_shared, vt_shared = _swa_kv(hn_kv, w_kv)
    groups = k_shared.shape[0]
    sink_lanes = jnp.repeat(swa_sinks[0].reshape(groups, 1, swa_heads // groups) * LOG2_E, WINDOW, axis=2)
    q = _matmul(hn_mix, swa_w_q[0].astype(BF16), BF16, 1024, 512,
                scale=SWA_HEAD_DIM ** -0.5 * LOG2_E, name="swa_q")
    o = _swa_attention(q, k_shared, vt_shared, _swa_bias(swa_heads, groups), sink_lanes, batch, seq)
    a = _matmul(o, swa_w_o[0].astype(BF16), F32, 1024, 512, name="swa_out")
    h, hn_ffn = _residual_norm(a, h, row(norm_mix_post[1]), [row(norm_ffn_pre[1])], perm=perm)
    (h,) = ffn(1, h, hn_ffn, [])
    return h.reshape(batch, seq, d)
```

```python
import functools

import jax
import jax.numpy as jnp
from jax import lax
from jax.experimental import pallas as pl
from jax.experimental.pallas import tpu as pltpu

EPS = 1e-6
LOG2_E = 1.4426950408889634
ROPE_THETA = 10000.0
MLA_NOPE = 128
MLA_ROPE = 64
MLA_V = 128
MLA_QK = MLA_NOPE + MLA_ROPE
SWA_HEAD_DIM = 64
SWA_KV_HEADS = 8
WINDOW = 128
CONV_W = 3

V7X_VMEM_BYTES = 64 * 1024 * 1024
VMEM_LIMIT_CAP = V7X_VMEM_BYTES - 8 * 1024 * 1024
SUBLANES = 8
PERM_UNIT = 256
PERM_GROUPS = PERM_UNIT // SUBLANES

BF16 = jnp.bfloat16
F32 = jnp.float32


def _nbytes(shape, dtype):
    n = 1
    for s in shape:
        n *= s
    return n * jnp.dtype(dtype).itemsize


def _params(semantics, blocks, extra=0):
    est = 2 * sum(_nbytes(s, d) for s, d in blocks) + extra
    limit = min(VMEM_LIMIT_CAP, max(est + est // 4, 16 * 1024 * 1024))
    return pltpu.CompilerParams(dimension_semantics=semantics, vmem_limit_bytes=limit)


def _rms(x, g):
    r = lax.rsqrt(jnp.mean(x * x, axis=-1, keepdims=True) + EPS)
    return (x * r) * g


def _swap_halves(r):
    half = r.shape[-1] // 2
    return jnp.concatenate([r[:, half:], r[:, :half]], axis=-1)


def _rope(r, cosf, sinf):
    return r * cosf + _swap_halves(r) * sinf


def _rope_table_kernel(pos_ref, freq_ref, cos_ref, sin_ref):
    ang = pos_ref[...].astype(F32) * freq_ref[...]
    half = ang.shape[-1] // 2
    lane = lax.broadcasted_iota(jnp.int32, ang.shape, 1)
    s = jnp.sin(ang)
    cos_ref[...] = jnp.cos(ang)
    sin_ref[...] = jnp.where(lane < half, -s, s)


def _rope_tables(pos_col, freqs2):
    m = pos_col.shape[0]
    d = freqs2.shape[1]
    tm = 1024
    return pl.pallas_call(
        _rope_table_kernel,
        out_shape=(jax.ShapeDtypeStruct((m, d), F32), jax.ShapeDtypeStruct((m, d), F32)),
        grid=(m // tm,),
        in_specs=[pl.BlockSpec((tm, 1), lambda i: (i, 0)), pl.BlockSpec((1, d), lambda i: (0, 0))],
        out_specs=(pl.BlockSpec((tm, d), lambda i: (i, 0)), pl.BlockSpec((tm, d), lambda i: (i, 0))),
        name="rope_tables",
    )(pos_col, freqs2)


def _mla_down_kernel(x_ref, g_ref, wdq_ref, qn_ref, wdkv_ref, kvn_ref, cos_ref, sin_ref,
                     cq_ref, c_ref, kr_ref):
    hn = _rms(x_ref[...], g_ref[...]).astype(BF16)
    cq = jnp.dot(hn, wdq_ref[...], preferred_element_type=F32)
    cq_ref[...] = _rms(cq, qn_ref[...]).astype(BF16)
    ckv = jnp.dot(hn, wdkv_ref[...], preferred_element_type=F32)
    lora = c_ref.shape[-1]
    c_ref[...] = _rms(ckv[:, :lora], kvn_ref[...]).astype(BF16)
    kr_ref[...] = _rope(ckv[:, lora:], cos_ref[...], sin_ref[...]).astype(BF16)


def _mla_down(x2, g, wdq, qn, wdkv, kvn, cosf, sinf):
    m, d = x2.shape
    nq = wdq.shape[1]
    nkv = wdkv.shape[1]
    lora = kvn.shape[1]
    rd = nkv - lora
    tm = 256
    row = lambda i: (i, 0)
    fixed = lambda i: (0, 0)
    blocks = [((tm, d), F32), ((d, nq), BF16), ((d, nkv), BF16), ((tm, nq), BF16), ((tm, lora), BF16)]
    return pl.pallas_call(
        _mla_down_kernel,
        out_shape=(jax.ShapeDtypeStruct((m, nq), BF16), jax.ShapeDtypeStruct((m, lora), BF16),
                   jax.ShapeDtypeStruct((m, rd), BF16)),
        grid=(m // tm,),
        in_specs=[pl.BlockSpec((tm, d), row), pl.BlockSpec((1, d), fixed),
                  pl.BlockSpec((d, nq), fixed), pl.BlockSpec((1, nq), fixed),
                  pl.BlockSpec((d, nkv), fixed), pl.BlockSpec((1, lora), fixed),
                  pl.BlockSpec((tm, rd), row), pl.BlockSpec((tm, rd), row)],
        out_specs=(pl.BlockSpec((tm, nq), row), pl.BlockSpec((tm, lora), row), pl.BlockSpec((tm, rd), row)),
        compiler_params=_params(("arbitrary",), blocks, extra=4 * _nbytes((tm, d), F32)),
        name="mla_down",
    )(x2, g, wdq, qn, wdkv, kvn, cosf, sinf)


def _mla_q_up_kernel(cq_ref, w_ref, cos_ref, sin_ref, q_ref, *, scale):
    res = jnp.dot(cq_ref[...], w_ref[0], preferred_element_type=F32)
    cosf = cos_ref[...]
    sinf = sin_ref[...]
    for hh in range(q_ref.shape[1]):
        base = hh * MLA_QK
        nope = res[:, base:base + MLA_NOPE]
        r = _rope(res[:, base + MLA_NOPE:base + MLA_QK], cosf, sinf)
        q_ref[0, hh, :, :MLA_NOPE] = (nope * scale).astype(BF16)
        q_ref[0, hh, :, MLA_NOPE:] = (r * scale).astype(BF16)


def _mla_q_up(cq, w3, cosf, sinf, batch, seq, scale):
    m, k = cq.shape
    groups, _, gn = w3.shape
    hpg = gn // MLA_QK
    heads = groups * hpg
    tm = 512
    nsb = seq // tm
    blocks = [((tm, k), BF16), ((k, gn), BF16), ((hpg, tm, MLA_QK), BF16)]
    return pl.pallas_call(
        functools.partial(_mla_q_up_kernel, scale=scale),
        out_shape=jax.ShapeDtypeStruct((batch, heads, seq, MLA_QK), BF16),
        grid=(m // tm, groups),
        in_specs=[pl.BlockSpec((tm, k), lambda i, g: (i, 0)),
                  pl.BlockSpec((1, k, gn), lambda i, g: (g, 0, 0)),
                  pl.BlockSpec((tm, MLA_ROPE), lambda i, g: (i, 0)),
                  pl.BlockSpec((tm, MLA_ROPE), lambda i, g: (i, 0))],
        out_specs=pl.BlockSpec((1, hpg, tm, MLA_QK), lambda i, g: (i // nsb, g, i % nsb, 0)),
        compiler_params=_params(("arbitrary", "arbitrary"), blocks, extra=2 * _nbytes((tm, gn), F32)),
        name="mla_q_up",
    )(cq, w3, cosf, sinf)


def _mla_kv_up_kernel(c_ref, w_ref, kr_ref, k_ref, vt_ref):
    res = jnp.dot(c_ref[...], w_ref[0], preferred_element_type=F32)
    kr = kr_ref[...]
    per = MLA_NOPE + MLA_V
    for hh in range(k_ref.shape[1]):
        base = hh * per
        k_ref[0, hh, :, :MLA_NOPE] = res[:, base:base + MLA_NOPE].astype(BF16)
        k_ref[0, hh, :, MLA_NOPE:] = kr
        vt_ref[0, hh] = res[:, base + MLA_NOPE:base + per].T.astype(BF16)


def _mla_kv_up(c, w3, kr, batch, seq):
    m, k = c.shape
    groups, _, gn = w3.shape
    per = MLA_NOPE + MLA_V
    hpg = gn // per
    heads = groups * hpg
    tm = 512
    nsb = seq // tm
    blocks = [((tm, k), BF16), ((k, gn), BF16), ((hpg, tm, MLA_QK), BF16), ((hpg, tm, MLA_V), BF16)]
    omap = lambda i, g: (i // nsb, g, i % nsb, 0)
    vmap = lambda i, g: (i // nsb, g, 0, i % nsb)
    return pl.pallas_call(
        _mla_kv_up_kernel,
        out_shape=(jax.ShapeDtypeStruct((batch, heads, seq, MLA_QK), BF16),
                   jax.ShapeDtypeStruct((batch, heads, MLA_V, seq), BF16)),
        grid=(m // tm, groups),
        in_specs=[pl.BlockSpec((tm, k), lambda i, g: (i, 0)),
                  pl.BlockSpec((1, k, gn), lambda i, g: (g, 0, 0)),
                  pl.BlockSpec((tm, MLA_ROPE), lambda i, g: (i, 0))],
        out_specs=(pl.BlockSpec((1, hpg, tm, MLA_QK), omap), pl.BlockSpec((1, hpg, MLA_V, tm), vmap)),
        compiler_params=_params(("arbitrary", "arbitrary"), blocks, extra=2 * _nbytes((tm, gn), F32)),
        name="mla_kv_up",
    )(c, w3, kr)


def _cast_riders(weights, layer, nsteps, step_of):
    in_specs, out_shapes, out_specs, blocks = [], [], [], []
    for w in weights:
        _, rows, cols = w.shape
        slab = rows // nsteps
        assert slab * nsteps == rows and slab % (2 * SUBLANES) == 0, (w.shape, nsteps)
        in_specs.append(pl.BlockSpec((None, slab, cols), lambda *g: (layer, step_of(*g), 0)))
        out_shapes.append(jax.ShapeDtypeStruct((rows, cols), BF16))
        out_specs.append(pl.BlockSpec((slab, cols), lambda *g: (step_of(*g), 0)))
        blocks += [((slab, cols), F32), ((slab, cols), BF16)]
    return in_specs, out_shapes, out_specs, blocks


def _run_cast_riders(src_refs, dst_refs):
    for src, dst in zip(src_refs, dst_refs):
        dst[...] = src[...].astype(dst.dtype)


def _mla_attn_kernel(*refs, tk, n_riders):
    q_ref, k_ref, vt_ref = refs[:3]
    o_ref = refs[3 + n_riders]
    sa_ref, sb_ref, m_ref, l_ref, acc_ref = refs[4 + 2 * n_riders:]
    _run_cast_riders(refs[3:3 + n_riders], refs[4 + n_riders:4 + 2 * n_riders])
    i = pl.program_id(2)
    hp = q_ref.shape[1]
    dv = acc_ref.shape[1]
    m_ref[...] = jnp.full(m_ref.shape, -jnp.inf, F32)
    l_ref[...] = jnp.zeros(l_ref.shape, F32)
    acc_ref[...] = jnp.zeros(acc_ref.shape, F32)

    def scores(j, s_ref):
        start = pl.multiple_of(j * tk, tk)
        for hh in range(hp):
            k = k_ref[0, hh, pl.ds(start, tk), :]
            s_ref[hh] = lax.dot_general(k, q_ref[0, hh], (((1,), (1,)), ((), ())),
                                        preferred_element_type=F32)

    def update(j, s_ref, masked):
        start = pl.multiple_of(j * tk, tk)
        for hh in range(hp):
            st = s_ref[hh]
            if masked:
                kj = lax.broadcasted_iota(jnp.int32, st.shape, 0)
                qi = lax.broadcasted_iota(jnp.int32, st.shape, 1)
                st = jnp.where(kj <= qi, st, -jnp.inf)
            vt = vt_ref[0, hh, :, pl.ds(start, tk)]
            m_prev = m_ref[hh]
            m_new = jnp.maximum(m_prev, jnp.max(st, axis=0, keepdims=True))
            alpha = jnp.exp2(m_prev - m_new)
            pt = jnp.exp2(st - m_new)
            l_ref[hh] = alpha * l_ref[hh] + jnp.sum(pt, axis=0, keepdims=True)
            acc_ref[hh] = alpha * acc_ref[hh] + jnp.dot(vt, pt.astype(BF16), preferred_element_type=F32)
            m_ref[hh] = m_new

    def finish():
        for hh in range(hp):
            o = acc_ref[hh] / l_ref[hh]
            o_ref[0, :, hh * dv:(hh + 1) * dv] = o.T.astype(o_ref.dtype)

    scores(0, sa_ref)

    def pair(jj, carry):
        j = 2 * jj
        scores(j + 1, sb_ref)
        update(j, sa_ref, False)
        scores(j + 2, sa_ref)
        update(j + 1, sb_ref, False)
        return carry

    lax.fori_loop(0, i // 2, pair, 0)

    @pl.when(i % 2 == 0)
    def _():
        update(i, sa_ref, True)
        finish()

    @pl.when(i % 2 == 1)
    def _():
        scores(i, sb_ref)
        update(i - 1, sa_ref, False)
        update(i, sb_ref, True)
        finish()


def _mla_attention(q, k, vt, hp, ride_weights, ride_layer):
    batch, heads, seq, dqk = q.shape
    dv = vt.shape[2]
    tq = 512
    nh, nq = heads // hp, seq // tq
    r_in, r_shapes, r_out, r_blocks = _cast_riders(ride_weights, ride_layer, batch * nh * nq,
                                                   lambda b, h, i: (b * nh + h) * nq + i)
    blocks = [((hp, tq, 256), BF16), ((hp, seq, 256), BF16), ((hp, dv, seq), BF16), ((tq, hp * dv), BF16)]
    return pl.pallas_call(
        functools.partial(_mla_attn_kernel, tk=tq, n_riders=len(ride_weights)),
        out_shape=(jax.ShapeDtypeStruct((batch, seq, heads * dv), BF16), *r_shapes),
        grid=(batch, nh, nq),
        in_specs=[pl.BlockSpec((1, hp, tq, dqk), lambda b, h, i: (b, h, i, 0)),
                  pl.BlockSpec((1, hp, seq, dqk), lambda b, h, i: (b, h, 0, 0)),
                  pl.BlockSpec((1, hp, dv, seq), lambda b, h, i: (b, h, 0, 0)), *r_in],
        out_specs=(pl.BlockSpec((1, tq, hp * dv), lambda b, h, i: (b, i, h)), *r_out),
        scratch_shapes=[pltpu.VMEM((hp, tq, tq), F32), pltpu.VMEM((hp, tq, tq), F32),
                        pltpu.VMEM((hp, 1, tq), F32), pltpu.VMEM((hp, 1, tq), F32),
                        pltpu.VMEM((hp, dv, tq), F32)],
        compiler_params=_params(("arbitrary", "arbitrary", "arbitrary"), blocks + r_blocks,
                                extra=8 * hp * _nbytes((tq, tq), F32)),
        name="mla_attention",
    )(q, k, vt, *ride_weights)


def _matmul_kernel(x_ref, w_ref, o_ref, *, scale):
    acc = jnp.dot(x_ref[...], w_ref[...], preferred_element_type=F32)
    if scale is not None:
        acc = acc * scale
    o_ref[...] = acc.astype(o_ref.dtype)


def _matmul(x, w, out_dtype, tm, tn, scale=None, name="matmul"):
    m, k = x.shape
    n = w.shape[1]
    blocks = [((tm, k), x.dtype), ((k, tn), w.dtype), ((tm, tn), out_dtype)]
    return pl.pallas_call(
        functools.partial(_matmul_kernel, scale=scale),
        out_shape=jax.ShapeDtypeStruct((m, n), out_dtype),
        grid=(m // tm, n // tn),
        in_specs=[pl.BlockSpec((tm, k), lambda i, j: (i, 0)), pl.BlockSpec((k, tn), lambda i, j: (0, j))],
        out_specs=pl.BlockSpec((tm, tn), lambda i, j: (i, j)),
        compiler_params=_params(("arbitrary", "arbitrary"), blocks, extra=2 * _nbytes((tm, tn), F32)),
        name=name,
    )(x, w)


def _residual_norm_kernel(*refs, n_next, interleave):
    a_ref, h_ref, gp_ref = refs[:3]
    gn_refs = refs[3:3 + n_next]
    perm_ref = refs[3 + n_next] if interleave else None
    out_refs = refs[3 + n_next + (1 if interleave else 0):]
    h_new = h_ref[...] + _rms(a_ref[...], gp_ref[...])
    out_refs[0][...] = h_new
    for gn_ref, o_ref in zip(gn_refs, out_refs[1:]):
        hn = _rms(h_new, gn_ref[...]).astype(o_ref.dtype)
        if interleave:
            hn = jnp.dot(perm_ref[...], hn, preferred_element_type=F32).astype(o_ref.dtype)
        o_ref[...] = hn


def _residual_norm(a, h, g_post, g_next, perm=None):
    m, d = a.shape
    n_next = len(g_next)
    tm = PERM_UNIT
    row = lambda i: (i, 0)
    fixed = lambda i: (0, 0)
    interleave = perm is not None
    out_shape = [jax.ShapeDtypeStruct((m, d), F32)] + [jax.ShapeDtypeStruct((m, d), BF16)] * n_next
    out_specs = [pl.BlockSpec((tm, d), row)] * (1 + n_next)
    blocks = [((tm, d), F32)] * (4 + n_next)
    perm_specs = [pl.BlockSpec((tm, tm), fixed)] if interleave else []
    perm_args = [perm] if interleave else []
    return pl.pallas_call(
        functools.partial(_residual_norm_kernel, n_next=n_next, interleave=interleave),
        out_shape=tuple(out_shape),
        grid=(m // tm,),
        in_specs=[pl.BlockSpec((tm, d), row), pl.BlockSpec((tm, d), row)]
        + [pl.BlockSpec((1, d), fixed)] * (1 + n_next) + perm_specs,
        out_specs=tuple(out_specs),
        compiler_params=_params(("arbitrary",), blocks),
        name="residual_norm",
    )(a, h, g_post, *g_next, *perm_args)


def _interleave_permutation():
    r = jnp.arange(PERM_UNIT)
    t = (r % SUBLANES) * PERM_GROUPS + r // SUBLANES
    return (t[:, None] == jnp.arange(PERM_UNIT)[None, :]).astype(BF16)


def _ffn_up_kernel(x_ref, wg_ref, wv_ref, cwg_ref, cwv_ref, cbg_ref, cbv_ref, pt_ref, o_ref, carry_ref,
                   *, blocks_per_seq):
    i = pl.program_id(0)
    j = pl.program_id(1)
    tn = wg_ref.shape[1]
    tm = x_ref.shape[0]
    grp = SUBLANES

    @pl.when((i % blocks_per_seq) == 0)
    def _():
        carry_ref[j] = jnp.zeros(carry_ref.shape[1:], F32)

    w = jnp.concatenate([wg_ref[...], wv_ref[...]], axis=1)
    cw = jnp.concatenate([cwg_ref[...], cwv_ref[...]], axis=1)
    cb = jnp.concatenate([cbg_ref[...], cbv_ref[...]], axis=1)
    first_sublane = lax.broadcasted_iota(jnp.int32, (grp, 2 * tn), 0) == 0
    tail = carry_ref[j]

    def project(c):
        return jnp.dot(x_ref[c * PERM_UNIT:(c + 1) * PERM_UNIT, :], w, preferred_element_type=F32)

    nunits = tm // PERM_UNIT
    u_next = project(0)
    for c in range(nunits):
        rows = slice(c * PERM_UNIT, (c + 1) * PERM_UNIT)
        u = u_next
        if c + 1 < nunits:
            u_next = project(c + 1)
        new_tail = u[PERM_UNIT - 2 * grp:, :]
        wrapped = jnp.where(jnp.concatenate([first_sublane, first_sublane], axis=0),
                            jnp.concatenate([pltpu.roll(tail[:grp], 1, 0), pltpu.roll(tail[grp:], 1, 0)], axis=0),
                            jnp.concatenate([pltpu.roll(new_tail[:grp], 1, 0), pltpu.roll(new_tail[grp:], 1, 0)],
                                            axis=0))
        ext = jnp.concatenate([wrapped, u], axis=0)
        acc = cw[0:1, :] * ext[0:PERM_UNIT] + cw[1:2, :] * ext[grp:grp + PERM_UNIT] + cw[2:3, :] * u
        conv = cb + acc
        gate, val = conv[:, :tn], conv[:, tn:]
        act = (gate * (1.0 / (1.0 + jnp.exp(-gate))) * val).astype(o_ref.dtype)
        o_ref[rows, :] = jnp.dot(pt_ref[...], act, preferred_element_type=F32).astype(o_ref.dtype)
        tail = new_tail
    carry_ref[j] = tail


def _ffn_up(hn, w_in, conv_w, conv_b, perm_t, seq):
    m, d = hn.shape
    d_ff = w_in.shape[1] // 2
    tn = 256
    nj = d_ff // tn
    tm = 1024
    blocks = [((tm, d), BF16), ((d, tn), BF16), ((d, tn), BF16), ((tm, tn), BF16)]
    scratch = _nbytes((nj, 2 * SUBLANES, 2 * tn), F32)
    gate_col = lambda i, j: (0, j)
    val_col = lambda i, j: (0, j + nj)
    return pl.pallas_call(
        functools.partial(_ffn_up_kernel, blocks_per_seq=seq // tm),
        out_shape=jax.ShapeDtypeStruct((m, d_ff), BF16),
        grid=(m // tm, nj),
        in_specs=[pl.BlockSpec((tm, d), lambda i, j: (i, 0)),
                  pl.BlockSpec((d, tn), gate_col), pl.BlockSpec((d, tn), val_col),
                  pl.BlockSpec((CONV_W, tn), gate_col), pl.BlockSpec((CONV_W, tn), val_col),
                  pl.BlockSpec((1, tn), gate_col), pl.BlockSpec((1, tn), val_col),
                  pl.BlockSpec((PERM_UNIT, PERM_UNIT), lambda i, j: (0, 0))],
        out_specs=pl.BlockSpec((tm, tn), lambda i, j: (i, j)),
        scratch_shapes=[pltpu.VMEM((nj, 2 * SUBLANES, 2 * tn), F32)],
        compiler_params=_params(("arbitrary", "arbitrary"), blocks,
                                extra=scratch + 10 * _nbytes((PERM_UNIT, 2 * tn), F32)),
        name="ffn_up",
    )(hn, w_in, w_in, conv_w, conv_w, conv_b, conv_b, perm_t)


def _swa_kv_kernel(x_ref, w_ref, k_ref, vt_ref):
    res = jnp.dot(x_ref[...], w_ref[...], preferred_element_type=F32)
    groups = k_ref.shape[0]
    hd = SWA_HEAD_DIM
    for t in range(groups):
        k_ref[t] = res[:, t * hd:(t + 1) * hd].astype(k_ref.dtype)
    vt = res[:, groups * hd:].T
    for t in range(groups):
        vt_ref[t] = vt[t * hd:(t + 1) * hd, :].astype(vt_ref.dtype)


def _swa_kv(x, w):
    m, k = x.shape
    n = w.shape[1]
    groups = n // (2 * SWA_HEAD_DIM)
    tm = 512
    blocks = [((tm, k), BF16), ((k, n), BF16), ((groups, tm, 128), BF16), ((groups, SWA_HEAD_DIM, tm), BF16)]
    return pl.pallas_call(
        _swa_kv_kernel,
        out_shape=(jax.ShapeDtypeStruct((groups, m, SWA_HEAD_DIM), BF16),
                   jax.ShapeDtypeStruct((groups, SWA_HEAD_DIM, m), BF16)),
        grid=(m // tm,),
        in_specs=[pl.BlockSpec((tm, k), lambda i: (i, 0)), pl.BlockSpec((k, n), lambda i: (0, 0))],
        out_specs=(pl.BlockSpec((groups, tm, SWA_HEAD_DIM), lambda i: (0, i, 0)),
                   pl.BlockSpec((groups, SWA_HEAD_DIM, tm), lambda i: (0, 0, i))),
        compiler_params=_params(("arbitrary",), blocks, extra=3 * _nbytes((tm, n), F32)),
        name="swa_kv",
    )(x, w)


def _swa_attn_kernel(*refs, rep, n_riders):
    bias_ref, sink_ref, q_ref, kp_ref, kc_ref, vtp_ref, vtc_ref = refs[:7]
    o_ref = refs[7 + n_riders]
    _run_cast_riders(refs[7:7 + n_riders], refs[8 + n_riders:8 + 2 * n_riders])
    n = pl.program_id(2)
    w = WINDOW
    hd = SWA_HEAD_DIM
    nq = q_ref.shape[0] // w
    kfull = jnp.concatenate([kp_ref[0], kc_ref[0]], axis=0)
    vtfull = jnp.concatenate([vtp_ref[0], vtc_ref[0]], axis=1)
    sink = sink_ref[0]
    first = jnp.minimum(n, 1)
    for c in range(nq):
        qt = q_ref[c * w:(c + 1) * w, :].astype(F32).T
        qt = jnp.concatenate([qt[r * hd:(r + 1) * hd, :] for r in range(rep)], axis=1).astype(BF16)
        bias = bias_ref[0, first] if c == 0 else bias_ref[0, 1]
        st = jnp.dot(kfull[c * w:(c + 2) * w], qt, preferred_element_type=F32) + bias
        mx = jnp.maximum(jnp.max(st, axis=0, keepdims=True), sink)
        e = jnp.exp2(st - mx)
        denom = jnp.sum(e, axis=0, keepdims=True) + jnp.exp2(sink - mx)
        ot = jnp.dot(vtfull[:, c * w:(c + 2) * w], e.astype(BF16), preferred_element_type=F32)
        ot = ot * (1.0 / denom)
        o = jnp.concatenate([ot[:, r * w:(r + 1) * w] for r in range(rep)], axis=0)
        o_ref[c * w:(c + 1) * w, :] = o.T.astype(o_ref.dtype)


def _swa_attention(q, k, vt, bias, sink_lanes, batch, seq, ride_weights, ride_layer):
    m, dq = q.shape
    groups = k.shape[0]
    rep = dq // (groups * SWA_HEAD_DIM)
    w = WINDOW
    nq = 4
    tq = nq * w
    nb = seq // tq
    gw = rep * SWA_HEAD_DIM
    prev_blk = lambda b, n: jnp.maximum((b * nb + n) * nq - 1, b * nb * nq)
    r_in, r_shapes, r_out, r_blocks = _cast_riders(ride_weights, ride_layer, groups * batch * nb,
                                                   lambda g, b, n: (g * batch + b) * nb + n)
    blocks = [((2, 2 * w, rep * w), F32), ((tq, gw), BF16), ((tq, gw), BF16)] + r_blocks
    return pl.pallas_call(
        functools.partial(_swa_attn_kernel, rep=rep, n_riders=len(ride_weights)),
        out_shape=(jax.ShapeDtypeStruct((m, dq), BF16), *r_shapes),
        grid=(groups, batch, nb),
        in_specs=[pl.BlockSpec((1, 2, 2 * w, rep * w), lambda g, b, n: (g, 0, 0, 0)),
                  pl.BlockSpec((1, 1, rep * w), lambda g, b, n: (g, 0, 0)),
                  pl.BlockSpec((tq, gw), lambda g, b, n: (b * nb + n, g)),
                  pl.BlockSpec((1, w, SWA_HEAD_DIM), lambda g, b, n: (g, prev_blk(b, n), 0)),
                  pl.BlockSpec((1, tq, SWA_HEAD_DIM), lambda g, b, n: (g, b * nb + n, 0)),
                  pl.BlockSpec((1, SWA_HEAD_DIM, w), lambda g, b, n: (g, 0, prev_blk(b, n))),
                  pl.BlockSpec((1, SWA_HEAD_DIM, tq), lambda g, b, n: (g, 0, b * nb + n)), *r_in],
        out_specs=(pl.BlockSpec((tq, gw), lambda g, b, n: (b * nb + n, g)), *r_out),
        compiler_params=_params(("arbitrary", "arbitrary", "arbitrary"), blocks,
                                extra=6 * nq * _nbytes((2 * w, rep * w), F32)),
        name="swa_attention",
    )(bias, sink_lanes, q, k, k, vt, vt, *ride_weights)


def _swa_bias(swa_heads, groups):
    w = WINDOW
    rep = swa_heads // groups
    slopes = 2.0 ** (-8.0 * jnp.arange(1, swa_heads + 1, dtype=F32) / swa_heads)
    kj = jnp.arange(2 * w)[:, None]
    qi = jnp.arange(w)[None, :]
    dist = qi + w - kj
    in_window = (dist >= 0) & (dist < w)
    valid = jnp.stack([in_window & (kj >= w), in_window])
    alibi = slopes.reshape(groups, 1, 1, rep, 1) * dist.astype(F32)[None, None, :, None, :]
    bias = jnp.where(valid[None, :, :, None, :], -alibi * LOG2_E, -jnp.inf)
    return bias.reshape(groups, 2, 2 * w, rep * w)


def kernel(x, positions, norm_mix_pre, norm_mix_post, norm_ffn_pre, norm_ffn_post, mla_w_dq, mla_q_norm,
           mla_w_uq, mla_w_dkv, mla_kv_norm, mla_w_ukv, mla_w_o, shared_kv_norm, swa_w_k, swa_w_v, swa_w_q,
           swa_sinks, swa_w_o, ffn_w_in, ffn_conv_w, ffn_conv_b, ffn_w_out):
    batch, seq, d = x.shape
    m = batch * seq
    depth = norm_mix_pre.shape[0]
    n_a = mla_w_dq.shape[0]
    heads = mla_w_o.shape[1] // MLA_V
    swa_heads = swa_w_q.shape[2] // SWA_HEAD_DIM
    hpg = 4

    half = MLA_ROPE // 2
    freqs = ROPE_THETA ** (-jnp.arange(half, dtype=F32) / half)
    freqs2 = jnp.concatenate([freqs, freqs])[None, :]
    cosf, sinf = _rope_tables(positions.reshape(m, 1), freqs2)

    row = lambda v: v.reshape(1, -1)
    h = x.reshape(m, d)
    assert depth == 2 and n_a == 1, "wired for one MLA layer followed by one sliding-window layer"

    perm = _interleave_permutation()
    d_ff2 = ffn_w_in.shape[2]
    ffn_riders = (ffn_w_in, ffn_w_out.reshape(depth, -1, d_ff2))

    def ffn(l, h, hn_ffn, w_in, w_out_flat, g_next):
        act = _ffn_up(hn_ffn, w_in, ffn_conv_w[l], row(ffn_conv_b[l]), perm.T, seq)
        f = _matmul(act, w_out_flat.reshape(ffn_w_out.shape[1:]), F32, 512, 512, name="ffn_down")
        return _residual_norm(f, h, row(norm_ffn_post[l]), g_next)

    w_uq = mla_w_uq[0].astype(BF16)
    w_uq3 = w_uq.reshape(w_uq.shape[0], heads // hpg, hpg * MLA_QK).transpose(1, 0, 2)
    w_ukv = mla_w_ukv[0].astype(BF16)
    w_ukv3 = w_ukv.reshape(w_ukv.shape[0], heads // hpg, hpg * (MLA_NOPE + MLA_V)).transpose(1, 0, 2)
    cq, c, kr = _mla_down(h, row(norm_mix_pre[0]), mla_w_dq[0].astype(BF16), row(mla_q_norm[0]),
                          mla_w_dkv[0].astype(BF16), row(mla_kv_norm[0]), cosf, sinf)
    q = _mla_q_up(cq, w_uq3, cosf, sinf, batch, seq, MLA_QK ** -0.5 * LOG2_E)
    k, vt = _mla_kv_up(c, w_ukv3, kr, batch, seq)
    o, w_in0, w_out0 = _mla_attention(q, k, vt, 4, ffn_riders, 0)
    a = _matmul(o.reshape(m, heads * MLA_V), mla_w_o[0].astype(BF16), F32, 1024, 512, name="mla_out")
    h, hn_ffn = _residual_norm(a, h, row(norm_mix_post[0]), [row(norm_ffn_pre[0])], perm=perm)
    h, hn_mix, hn_kv = ffn(0, h, hn_ffn, w_in0, w_out0, [row(norm_mix_pre[1]), row(shared_kv_norm)])

    w_kv = jnp.concatenate([swa_w_k, swa_w_v], axis=1).astype(BF16)
    k_shared, vt_shared = _swa_kv(hn_kv, w_kv)
    groups = k_shared.shape[0]
    sink_lanes = jnp.repeat(swa_sinks[0].reshape(groups, 1, swa_heads // groups) * LOG2_E, WINDOW, axis=2)
    q = _matmul(hn_mix, swa_w_q[0].astype(BF16), BF16, 1024, 512,
                scale=SWA_HEAD_DIM ** -0.5 * LOG2_E, name="swa_q")
    o, w_in1, w_out1 = _swa_attention(q, k_shared, vt_shared, _swa_bias(swa_heads, groups), sink_lanes,
                                      batch, seq, ffn_riders, 1)
    a = _matmul(o, swa_w_o[0].astype(BF16), F32, 1024, 512, name="swa_out")
    h, hn_ffn = _residual_norm(a, h, row(norm_mix_post[1]), [row(norm_ffn_pre[1])], perm=perm)
    (h,) = ffn(1, h, hn_ffn, w_in1, w_out1, [])
    return h.reshape(batch, seq, d)
```

```python
import functools

import jax
import jax.numpy as jnp
from jax import lax
from jax.experimental import pallas as pl
from jax.experimental.pallas import tpu as pltpu

EPS = 1e-6
LOG2_E = 1.4426950408889634
ROPE_THETA = 10000.0
MLA_NOPE = 128
MLA_ROPE = 64
MLA_V = 128
MLA_QK = MLA_NOPE + MLA_ROPE
SWA_HEAD_DIM = 64
SWA_KV_HEADS = 8
WINDOW = 128
CONV_W = 3

V7X_VMEM_BYTES = 64 * 1024 * 1024
VMEM_LIMIT_CAP = V7X_VMEM_BYTES - 8 * 1024 * 1024
SUBLANES = 8
PERM_UNIT = 256
PERM_GROUPS = PERM_UNIT // SUBLANES

BF16 = jnp.bfloat16
F32 = jnp.float32


def _nbytes(shape, dtype):
    n = 1
    for s in shape:
        n *= s
    return n * jnp.dtype(dtype).itemsize


def _params(semantics, blocks, extra=0):
    est = 2 * sum(_nbytes(s, d) for s, d in blocks) + extra
    limit = min(VMEM_LIMIT_CAP, max(est + est // 4, 16 * 1024 * 1024))
    return pltpu.CompilerParams(dimension_semantics=semantics, vmem_limit_bytes=limit)


def _rms(x, g):
    r = lax.rsqrt(jnp.mean(x * x, axis=-1, keepdims=True) + EPS)
    return (x * r) * g


def _swap_halves(r):
    half = r.shape[-1] // 2
    return jnp.concatenate([r[:, half:], r[:, :half]], axis=-1)


def _rope(r, cosf, sinf):
    return r * cosf + _swap_halves(r) * sinf


def _rope_table_kernel(pos_ref, freq_ref, cos_ref, sin_ref):
    ang = pos_ref[...].astype(F32) * freq_ref[...]
    half = ang.shape[-1] // 2
    lane = lax.broadcasted_iota(jnp.int32, ang.shape, 1)
    s = jnp.sin(ang)
    cos_ref[...] = jnp.cos(ang)
    sin_ref[...] = jnp.where(lane < half, -s, s)


def _rope_tables(pos_col, freqs2):
    m = pos_col.shape[0]
    d = freqs2.shape[1]
    tm = 1024
    return pl.pallas_call(
        _rope_table_kernel,
        out_shape=(jax.ShapeDtypeStruct((m, d), F32), jax.ShapeDtypeStruct((m, d), F32)),
        grid=(m // tm,),
        in_specs=[pl.BlockSpec((tm, 1), lambda i: (i, 0)), pl.BlockSpec((1, d), lambda i: (0, 0))],
        out_specs=(pl.BlockSpec((tm, d), lambda i: (i, 0)), pl.BlockSpec((tm, d), lambda i: (i, 0))),
        name="rope_tables",
    )(pos_col, freqs2)


def _mla_down_kernel(x_ref, g_ref, wdq_ref, qn_ref, wdkv_ref, kvn_ref, cos_ref, sin_ref,
                     cq_ref, c_ref, kr_ref):
    hn = _rms(x_ref[...], g_ref[...]).astype(BF16)
    cq = jnp.dot(hn, wdq_ref[...], preferred_element_type=F32)
    cq_ref[...] = _rms(cq, qn_ref[...]).astype(BF16)
    ckv = jnp.dot(hn, wdkv_ref[...], preferred_element_type=F32)
    lora = c_ref.shape[-1]
    c_ref[...] = _rms(ckv[:, :lora], kvn_ref[...]).astype(BF16)
    kr_ref[...] = _rope(ckv[:, lora:], cos_ref[...], sin_ref[...]).astype(BF16)


def _mla_down(x2, g, wdq, qn, wdkv, kvn, cosf, sinf):
    m, d = x2.shape
    nq = wdq.shape[1]
    nkv = wdkv.shape[1]
    lora = kvn.shape[1]
    rd = nkv - lora
    tm = 256
    row = lambda i: (i, 0)
    fixed = lambda i: (0, 0)
    blocks = [((tm, d), F32), ((d, nq), BF16), ((d, nkv), BF16), ((tm, nq), BF16), ((tm, lora), BF16)]
    return pl.pallas_call(
        _mla_down_kernel,
        out_shape=(jax.ShapeDtypeStruct((m, nq), BF16), jax.ShapeDtypeStruct((m, lora), BF16),
                   jax.ShapeDtypeStruct((m, rd), BF16)),
        grid=(m // tm,),
        in_specs=[pl.BlockSpec((tm, d), row), pl.BlockSpec((1, d), fixed),
                  pl.BlockSpec((d, nq), fixed), pl.BlockSpec((1, nq), fixed),
                  pl.BlockSpec((d, nkv), fixed), pl.BlockSpec((1, lora), fixed),
                  pl.BlockSpec((tm, rd), row), pl.BlockSpec((tm, rd), row)],
        out_specs=(pl.BlockSpec((tm, nq), row), pl.BlockSpec((tm, lora), row), pl.BlockSpec((tm, rd), row)),
        compiler_params=_params(("arbitrary",), blocks, extra=4 * _nbytes((tm, d), F32)),
        name="mla_down",
    )(x2, g, wdq, qn, wdkv, kvn, cosf, sinf)


def _mla_q_up_kernel(cq_ref, w_ref, cos_ref, sin_ref, q_ref, *, scale):
    res = jnp.dot(cq_ref[...], w_ref[0], preferred_element_type=F32)
    cosf = cos_ref[...]
    sinf = sin_ref[...]
    for hh in range(q_ref.shape[1]):
        base = hh * MLA_QK
        nope = res[:, base:base + MLA_NOPE]
        r = _rope(res[:, base + MLA_NOPE:base + MLA_QK], cosf, sinf)
        q_ref[0, hh, :, :MLA_NOPE] = (nope * scale).astype(BF16)
        q_ref[0, hh, :, MLA_NOPE:] = (r * scale).astype(BF16)


def _mla_q_up(cq, w3, cosf, sinf, batch, seq, scale):
    m, k = cq.shape
    groups, _, gn = w3.shape
    hpg = gn // MLA_QK
    heads = groups * hpg
    tm = 512
    nsb = seq // tm
    blocks = [((tm, k), BF16), ((k, gn), BF16), ((hpg, tm, MLA_QK), BF16)]
    return pl.pallas_call(
        functools.partial(_mla_q_up_kernel, scale=scale),
        out_shape=jax.ShapeDtypeStruct((batch, heads, seq, MLA_QK), BF16),
        grid=(m // tm, groups),
        in_specs=[pl.BlockSpec((tm, k), lambda i, g: (i, 0)),
                  pl.BlockSpec((1, k, gn), lambda i, g: (g, 0, 0)),
                  pl.BlockSpec((tm, MLA_ROPE), lambda i, g: (i, 0)),
                  pl.BlockSpec((tm, MLA_ROPE), lambda i, g: (i, 0))],
        out_specs=pl.BlockSpec((1, hpg, tm, MLA_QK), lambda i, g: (i // nsb, g, i % nsb, 0)),
        compiler_params=_params(("arbitrary", "arbitrary"), blocks, extra=2 * _nbytes((tm, gn), F32)),
        name="mla_q_up",
    )(cq, w3, cosf, sinf)


def _mla_kv_up_kernel(c_ref, w_ref, kr_ref, k_ref, vt_ref):
    res = jnp.dot(c_ref[...], w_ref[0], preferred_element_type=F32)
    kr = kr_ref[...]
    per = MLA_NOPE + MLA_V
    for hh in range(k_ref.shape[1]):
        base = hh * per
        k_ref[0, hh, :, :MLA_NOPE] = res[:, base:base + MLA_NOPE].astype(BF16)
        k_ref[0, hh, :, MLA_NOPE:] = kr
        vt_ref[0, hh] = res[:, base + MLA_NOPE:base + per].T.astype(BF16)


def _mla_kv_up(c, w3, kr, batch, seq):
    m, k = c.shape
    groups, _, gn = w3.shape
    per = MLA_NOPE + MLA_V
    hpg = gn // per
    heads = groups * hpg
    tm = 512
    nsb = seq // tm
    blocks = [((tm, k), BF16), ((k, gn), BF16), ((hpg, tm, MLA_QK), BF16), ((hpg, tm, MLA_V), BF16)]
    omap = lambda i, g: (i // nsb, g, i % nsb, 0)
    vmap = lambda i, g: (i // nsb, g, 0, i % nsb)
    return pl.pallas_call(
        _mla_kv_up_kernel,
        out_shape=(jax.ShapeDtypeStruct((batch, heads, seq, MLA_QK), BF16),
                   jax.ShapeDtypeStruct((batch, heads, MLA_V, seq), BF16)),
        grid=(m // tm, groups),
        in_specs=[pl.BlockSpec((tm, k), lambda i, g: (i, 0)),
                  pl.BlockSpec((1, k, gn), lambda i, g: (g, 0, 0)),
                  pl.BlockSpec((tm, MLA_ROPE), lambda i, g: (i, 0))],
        out_specs=(pl.BlockSpec((1, hpg, tm, MLA_QK), omap), pl.BlockSpec((1, hpg, MLA_V, tm), vmap)),
        compiler_params=_params(("arbitrary", "arbitrary"), blocks, extra=2 * _nbytes((tm, gn), F32)),
        name="mla_kv_up",
    )(c, w3, kr)


def _cast_riders(weights, layer, nsteps, step_of):
    in_specs, out_shapes, out_specs, blocks = [], [], [], []
    for w in weights:
        _, rows, cols = w.shape
        slab = rows // nsteps
        assert slab * nsteps == rows and slab % (2 * SUBLANES) == 0, (w.shape, nsteps)
        in_specs.append(pl.BlockSpec((None, slab, cols), lambda *g: (layer, step_of(*g), 0)))
        out_shapes.append(jax.ShapeDtypeStruct((rows, cols), BF16))
        out_specs.append(pl.BlockSpec((slab, cols), lambda *g: (step_of(*g), 0)))
        blocks += [((slab, cols), F32), ((slab, cols), BF16)]
    return in_specs, out_shapes, out_specs, blocks


def _run_cast_riders(src_refs, dst_refs):
    for src, dst in zip(src_refs, dst_refs):
        dst[...] = src[...].astype(dst.dtype)


def _mla_attn_kernel(*refs, tk, n_riders):
    q_ref, k_ref, vt_ref = refs[:3]
    o_ref = refs[3 + n_riders]
    sa_ref, sb_ref, m_ref, l_ref, acc_ref = refs[4 + 2 * n_riders:]
    _run_cast_riders(refs[3:3 + n_riders], refs[4 + n_riders:4 + 2 * n_riders])
    i = pl.program_id(2)
    hp = q_ref.shape[1]
    dv = acc_ref.shape[1]
    m_ref[...] = jnp.full(m_ref.shape, -jnp.inf, F32)
    l_ref[...] = jnp.zeros(l_ref.shape, F32)
    acc_ref[...] = jnp.zeros(acc_ref.shape, F32)

    def scores(j, s_ref):
        start = pl.multiple_of(j * tk, tk)
        for hh in range(hp):
            k = k_ref[0, hh, pl.ds(start, tk), :]
            s_ref[hh] = lax.dot_general(k, q_ref[0, hh], (((1,), (1,)), ((), ())),
                                        preferred_element_type=F32)

    def update(j, s_ref, masked):
        start = pl.multiple_of(j * tk, tk)
        for hh in range(hp):
            st = s_ref[hh]
            if masked:
                kj = lax.broadcasted_iota(jnp.int32, st.shape, 0)
                qi = lax.broadcasted_iota(jnp.int32, st.shape, 1)
                st = jnp.where(kj <= qi, st, -jnp.inf)
            vt = vt_ref[0, hh, :, pl.ds(start, tk)]
            m_prev = m_ref[hh]
            m_new = jnp.maximum(m_prev, jnp.max(st, axis=0, keepdims=True))
            alpha = jnp.exp2(m_prev - m_new)
            pt = jnp.exp2(st - m_new)
            l_ref[hh] = alpha * l_ref[hh] + jnp.sum(pt, axis=0, keepdims=True)
            acc_ref[hh] = alpha * acc_ref[hh] + jnp.dot(vt, pt.astype(BF16), preferred_element_type=F32)
            m_ref[hh] = m_new

    def finish():
        for hh in range(hp):
            o = acc_ref[hh] / l_ref[hh]
            o_ref[0, :, hh * dv:(hh + 1) * dv] = o.T.astype(o_ref.dtype)

    scores(0, sa_ref)

    def pair(jj, carry):
        j = 2 * jj
        scores(j + 1, sb_ref)
        update(j, sa_ref, False)
        scores(j + 2, sa_ref)
        update(j + 1, sb_ref, False)
        return carry

    lax.fori_loop(0, i // 2, pair, 0)

    @pl.when(i % 2 == 0)
    def _():
        update(i, sa_ref, True)
        finish()

    @pl.when(i % 2 == 1)
    def _():
        scores(i, sb_ref)
        update(i - 1, sa_ref, False)
        update(i, sb_ref, True)
        finish()


def _mla_attention(q, k, vt, hp, ride_weights, ride_layer):
    batch, heads, seq, dqk = q.shape
    dv = vt.shape[2]
    tq = 512
    nh, nq = heads // hp, seq // tq
    r_in, r_shapes, r_out, r_blocks = _cast_riders(ride_weights, ride_layer, batch * nh * nq,
                                                   lambda b, h, i: (b * nh + h) * nq + i)
    blocks = [((hp, tq, 256), BF16), ((hp, seq, 256), BF16), ((hp, dv, seq), BF16), ((tq, hp * dv), BF16)]
    return pl.pallas_call(
        functools.partial(_mla_attn_kernel, tk=tq, n_riders=len(ride_weights)),
        out_shape=(jax.ShapeDtypeStruct((batch, seq, heads * dv), BF16), *r_shapes),
        grid=(batch, nh, nq),
        in_specs=[pl.BlockSpec((1, hp, tq, dqk), lambda b, h, i: (b, h, i, 0)),
                  pl.BlockSpec((1, hp, seq, dqk), lambda b, h, i: (b, h, 0, 0)),
                  pl.BlockSpec((1, hp, dv, seq), lambda b, h, i: (b, h, 0, 0)), *r_in],
        out_specs=(pl.BlockSpec((1, tq, hp * dv), lambda b, h, i: (b, i, h)), *r_out),
        scratch_shapes=[pltpu.VMEM((hp, tq, tq), F32), pltpu.VMEM((hp, tq, tq), F32),
                        pltpu.VMEM((hp, 1, tq), F32), pltpu.VMEM((hp, 1, tq), F32),
                        pltpu.VMEM((hp, dv, tq), F32)],
        compiler_params=_params(("arbitrary", "arbitrary", "arbitrary"), blocks + r_blocks,
                                extra=8 * hp * _nbytes((tq, tq), F32)),
        name="mla_attention",
    )(q, k, vt, *ride_weights)


def _matmul_kernel(x_ref, w_ref, o_ref, *, scale):
    acc = jnp.dot(x_ref[...], w_ref[...], preferred_element_type=F32)
    if scale is not None:
        acc = acc * scale
    o_ref[...] = acc.astype(o_ref.dtype)


def _matmul(x, w, out_dtype, tm, tn, scale=None, name="matmul"):
    m, k = x.shape
    n = w.shape[1]
    blocks = [((tm, k), x.dtype), ((k, tn), w.dtype), ((tm, tn), out_dtype)]
    return pl.pallas_call(
        functools.partial(_matmul_kernel, scale=scale),
        out_shape=jax.ShapeDtypeStruct((m, n), out_dtype),
        grid=(m // tm, n // tn),
        in_specs=[pl.BlockSpec((tm, k), lambda i, j: (i, 0)), pl.BlockSpec((k, tn), lambda i, j: (0, j))],
        out_specs=pl.BlockSpec((tm, tn), lambda i, j: (i, j)),
        compiler_params=_params(("arbitrary", "arbitrary"), blocks, extra=2 * _nbytes((tm, tn), F32)),
        name=name,
    )(x, w)


def _residual_norm_kernel(*refs, n_next, interleave):
    a_ref, h_ref, gp_ref = refs[:3]
    gn_refs = refs[3:3 + n_next]
    perm_ref = refs[3 + n_next] if interleave else None
    out_refs = refs[3 + n_next + (1 if interleave else 0):]
    h_new = h_ref[...] + _rms(a_ref[...], gp_ref[...])
    out_refs[0][...] = h_new
    for gn_ref, o_ref in zip(gn_refs, out_refs[1:]):
        hn = _rms(h_new, gn_ref[...]).astype(o_ref.dtype)
        if interleave:
            hn = jnp.dot(perm_ref[...], hn, preferred_element_type=F32).astype(o_ref.dtype)
        o_ref[...] = hn


def _residual_norm(a, h, g_post, g_next, perm=None):
    m, d = a.shape
    n_next = len(g_next)
    tm = PERM_UNIT
    row = lambda i: (i, 0)
    fixed = lambda i: (0, 0)
    interleave = perm is not None
    out_shape = [jax.ShapeDtypeStruct((m, d), F32)] + [jax.ShapeDtypeStruct((m, d), BF16)] * n_next
    out_specs = [pl.BlockSpec((tm, d), row)] * (1 + n_next)
    blocks = [((tm, d), F32)] * (4 + n_next)
    perm_specs = [pl.BlockSpec((tm, tm), fixed)] if interleave else []
    perm_args = [perm] if interleave else []
    return pl.pallas_call(
        functools.partial(_residual_norm_kernel, n_next=n_next, interleave=interleave),
        out_shape=tuple(out_shape),
        grid=(m // tm,),
        in_specs=[pl.BlockSpec((tm, d), row), pl.BlockSpec((tm, d), row)]
        + [pl.BlockSpec((1, d), fixed)] * (1 + n_next) + perm_specs,
        out_specs=tuple(out_specs),
        compiler_params=_params(("arbitrary",), blocks),
        name="residual_norm",
    )(a, h, g_post, *g_next, *perm_args)


def _interleave_permutation():
    r = jnp.arange(PERM_UNIT)
    t = (r % SUBLANES) * PERM_GROUPS + r // SUBLANES
    return (t[:, None] == jnp.arange(PERM_UNIT)[None, :]).astype(BF16)


def _ffn_up_kernel(x_ref, wg_ref, wv_ref, cwg_ref, cwv_ref, cbg_ref, cbv_ref, pt_ref, wdn_ref, o_ref,
                   wdn_bf_ref, carry_ref, *, blocks_per_seq):
    _run_cast_riders([wdn_ref], [wdn_bf_ref])
    i = pl.program_id(0)
    j = pl.program_id(1)
    tn = wg_ref.shape[1]
    tm = x_ref.shape[0]
    grp = SUBLANES

    @pl.when((i % blocks_per_seq) == 0)
    def _():
        carry_ref[j] = jnp.zeros(carry_ref.shape[1:], F32)

    w = jnp.concatenate([wg_ref[...], wv_ref[...]], axis=1)
    cw = jnp.concatenate([cwg_ref[...], cwv_ref[...]], axis=1)
    cb = jnp.concatenate([cbg_ref[...], cbv_ref[...]], axis=1)
    first_sublane = lax.broadcasted_iota(jnp.int32, (grp, 2 * tn), 0) == 0
    tail = carry_ref[j]

    def project(c):
        return jnp.dot(x_ref[c * PERM_UNIT:(c + 1) * PERM_UNIT, :], w, preferred_element_type=F32)

    nunits = tm // PERM_UNIT
    u_next = project(0)
    for c in range(nunits):
        rows = slice(c * PERM_UNIT, (c + 1) * PERM_UNIT)
        u = u_next
        if c + 1 < nunits:
            u_next = project(c + 1)
        new_tail = u[PERM_UNIT - 2 * grp:, :]
        wrapped = jnp.where(jnp.concatenate([first_sublane, first_sublane], axis=0),
                            jnp.concatenate([pltpu.roll(tail[:grp], 1, 0), pltpu.roll(tail[grp:], 1, 0)], axis=0),
                            jnp.concatenate([pltpu.roll(new_tail[:grp], 1, 0), pltpu.roll(new_tail[grp:], 1, 0)],
                                            axis=0))
        ext = jnp.concatenate([wrapped, u], axis=0)
        acc = cw[0:1, :] * ext[0:PERM_UNIT] + cw[1:2, :] * ext[grp:grp + PERM_UNIT] + cw[2:3, :] * u
        conv = cb + acc
        gate, val = conv[:, :tn], conv[:, tn:]
        act = (gate * (1.0 / (1.0 + jnp.exp(-gate))) * val).astype(o_ref.dtype)
        o_ref[rows, :] = jnp.dot(pt_ref[...], act, preferred_element_type=F32).astype(o_ref.dtype)
        tail = new_tail
    carry_ref[j] = tail


def _ffn_up(hn, w_in, conv_w, conv_b, perm_t, w_out_all, layer, seq):
    m, d = hn.shape
    d_ff = w_in.shape[1] // 2
    tn = 256
    nj = d_ff // tn
    tm = 1024
    r_in, r_shapes, r_out, r_blocks = _cast_riders([w_out_all], layer, (m // tm) * nj, lambda i, j: i * nj + j)
    blocks = [((tm, d), BF16), ((d, tn), BF16), ((d, tn), BF16), ((tm, tn), BF16)] + r_blocks
    scratch = _nbytes((nj, 2 * SUBLANES, 2 * tn), F32)
    gate_col = lambda i, j: (0, j)
    val_col = lambda i, j: (0, j + nj)
    return pl.pallas_call(
        functools.partial(_ffn_up_kernel, blocks_per_seq=seq // tm),
        out_shape=(jax.ShapeDtypeStruct((m, d_ff), BF16), *r_shapes),
        grid=(m // tm, nj),
        in_specs=[pl.BlockSpec((tm, d), lambda i, j: (i, 0)),
                  pl.BlockSpec((d, tn), gate_col), pl.BlockSpec((d, tn), val_col),
                  pl.BlockSpec((CONV_W, tn), gate_col), pl.BlockSpec((CONV_W, tn), val_col),
                  pl.BlockSpec((1, tn), gate_col), pl.BlockSpec((1, tn), val_col),
                  pl.BlockSpec((PERM_UNIT, PERM_UNIT), lambda i, j: (0, 0)), *r_in],
        out_specs=(pl.BlockSpec((tm, tn), lambda i, j: (i, j)), *r_out),
        scratch_shapes=[pltpu.VMEM((nj, 2 * SUBLANES, 2 * tn), F32)],
        compiler_params=_params(("arbitrary", "arbitrary"), blocks,
                                extra=scratch + 10 * _nbytes((PERM_UNIT, 2 * tn), F32)),
        name="ffn_up",
    )(hn, w_in, w_in, conv_w, conv_w, conv_b, conv_b, perm_t, w_out_all)


def _swa_kv_kernel(x_ref, w_ref, k_ref, vt_ref):
    res = jnp.dot(x_ref[...], w_ref[...], preferred_element_type=F32)
    groups = k_ref.shape[0]
    hd = SWA_HEAD_DIM
    for t in range(groups):
        k_ref[t] = res[:, t * hd:(t + 1) * hd].astype(k_ref.dtype)
    vt = res[:, groups * hd:].T
    for t in range(groups):
        vt_ref[t] = vt[t * hd:(t + 1) * hd, :].astype(vt_ref.dtype)


def _swa_kv(x, w):
    m, k = x.shape
    n = w.shape[1]
    groups = n // (2 * SWA_HEAD_DIM)
    tm = 512
    blocks = [((tm, k), BF16), ((k, n), BF16), ((groups, tm, 128), BF16), ((groups, SWA_HEAD_DIM, tm), BF16)]
    return pl.pallas_call(
        _swa_kv_kernel,
        out_shape=(jax.ShapeDtypeStruct((groups, m, SWA_HEAD_DIM), BF16),
                   jax.ShapeDtypeStruct((groups, SWA_HEAD_DIM, m), BF16)),
        grid=(m // tm,),
        in_specs=[pl.BlockSpec((tm, k), lambda i: (i, 0)), pl.BlockSpec((k, n), lambda i: (0, 0))],
        out_specs=(pl.BlockSpec((groups, tm, SWA_HEAD_DIM), lambda i: (0, i, 0)),
                   pl.BlockSpec((groups, SWA_HEAD_DIM, tm), lambda i: (0, 0, i))),
        compiler_params=_params(("arbitrary",), blocks, extra=3 * _nbytes((tm, n), F32)),
        name="swa_kv",
    )(x, w)


def _swa_attn_kernel(*refs, rep, n_riders):
    bias_ref, sink_ref, q_ref, kp_ref, kc_ref, vtp_ref, vtc_ref = refs[:7]
    o_ref = refs[7 + n_riders]
    _run_cast_riders(refs[7:7 + n_riders], refs[8 + n_riders:8 + 2 * n_riders])
    n = pl.program_id(2)
    w = WINDOW
    hd = SWA_HEAD_DIM
    nq = q_ref.shape[0] // w
    kfull = jnp.concatenate([kp_ref[0], kc_ref[0]], axis=0)
    vtfull = jnp.concatenate([vtp_ref[0], vtc_ref[0]], axis=1)
    sink = sink_ref[0]
    first = jnp.minimum(n, 1)
    for c in range(nq):
        qt = q_ref[c * w:(c + 1) * w, :].astype(F32).T
        qt = jnp.concatenate([qt[r * hd:(r + 1) * hd, :] for r in range(rep)], axis=1).astype(BF16)
        bias = bias_ref[0, first] if c == 0 else bias_ref[0, 1]
        st = jnp.dot(kfull[c * w:(c + 2) * w], qt, preferred_element_type=F32) + bias
        mx = jnp.maximum(jnp.max(st, axis=0, keepdims=True), sink)
        e = jnp.exp2(st - mx)
        denom = jnp.sum(e, axis=0, keepdims=True) + jnp.exp2(sink - mx)
        ot = jnp.dot(vtfull[:, c * w:(c + 2) * w], e.astype(BF16), preferred_element_type=F32)
        ot = ot * (1.0 / denom)
        o = jnp.concatenate([ot[:, r * w:(r + 1) * w] for r in range(rep)], axis=0)
        o_ref[c * w:(c + 1) * w, :] = o.T.astype(o_ref.dtype)


def _swa_attention(q, k, vt, bias, sink_lanes, batch, seq, ride_weights, ride_layer):
    m, dq = q.shape
    groups = k.shape[0]
    rep = dq // (groups * SWA_HEAD_DIM)
    w = WINDOW
    nq = 4
    tq = nq * w
    nb = seq // tq
    gw = rep * SWA_HEAD_DIM
    prev_blk = lambda b, n: jnp.maximum((b * nb + n) * nq - 1, b * nb * nq)
    r_in, r_shapes, r_out, r_blocks = _cast_riders(ride_weights, ride_layer, groups * batch * nb,
                                                   lambda g, b, n: (g * batch + b) * nb + n)
    blocks = [((2, 2 * w, rep * w), F32), ((tq, gw), BF16), ((tq, gw), BF16)] + r_blocks
    return pl.pallas_call(
        functools.partial(_swa_attn_kernel, rep=rep, n_riders=len(ride_weights)),
        out_shape=(jax.ShapeDtypeStruct((m, dq), BF16), *r_shapes),
        grid=(groups, batch, nb),
        in_specs=[pl.BlockSpec((1, 2, 2 * w, rep * w), lambda g, b, n: (g, 0, 0, 0)),
                  pl.BlockSpec((1, 1, rep * w), lambda g, b, n: (g, 0, 0)),
                  pl.BlockSpec((tq, gw), lambda g, b, n: (b * nb + n, g)),
                  pl.BlockSpec((1, w, SWA_HEAD_DIM), lambda g, b, n: (g, prev_blk(b, n), 0)),
                  pl.BlockSpec((1, tq, SWA_HEAD_DIM), lambda g, b, n: (g, b * nb + n, 0)),
                  pl.BlockSpec((1, SWA_HEAD_DIM, w), lambda g, b, n: (g, 0, prev_blk(b, n))),
                  pl.BlockSpec((1, SWA_HEAD_DIM, tq), lambda g, b, n: (g, 0, b * nb + n)), *r_in],
        out_specs=(pl.BlockSpec((tq, gw), lambda g, b, n: (b * nb + n, g)), *r_out),
        compiler_params=_params(("arbitrary", "arbitrary", "arbitrary"), blocks,
                                extra=6 * nq * _nbytes((2 * w, rep * w), F32)),
        name="swa_attention",
    )(bias, sink_lanes, q, k, k, vt, vt, *ride_weights)


def _swa_bias(swa_heads, groups):
    w = WINDOW
    rep = swa_heads // groups
    slopes = 2.0 ** (-8.0 * jnp.arange(1, swa_heads + 1, dtype=F32) / swa_heads)
    kj = jnp.arange(2 * w)[:, None]
    qi = jnp.arange(w)[None, :]
    dist = qi + w - kj
    in_window = (dist >= 0) & (dist < w)
    valid = jnp.stack([in_window & (kj >= w), in_window])
    alibi = slopes.reshape(groups, 1, 1, rep, 1) * dist.astype(F32)[None, None, :, None, :]
    bias = jnp.where(valid[None, :, :, None, :], -alibi * LOG2_E, -jnp.inf)
    return bias.reshape(groups, 2, 2 * w, rep * w)


def kernel(x, positions, norm_mix_pre, norm_mix_post, norm_ffn_pre, norm_ffn_post, mla_w_dq, mla_q_norm,
           mla_w_uq, mla_w_dkv, mla_kv_norm, mla_w_ukv, mla_w_o, shared_kv_norm, swa_w_k, swa_w_v, swa_w_q,
           swa_sinks, swa_w_o, ffn_w_in, ffn_conv_w, ffn_conv_b, ffn_w_out):
    batch, seq, d = x.shape
    m = batch * seq
    depth = norm_mix_pre.shape[0]
    n_a = mla_w_dq.shape[0]
    heads = mla_w_o.shape[1] // MLA_V
    swa_heads = swa_w_q.shape[2] // SWA_HEAD_DIM
    hpg = 4

    half = MLA_ROPE // 2
    freqs = ROPE_THETA ** (-jnp.arange(half, dtype=F32) / half)
    freqs2 = jnp.concatenate([freqs, freqs])[None, :]
    cosf, sinf = _rope_tables(positions.reshape(m, 1), freqs2)

    row = lambda v: v.reshape(1, -1)
    h = x.reshape(m, d)
    assert depth == 2 and n_a == 1, "wired for one MLA layer followed by one sliding-window layer"

    perm = _interleave_permutation()

    def ffn(l, h, hn_ffn, w_in, g_next):
        act, w_out = _ffn_up(hn_ffn, w_in, ffn_conv_w[l], row(ffn_conv_b[l]), perm.T, ffn_w_out, l, seq)
        f = _matmul(act, w_out, F32, 512, 512, name="ffn_down")
        return _residual_norm(f, h, row(norm_ffn_post[l]), g_next)

    w_uq = mla_w_uq[0].astype(BF16)
    w_uq3 = w_uq.reshape(w_uq.shape[0], heads // hpg, hpg * MLA_QK).transpose(1, 0, 2)
    w_ukv = mla_w_ukv[0].astype(BF16)
    w_ukv3 = w_ukv.reshape(w_ukv.shape[0], heads // hpg, hpg * (MLA_NOPE + MLA_V)).transpose(1, 0, 2)
    cq, c, kr = _mla_down(h, row(norm_mix_pre[0]), mla_w_dq[0].astype(BF16), row(mla_q_norm[0]),
                          mla_w_dkv[0].astype(BF16), row(mla_kv_norm[0]), cosf, sinf)
    q = _mla_q_up(cq, w_uq3, cosf, sinf, batch, seq, MLA_QK ** -0.5 * LOG2_E)
    k, vt = _mla_kv_up(c, w_ukv3, kr, batch, seq)
    o, w_in0 = _mla_attention(q, k, vt, 4, [ffn_w_in], 0)
    a = _matmul(o.reshape(m, heads * MLA_V), mla_w_o[0].astype(BF16), F32, 1024, 512, name="mla_out")
    h, hn_ffn = _residual_norm(a, h, row(norm_mix_post[0]), [row(norm_ffn_pre[0])], perm=perm)
    h, hn_mix, hn_kv = ffn(0, h, hn_ffn, w_in0, [row(norm_mix_pre[1]), row(shared_kv_norm)])

    w_kv = jnp.concatenate([swa_w_k, swa_w_v], axis=1).astype(BF16)
    k_shared, vt_shared = _swa_kv(hn_kv, w_kv)
    groups = k_shared.shape[0]
    sink_lanes = jnp.repeat(swa_sinks[0].reshape(groups, 1, swa_heads // groups) * LOG2_E, WINDOW, axis=2)
    q = _matmul(hn_mix, swa_w_q[0].astype(BF16), BF16, 1024, 512,
                scale=SWA_HEAD_DIM ** -0.5 * LOG2_E, name="swa_q")
    o, w_in1 = _swa_attention(q, k_shared, vt_shared, _swa_bias(swa_heads, groups), sink_lanes,
                              batch, seq, [ffn_w_in], 1)
    a = _matmul(o, swa_w_o[0].astype(BF16), F32, 1024, 512, name="swa_out")
    h, hn_ffn = _residual_norm(a, h, row(norm_mix_post[1]), [row(norm_ffn_pre[1])], perm=perm)
    (h,) = ffn(1, h, hn_ffn, w_in1, [])
    return h.reshape(batch, seq, d)
```

```python
import functools

import jax
import jax.numpy as jnp
from jax import lax
from jax.experimental import pallas as pl
from jax.experimental.pallas import tpu as pltpu

EPS = 1e-6
LOG2_E = 1.4426950408889634
ROPE_THETA = 10000.0
MLA_NOPE = 128
MLA_ROPE = 64
MLA_V = 128
MLA_QK = MLA_NOPE + MLA_ROPE
MLA_QK_PAD = MLA_NOPE + 2 * MLA_ROPE
SWA_HEAD_DIM = 64
SWA_KV_HEADS = 8
WINDOW = 128
CONV_W = 3

V7X_VMEM_BYTES = 64 * 1024 * 1024
VMEM_LIMIT_CAP = V7X_VMEM_BYTES - 8 * 1024 * 1024
SUBLANES = 8
PERM_UNIT = 256
PERM_GROUPS = PERM_UNIT // SUBLANES

BF16 = jnp.bfloat16
F32 = jnp.float32


def _nbytes(shape, dtype):
    n = 1
    for s in shape:
        n *= s
    return n * jnp.dtype(dtype).itemsize


def _params(semantics, blocks, extra=0):
    est = 2 * sum(_nbytes(s, d) for s, d in blocks) + extra
    limit = min(VMEM_LIMIT_CAP, max(est + est // 4, 16 * 1024 * 1024))
    return pltpu.CompilerParams(dimension_semantics=semantics, vmem_limit_bytes=limit)


def _rms(x, g):
    r = lax.rsqrt(jnp.mean(x * x, axis=-1, keepdims=True) + EPS)
    return (x * r) * g


def _rope_table_kernel(pos_ref, freq_ref, cos_ref, sin_ref):
    ang = pos_ref[...].astype(F32) * freq_ref[...]
    lane = lax.broadcasted_iota(jnp.int32, ang.shape, 1)
    s = jnp.sin(ang)
    cos_ref[...] = jnp.cos(ang)
    sin_ref[...] = jnp.where(lane % MLA_ROPE < MLA_ROPE // 2, -s, s)


def _rope_pair(r, cos2, sin2):
    lane = lax.broadcasted_iota(jnp.int32, r.shape, 1)
    half = MLA_ROPE // 2
    partner = jnp.where(lane % MLA_ROPE < half, pltpu.roll(r, r.shape[1] - half, 1), pltpu.roll(r, half, 1))
    return r * cos2 + partner * sin2


def _rope_tables(pos_col, freqs2):
    m = pos_col.shape[0]
    d = freqs2.shape[1]
    tm = 1024
    return pl.pallas_call(
        _rope_table_kernel,
        out_shape=(jax.ShapeDtypeStruct((m, d), F32), jax.ShapeDtypeStruct((m, d), F32)),
        grid=(m // tm,),
        in_specs=[pl.BlockSpec((tm, 1), lambda i: (i, 0)), pl.BlockSpec((1, d), lambda i: (0, 0))],
        out_specs=(pl.BlockSpec((tm, d), lambda i: (i, 0)), pl.BlockSpec((tm, d), lambda i: (i, 0))),
        name="rope_tables",
    )(pos_col, freqs2)


def _mla_down_kernel(x_ref, g_ref, wdq_ref, qn_ref, wdkv_ref, kvn_ref, cos_ref, sin_ref,
                     cq_ref, c_ref, kr_ref):
    hn = _rms(x_ref[...], g_ref[...]).astype(BF16)
    cq = jnp.dot(hn, wdq_ref[...], preferred_element_type=F32)
    cq_ref[...] = _rms(cq, qn_ref[...]).astype(BF16)
    ckv = jnp.dot(hn, wdkv_ref[...], preferred_element_type=F32)
    lora = c_ref.shape[-1]
    c_ref[...] = _rms(ckv[:, :lora], kvn_ref[...]).astype(BF16)
    kr_ref[...] = _rope_pair(ckv[:, lora:], cos_ref[...], sin_ref[...]).astype(BF16)


def _mla_down(x2, g, wdq, qn, wdkv, kvn, cosf, sinf):
    m, d = x2.shape
    nq = wdq.shape[1]
    nkv = wdkv.shape[1]
    lora = kvn.shape[1]
    rd = nkv - lora
    tm = 256
    row = lambda i: (i, 0)
    fixed = lambda i: (0, 0)
    blocks = [((tm, d), F32), ((d, nq), BF16), ((d, nkv), BF16), ((tm, nq), BF16), ((tm, lora), BF16)]
    return pl.pallas_call(
        _mla_down_kernel,
        out_shape=(jax.ShapeDtypeStruct((m, nq), BF16), jax.ShapeDtypeStruct((m, lora), BF16),
                   jax.ShapeDtypeStruct((m, rd), BF16)),
        grid=(m // tm,),
        in_specs=[pl.BlockSpec((tm, d), row), pl.BlockSpec((1, d), fixed),
                  pl.BlockSpec((d, nq), fixed), pl.BlockSpec((1, nq), fixed),
                  pl.BlockSpec((d, nkv), fixed), pl.BlockSpec((1, lora), fixed),
                  pl.BlockSpec((tm, rd), row), pl.BlockSpec((tm, rd), row)],
        out_specs=(pl.BlockSpec((tm, nq), row), pl.BlockSpec((tm, lora), row), pl.BlockSpec((tm, rd), row)),
        compiler_params=_params(("arbitrary",), blocks, extra=4 * _nbytes((tm, d), F32)),
        name="mla_down",
    )(x2, g, wdq, qn, wdkv, kvn, cosf, sinf)


def _mla_q_up_kernel(cq_ref, w_ref, cos_ref, sin_ref, q_ref, *, scale):
    res = jnp.dot(cq_ref[...], w_ref[0], preferred_element_type=F32)
    hpg = q_ref.shape[1]
    cos2 = cos_ref[...]
    sin2 = sin_ref[...]
    for hh in range(hpg):
        q_ref[0, hh, :, :MLA_NOPE] = (res[:, hh * MLA_NOPE:(hh + 1) * MLA_NOPE] * scale).astype(BF16)
    low = lax.broadcasted_iota(jnp.int32, cos2.shape, 1) < MLA_ROPE
    for p in range(hpg // 2):
        base = hpg * MLA_NOPE + p * 2 * MLA_ROPE
        r = _rope_pair(res[:, base:base + 2 * MLA_ROPE], cos2, sin2) * scale
        q_ref[0, 2 * p, :, MLA_NOPE:] = jnp.where(low, r, 0.0).astype(BF16)
        q_ref[0, 2 * p + 1, :, MLA_NOPE:] = jnp.where(low, 0.0, r).astype(BF16)


def _mla_q_up(cq, w3, cos2, sin2, batch, seq, scale):
    m, k = cq.shape
    groups, _, gn = w3.shape
    hpg = gn // MLA_QK
    heads = groups * hpg
    tm = 512
    nsb = seq // tm
    blocks = [((tm, k), BF16), ((k, gn), BF16), ((hpg, tm, MLA_QK_PAD), BF16)]
    return pl.pallas_call(
        functools.partial(_mla_q_up_kernel, scale=scale),
        out_shape=jax.ShapeDtypeStruct((batch, heads, seq, MLA_QK_PAD), BF16),
        grid=(m // tm, groups),
        in_specs=[pl.BlockSpec((tm, k), lambda i, g: (i, 0)),
                  pl.BlockSpec((1, k, gn), lambda i, g: (g, 0, 0)),
                  pl.BlockSpec((tm, 2 * MLA_ROPE), lambda i, g: (i, 0)),
                  pl.BlockSpec((tm, 2 * MLA_ROPE), lambda i, g: (i, 0))],
        out_specs=pl.BlockSpec((1, hpg, tm, MLA_QK_PAD), lambda i, g: (i // nsb, g, i % nsb, 0)),
        compiler_params=_params(("arbitrary", "arbitrary"), blocks, extra=2 * _nbytes((tm, gn), F32)),
        name="mla_q_up",
    )(cq, w3, cos2, sin2)


def _mla_kv_up_kernel(c_ref, w_ref, kr_ref, k_ref, vt_ref):
    res = jnp.dot(c_ref[...], w_ref[0], preferred_element_type=F32)
    kr = kr_ref[...]
    per = MLA_NOPE + MLA_V
    for hh in range(k_ref.shape[1]):
        base = hh * per
        k_ref[0, hh, :, :MLA_NOPE] = res[:, base:base + MLA_NOPE].astype(BF16)
        k_ref[0, hh, :, MLA_NOPE:] = kr
        vt_ref[0, hh] = res[:, base + MLA_NOPE:base + per].T.astype(BF16)


def _mla_kv_up(c, w3, kr, batch, seq):
    m, k = c.shape
    groups, _, gn = w3.shape
    per = MLA_NOPE + MLA_V
    hpg = gn // per
    heads = groups * hpg
    tm = 512
    nsb = seq // tm
    blocks = [((tm, k), BF16), ((k, gn), BF16), ((hpg, tm, MLA_QK_PAD), BF16), ((hpg, tm, MLA_V), BF16)]
    omap = lambda i, g: (i // nsb, g, i % nsb, 0)
    vmap = lambda i, g: (i // nsb, g, 0, i % nsb)
    return pl.pallas_call(
        _mla_kv_up_kernel,
        out_shape=(jax.ShapeDtypeStruct((batch, heads, seq, MLA_QK_PAD), BF16),
                   jax.ShapeDtypeStruct((batch, heads, MLA_V, seq), BF16)),
        grid=(m // tm, groups),
        in_specs=[pl.BlockSpec((tm, k), lambda i, g: (i, 0)),
                  pl.BlockSpec((1, k, gn), lambda i, g: (g, 0, 0)),
                  pl.BlockSpec((tm, 2 * MLA_ROPE), lambda i, g: (i, 0))],
        out_specs=(pl.BlockSpec((1, hpg, tm, MLA_QK_PAD), omap), pl.BlockSpec((1, hpg, MLA_V, tm), vmap)),
        compiler_params=_params(("arbitrary", "arbitrary"), blocks, extra=2 * _nbytes((tm, gn), F32)),
        name="mla_kv_up",
    )(c, w3, kr)


def _cast_riders(weights, layer, nsteps, step_of):
    in_specs, out_shapes, out_specs, blocks = [], [], [], []
    for w in weights:
        _, rows, cols = w.shape
        slab = rows // nsteps
        assert slab * nsteps == rows and slab % (2 * SUBLANES) == 0, (w.shape, nsteps)
        in_specs.append(pl.BlockSpec((None, slab, cols), lambda *g: (layer, step_of(*g), 0)))
        out_shapes.append(jax.ShapeDtypeStruct((rows, cols), BF16))
        out_specs.append(pl.BlockSpec((slab, cols), lambda *g: (step_of(*g), 0)))
        blocks += [((slab, cols), F32), ((slab, cols), BF16)]
    return in_specs, out_shapes, out_specs, blocks


def _run_cast_riders(src_refs, dst_refs):
    for src, dst in zip(src_refs, dst_refs):
        dst[...] = src[...].astype(dst.dtype)


def _mla_attn_kernel(*refs, tk, n_riders):
    q_ref, k_ref, vt_ref = refs[:3]
    o_ref = refs[3 + n_riders]
    sa_ref, sb_ref, m_ref, l_ref, acc_ref = refs[4 + 2 * n_riders:]
    _run_cast_riders(refs[3:3 + n_riders], refs[4 + n_riders:4 + 2 * n_riders])
    i = pl.program_id(2)
    hp = q_ref.shape[1]
    dv = acc_ref.shape[1]
    m_ref[...] = jnp.full(m_ref.shape, -jnp.inf, F32)
    l_ref[...] = jnp.zeros(l_ref.shape, F32)
    acc_ref[...] = jnp.zeros(acc_ref.shape, F32)

    def scores(j, s_ref):
        start = pl.multiple_of(j * tk, tk)
        for hh in range(hp):
            k = k_ref[0, hh, pl.ds(start, tk), :]
            s_ref[hh] = lax.dot_general(k, q_ref[0, hh], (((1,), (1,)), ((), ())),
                                        preferred_element_type=F32)

    def update(j, s_ref, masked):
        start = pl.multiple_of(j * tk, tk)
        for hh in range(hp):
            st = s_ref[hh]
            if masked:
                kj = lax.broadcasted_iota(jnp.int32, st.shape, 0)
                qi = lax.broadcasted_iota(jnp.int32, st.shape, 1)
                st = jnp.where(kj <= qi, st, -jnp.inf)
            vt = vt_ref[0, hh, :, pl.ds(start, tk)]
            m_prev = m_ref[hh]
            m_new = jnp.maximum(m_prev, jnp.max(st, axis=0, keepdims=True))
            alpha = jnp.exp2(m_prev - m_new)
            pt = jnp.exp2(st - m_new)
            l_ref[hh] = alpha * l_ref[hh] + jnp.sum(pt, axis=0, keepdims=True)
            acc_ref[hh] = alpha * acc_ref[hh] + jnp.dot(vt, pt.astype(BF16), preferred_element_type=F32)
            m_ref[hh] = m_new

    def finish():
        for hh in range(hp):
            o = acc_ref[hh] / l_ref[hh]
            o_ref[0, :, hh * dv:(hh + 1) * dv] = o.T.astype(o_ref.dtype)

    scores(0, sa_ref)

    def pair(jj, carry):
        j = 2 * jj
        scores(j + 1, sb_ref)
        update(j, sa_ref, False)
        scores(j + 2, sa_ref)
        update(j + 1, sb_ref, False)
        return carry

    lax.fori_loop(0, i // 2, pair, 0)

    @pl.when(i % 2 == 0)
    def _():
        update(i, sa_ref, True)
        finish()

    @pl.when(i % 2 == 1)
    def _():
        scores(i, sb_ref)
        update(i - 1, sa_ref, False)
        update(i, sb_ref, True)
        finish()


def _mla_attention(q, k, vt, hp, ride_weights, ride_layer):
    batch, heads, seq, dqk = q.shape
    dv = vt.shape[2]
    tq = 512
    nh, nq = heads // hp, seq // tq
    r_in, r_shapes, r_out, r_blocks = _cast_riders(ride_weights, ride_layer, batch * nh * nq,
                                                   lambda b, h, i: (b * nh + h) * nq + i)
    blocks = [((hp, tq, 256), BF16), ((hp, seq, 256), BF16), ((hp, dv, seq), BF16), ((tq, hp * dv), BF16)]
    return pl.pallas_call(
        functools.partial(_mla_attn_kernel, tk=tq, n_riders=len(ride_weights)),
        out_shape=(jax.ShapeDtypeStruct((batch, seq, heads * dv), BF16), *r_shapes),
        grid=(batch, nh, nq),
        in_specs=[pl.BlockSpec((1, hp, tq, dqk), lambda b, h, i: (b, h, i, 0)),
                  pl.BlockSpec((1, hp, seq, dqk), lambda b, h, i: (b, h, 0, 0)),
                  pl.BlockSpec((1, hp, dv, seq), lambda b, h, i: (b, h, 0, 0)), *r_in],
        out_specs=(pl.BlockSpec((1, tq, hp * dv), lambda b, h, i: (b, i, h)), *r_out),
        scratch_shapes=[pltpu.VMEM((hp, tq, tq), F32), pltpu.VMEM((hp, tq, tq), F32),
                        pltpu.VMEM((hp, 1, tq), F32), pltpu.VMEM((hp, 1, tq), F32),
                        pltpu.VMEM((hp, dv, tq), F32)],
        compiler_params=_params(("arbitrary", "arbitrary", "arbitrary"), blocks + r_blocks,
                                extra=8 * hp * _nbytes((tq, tq), F32)),
        name="mla_attention",
    )(q, k, vt, *ride_weights)


def _matmul_kernel(x_ref, w_ref, o_ref, *, scale):
    acc = jnp.dot(x_ref[...], w_ref[...], preferred_element_type=F32)
    if scale is not None:
        acc = acc * scale
    o_ref[...] = acc.astype(o_ref.dtype)


def _matmul(x, w, out_dtype, tm, tn, scale=None, name="matmul"):
    m, k = x.shape
    n = w.shape[1]
    blocks = [((tm, k), x.dtype), ((k, tn), w.dtype), ((tm, tn), out_dtype)]
    return pl.pallas_call(
        functools.partial(_matmul_kernel, scale=scale),
        out_shape=jax.ShapeDtypeStruct((m, n), out_dtype),
        grid=(m // tm, n // tn),
        in_specs=[pl.BlockSpec((tm, k), lambda i, j: (i, 0)), pl.BlockSpec((k, tn), lambda i, j: (0, j))],
        out_specs=pl.BlockSpec((tm, tn), lambda i, j: (i, j)),
        compiler_params=_params(("arbitrary", "arbitrary"), blocks, extra=2 * _nbytes((tm, tn), F32)),
        name=name,
    )(x, w)


def _residual_norm_kernel(*refs, n_next, interleave):
    a_ref, h_ref, gp_ref = refs[:3]
    gn_refs = refs[3:3 + n_next]
    perm_ref = refs[3 + n_next] if interleave else None
    out_refs = refs[3 + n_next + (1 if interleave else 0):]
    h_new = h_ref[...] + _rms(a_ref[...], gp_ref[...])
    out_refs[0][...] = h_new
    for gn_ref, o_ref in zip(gn_refs, out_refs[1:]):
        hn = _rms(h_new, gn_ref[...]).astype(o_ref.dtype)
        if interleave:
            hn = jnp.dot(perm_ref[...], hn, preferred_element_type=F32).astype(o_ref.dtype)
        o_ref[...] = hn


def _residual_norm(a, h, g_post, g_next, perm=None):
    m, d = a.shape
    n_next = len(g_next)
    tm = PERM_UNIT
    row = lambda i: (i, 0)
    fixed = lambda i: (0, 0)
    interleave = perm is not None
    out_shape = [jax.ShapeDtypeStruct((m, d), F32)] + [jax.ShapeDtypeStruct((m, d), BF16)] * n_next
    out_specs = [pl.BlockSpec((tm, d), row)] * (1 + n_next)
    blocks = [((tm, d), F32)] * (4 + n_next)
    perm_specs = [pl.BlockSpec((tm, tm), fixed)] if interleave else []
    perm_args = [perm] if interleave else []
    return pl.pallas_call(
        functools.partial(_residual_norm_kernel, n_next=n_next, interleave=interleave),
        out_shape=tuple(out_shape),
        grid=(m // tm,),
        in_specs=[pl.BlockSpec((tm, d), row), pl.BlockSpec((tm, d), row)]
        + [pl.BlockSpec((1, d), fixed)] * (1 + n_next) + perm_specs,
        out_specs=tuple(out_specs),
        compiler_params=_params(("arbitrary",), blocks),
        name="residual_norm",
    )(a, h, g_post, *g_next, *perm_args)


def _interleave_permutation():
    r = jnp.arange(PERM_UNIT)
    t = (r % SUBLANES) * PERM_GROUPS + r // SUBLANES
    return (t[:, None] == jnp.arange(PERM_UNIT)[None, :]).astype(BF16)


def _ffn_up_kernel(x_ref, wg_ref, wv_ref, cwg_ref, cwv_ref, cbg_ref, cbv_ref, pt_ref, wdn_ref, o_ref,
                   wdn_bf_ref, carry_ref, *, blocks_per_seq):
    _run_cast_riders([wdn_ref], [wdn_bf_ref])
    i = pl.program_id(0)
    j = pl.program_id(1)
    tn = wg_ref.shape[1]
    tm = x_ref.shape[0]
    grp = SUBLANES

    @pl.when((i % blocks_per_seq) == 0)
    def _():
        carry_ref[j] = jnp.zeros(carry_ref.shape[1:], F32)

    w = jnp.concatenate([wg_ref[...], wv_ref[...]], axis=1)
    cw = jnp.concatenate([cwg_ref[...], cwv_ref[...]], axis=1)
    cb = jnp.concatenate([cbg_ref[...], cbv_ref[...]], axis=1)
    first_sublane = lax.broadcasted_iota(jnp.int32, (grp, 2 * tn), 0) == 0
    tail = carry_ref[j]

    def project(c):
        return jnp.dot(x_ref[c * PERM_UNIT:(c + 1) * PERM_UNIT, :], w, preferred_element_type=F32)

    nunits = tm // PERM_UNIT
    u_next = project(0)
    for c in range(nunits):
        rows = slice(c * PERM_UNIT, (c + 1) * PERM_UNIT)
        u = u_next
        if c + 1 < nunits:
            u_next = project(c + 1)
        new_tail = u[PERM_UNIT - 2 * grp:, :]
        wrapped = jnp.where(jnp.concatenate([first_sublane, first_sublane], axis=0),
                            jnp.concatenate([pltpu.roll(tail[:grp], 1, 0), pltpu.roll(tail[grp:], 1, 0)], axis=0),
                            jnp.concatenate([pltpu.roll(new_tail[:grp], 1, 0), pltpu.roll(new_tail[grp:], 1, 0)],
                                            axis=0))
        ext = jnp.concatenate([wrapped, u], axis=0)
        acc = cw[0:1, :] * ext[0:PERM_UNIT] + cw[1:2, :] * ext[grp:grp + PERM_UNIT] + cw[2:3, :] * u
        conv = cb + acc
        gate, val = conv[:, :tn], conv[:, tn:]
        act = (gate * (1.0 / (1.0 + jnp.exp(-gate))) * val).astype(o_ref.dtype)
        o_ref[rows, :] = jnp.dot(pt_ref[...], act, preferred_element_type=F32).astype(o_ref.dtype)
        tail = new_tail
    carry_ref[j] = tail


def _ffn_up(hn, w_in, conv_w, conv_b, perm_t, w_out_all, layer, seq):
    m, d = hn.shape
    d_ff = w_in.shape[1] // 2
    tn = 256
    nj = d_ff // tn
    tm = 1024
    r_in, r_shapes, r_out, r_blocks = _cast_riders([w_out_all], layer, (m // tm) * nj, lambda i, j: i * nj + j)
    blocks = [((tm, d), BF16), ((d, tn), BF16), ((d, tn), BF16), ((tm, tn), BF16)] + r_blocks
    scratch = _nbytes((nj, 2 * SUBLANES, 2 * tn), F32)
    gate_col = lambda i, j: (0, j)
    val_col = lambda i, j: (0, j + nj)
    return pl.pallas_call(
        functools.partial(_ffn_up_kernel, blocks_per_seq=seq // tm),
        out_shape=(jax.ShapeDtypeStruct((m, d_ff), BF16), *r_shapes),
        grid=(m // tm, nj),
        in_specs=[pl.BlockSpec((tm, d), lambda i, j: (i, 0)),
                  pl.BlockSpec((d, tn), gate_col), pl.BlockSpec((d, tn), val_col),
                  pl.BlockSpec((CONV_W, tn), gate_col), pl.BlockSpec((CONV_W, tn), val_col),
                  pl.BlockSpec((1, tn), gate_col), pl.BlockSpec((1, tn), val_col),
                  pl.BlockSpec((PERM_UNIT, PERM_UNIT), lambda i, j: (0, 0)), *r_in],
        out_specs=(pl.BlockSpec((tm, tn), lambda i, j: (i, j)), *r_out),
        scratch_shapes=[pltpu.VMEM((nj, 2 * SUBLANES, 2 * tn), F32)],
        compiler_params=_params(("arbitrary", "arbitrary"), blocks,
                                extra=scratch + 10 * _nbytes((PERM_UNIT, 2 * tn), F32)),
        name="ffn_up",
    )(hn, w_in, w_in, conv_w, conv_w, conv_b, conv_b, perm_t, w_out_all)


def _swa_kv_kernel(x_ref, w_ref, k_ref, vt_ref):
    res = jnp.dot(x_ref[...], w_ref[...], preferred_element_type=F32)
    groups = k_ref.shape[0]
    hd = SWA_HEAD_DIM
    for t in range(groups):
        k_ref[t] = res[:, t * hd:(t + 1) * hd].astype(k_ref.dtype)
    vt = res[:, groups * hd:].T
    for t in range(groups):
        vt_ref[t] = vt[t * hd:(t + 1) * hd, :].astype(vt_ref.dtype)


def _swa_kv(x, w):
    m, k = x.shape
    n = w.shape[1]
    groups = n // (2 * SWA_HEAD_DIM)
    tm = 512
    blocks = [((tm, k), BF16), ((k, n), BF16), ((groups, tm, 128), BF16), ((groups, SWA_HEAD_DIM, tm), BF16)]
    return pl.pallas_call(
        _swa_kv_kernel,
        out_shape=(jax.ShapeDtypeStruct((groups, m, SWA_HEAD_DIM), BF16),
                   jax.ShapeDtypeStruct((groups, SWA_HEAD_DIM, m), BF16)),
        grid=(m // tm,),
        in_specs=[pl.BlockSpec((tm, k), lambda i: (i, 0)), pl.BlockSpec((k, n), lambda i: (0, 0))],
        out_specs=(pl.BlockSpec((groups, tm, SWA_HEAD_DIM), lambda i: (0, i, 0)),
                   pl.BlockSpec((groups, SWA_HEAD_DIM, tm), lambda i: (0, 0, i))),
        compiler_params=_params(("arbitrary",), blocks, extra=3 * _nbytes((tm, n), F32)),
        name="swa_kv",
    )(x, w)


def _swa_attn_kernel(*refs, rep, n_riders):
    bias_ref, sink_ref, q_ref, kp_ref, kc_ref, vtp_ref, vtc_ref = refs[:7]
    o_ref = refs[7 + n_riders]
    _run_cast_riders(refs[7:7 + n_riders], refs[8 + n_riders:8 + 2 * n_riders])
    n = pl.program_id(2)
    w = WINDOW
    hd = SWA_HEAD_DIM
    nq = q_ref.shape[0] // w
    kfull = jnp.concatenate([kp_ref[0], kc_ref[0]], axis=0)
    vtfull = jnp.concatenate([vtp_ref[0], vtc_ref[0]], axis=1)
    sink = sink_ref[0]
    first = jnp.minimum(n, 1)
    for c in range(nq):
        qt = q_ref[c * w:(c + 1) * w, :].astype(F32).T
        qt = jnp.concatenate([qt[r * hd:(r + 1) * hd, :] for r in range(rep)], axis=1).astype(BF16)
        bias = bias_ref[0, first] if c == 0 else bias_ref[0, 1]
        st = jnp.dot(kfull[c * w:(c + 2) * w], qt, preferred_element_type=F32) + bias
        mx = jnp.maximum(jnp.max(st, axis=0, keepdims=True), sink)
        e = jnp.exp2(st - mx)
        denom = jnp.sum(e, axis=0, keepdims=True) + jnp.exp2(sink - mx)
        ot = jnp.dot(vtfull[:, c * w:(c + 2) * w], e.astype(BF16), preferred_element_type=F32)
        ot = ot * (1.0 / denom)
        o = jnp.concatenate([ot[:, r * w:(r + 1) * w] for r in range(rep)], axis=0)
        o_ref[c * w:(c + 1) * w, :] = o.T.astype(o_ref.dtype)


def _swa_attention(q, k, vt, bias, sink_lanes, batch, seq, ride_weights, ride_layer):
    m, dq = q.shape
    groups = k.shape[0]
    rep = dq // (groups * SWA_HEAD_DIM)
    w = WINDOW
    nq = 4
    tq = nq * w
    nb = seq // tq
    gw = rep * SWA_HEAD_DIM
    prev_blk = lambda b, n: jnp.maximum((b * nb + n) * nq - 1, b * nb * nq)
    r_in, r_shapes, r_out, r_blocks = _cast_riders(ride_weights, ride_layer, groups * batch * nb,
                                                   lambda g, b, n: (g * batch + b) * nb + n)
    blocks = [((2, 2 * w, rep * w), F32), ((tq, gw), BF16), ((tq, gw), BF16)] + r_blocks
    return pl.pallas_call(
        functools.partial(_swa_attn_kernel, rep=rep, n_riders=len(ride_weights)),
        out_shape=(jax.ShapeDtypeStruct((m, dq), BF16), *r_shapes),
        grid=(groups, batch, nb),
        in_specs=[pl.BlockSpec((1, 2, 2 * w, rep * w), lambda g, b, n: (g, 0, 0, 0)),
                  pl.BlockSpec((1, 1, rep * w), lambda g, b, n: (g, 0, 0)),
                  pl.BlockSpec((tq, gw), lambda g, b, n: (b * nb + n, g)),
                  pl.BlockSpec((1, w, SWA_HEAD_DIM), lambda g, b, n: (g, prev_blk(b, n), 0)),
                  pl.BlockSpec((1, tq, SWA_HEAD_DIM), lambda g, b, n: (g, b * nb + n, 0)),
                  pl.BlockSpec((1, SWA_HEAD_DIM, w), lambda g, b, n: (g, 0, prev_blk(b, n))),
                  pl.BlockSpec((1, SWA_HEAD_DIM, tq), lambda g, b, n: (g, 0, b * nb + n)), *r_in],
        out_specs=(pl.BlockSpec((tq, gw), lambda g, b, n: (b * nb + n, g)), *r_out),
        compiler_params=_params(("arbitrary", "arbitrary", "arbitrary"), blocks,
                                extra=6 * nq * _nbytes((2 * w, rep * w), F32)),
        name="swa_attention",
    )(bias, sink_lanes, q, k, k, vt, vt, *ride_weights)


def _swa_bias(swa_heads, groups):
    w = WINDOW
    rep = swa_heads // groups
    slopes = 2.0 ** (-8.0 * jnp.arange(1, swa_heads + 1, dtype=F32) / swa_heads)
    kj = jnp.arange(2 * w)[:, None]
    qi = jnp.arange(w)[None, :]
    dist = qi + w - kj
    in_window = (dist >= 0) & (dist < w)
    valid = jnp.stack([in_window & (kj >= w), in_window])
    alibi = slopes.reshape(groups, 1, 1, rep, 1) * dist.astype(F32)[None, None, :, None, :]
    bias = jnp.where(valid[None, :, :, None, :], -alibi * LOG2_E, -jnp.inf)
    return bias.reshape(groups, 2, 2 * w, rep * w)


def kernel(x, positions, norm_mix_pre, norm_mix_post, norm_ffn_pre, norm_ffn_post, mla_w_dq, mla_q_norm,
           mla_w_uq, mla_w_dkv, mla_kv_norm, mla_w_ukv, mla_w_o, shared_kv_norm, swa_w_k, swa_w_v, swa_w_q,
           swa_sinks, swa_w_o, ffn_w_in, ffn_conv_w, ffn_conv_b, ffn_w_out):
    batch, seq, d = x.shape
    m = batch * seq
    depth = norm_mix_pre.shape[0]
    n_a = mla_w_dq.shape[0]
    heads = mla_w_o.shape[1] // MLA_V
    swa_heads = swa_w_q.shape[2] // SWA_HEAD_DIM
    hpg = 4

    half = MLA_ROPE // 2
    freqs = ROPE_THETA ** (-jnp.arange(half, dtype=F32) / half)
    cos2, sin2 = _rope_tables(positions.reshape(m, 1), jnp.tile(freqs, 4)[None, :])

    row = lambda v: v.reshape(1, -1)
    h = x.reshape(m, d)
    assert depth == 2 and n_a == 1, "wired for one MLA layer followed by one sliding-window layer"

    perm = _interleave_permutation()

    def ffn(l, h, hn_ffn, w_in, g_next):
        act, w_out = _ffn_up(hn_ffn, w_in, ffn_conv_w[l], row(ffn_conv_b[l]), perm.T, ffn_w_out, l, seq)
        f = _matmul(act, w_out, F32, 512, 512, name="ffn_down")
        return _residual_norm(f, h, row(norm_ffn_post[l]), g_next)

    w_uq = mla_w_uq[0].astype(BF16).reshape(-1, heads // hpg, hpg, MLA_QK)
    w_uq3 = jnp.concatenate([w_uq[..., :MLA_NOPE].reshape(-1, heads // hpg, hpg * MLA_NOPE),
                             w_uq[..., MLA_NOPE:].reshape(-1, heads // hpg, hpg * MLA_ROPE)],
                            axis=2).transpose(1, 0, 2)
    w_ukv = mla_w_ukv[0].astype(BF16)
    w_ukv3 = w_ukv.reshape(w_ukv.shape[0], heads // hpg, hpg * (MLA_NOPE + MLA_V)).transpose(1, 0, 2)
    lora = mla_kv_norm.shape[1]
    w_dkv = mla_w_dkv[0].astype(BF16)
    w_dkv2 = jnp.concatenate([w_dkv, w_dkv[:, lora:]], axis=1)
    cq, c, kr = _mla_down(h, row(norm_mix_pre[0]), mla_w_dq[0].astype(BF16), row(mla_q_norm[0]),
                          w_dkv2, row(mla_kv_norm[0]), cos2, sin2)
    q = _mla_q_up(cq, w_uq3, cos2, sin2, batch, seq, MLA_QK ** -0.5 * LOG2_E)
    k, vt = _mla_kv_up(c, w_ukv3, kr, batch, seq)
    o, w_in0, mla_wo, swa_wq, swa_wo = _mla_attention(q, k, vt, 4, [ffn_w_in, mla_w_o, swa_w_q, swa_w_o], 0)
    a = _matmul(o.reshape(m, heads * MLA_V), mla_wo, F32, 1024, 512, name="mla_out")
    h, hn_ffn = _residual_norm(a, h, row(norm_mix_post[0]), [row(norm_ffn_pre[0])], perm=perm)
    h, hn_mix, hn_kv = ffn(0, h, hn_ffn, w_in0, [row(norm_mix_pre[1]), row(shared_kv_norm)])

    w_kv = jnp.concatenate([swa_w_k, swa_w_v], axis=1).astype(BF16)
    k_shared, vt_shared = _swa_kv(hn_kv, w_kv)
    groups = k_shared.shape[0]
    sink_lanes = jnp.repeat(swa_sinks[0].reshape(groups, 1, swa_heads // groups) * LOG2_E, WINDOW, axis=2)
    q = _matmul(hn_mix, swa_wq, BF16, 1024, 512, scale=SWA_HEAD_DIM ** -0.5 * LOG2_E, name="swa_q")
    o, w_in1 = _swa_attention(q, k_shared, vt_shared, _swa_bias(swa_heads, groups), sink_lanes,
                              batch, seq, [ffn_w_in], 1)
    a = _matmul(o, swa_wo, F32, 1024, 512, name="swa_out")
    h, hn_ffn = _residual_norm(a, h, row(norm_mix_post[1]), [row(norm_ffn_pre[1])], perm=perm)
    (h,) = ffn(1, h, hn_ffn, w_in1, [])
    return h.reshape(batch, seq, d)
```

```python
import functools

import jax
import jax.numpy as jnp
from jax import lax
from jax.experimental import pallas as pl
from jax.experimental.pallas import tpu as pltpu

EPS = 1e-6
LOG2_E = 1.4426950408889634
ROPE_THETA = 10000.0
MLA_NOPE = 128
MLA_ROPE = 64
MLA_V = 128
MLA_QK = MLA_NOPE + MLA_ROPE
MLA_QK_PAD = MLA_NOPE + 2 * MLA_ROPE
SWA_HEAD_DIM = 64
SWA_KV_HEADS = 8
WINDOW = 128
CONV_W = 3

V7X_VMEM_BYTES = 64 * 1024 * 1024
VMEM_LIMIT_CAP = V7X_VMEM_BYTES - 8 * 1024 * 1024
SUBLANES = 8
PERM_UNIT = 256
PERM_GROUPS = PERM_UNIT // SUBLANES

BF16 = jnp.bfloat16
F32 = jnp.float32
BRANCH_DTYPE = BF16


def _nbytes(shape, dtype):
    n = 1
    for s in shape:
        n *= s
    return n * jnp.dtype(dtype).itemsize


def _params(semantics, blocks, extra=0):
    est = 2 * sum(_nbytes(s, d) for s, d in blocks) + extra
    limit = min(VMEM_LIMIT_CAP, max(est + est // 4, 16 * 1024 * 1024))
    return pltpu.CompilerParams(dimension_semantics=semantics, vmem_limit_bytes=limit)


def _rms(x, g):
    r = lax.rsqrt(jnp.mean(x * x, axis=-1, keepdims=True) + EPS)
    return (x * r) * g


def _rope_table_kernel(pos_ref, freq_ref, cos_ref, sin_ref):
    ang = pos_ref[...].astype(F32) * freq_ref[...]
    lane = lax.broadcasted_iota(jnp.int32, ang.shape, 1)
    s = jnp.sin(ang)
    cos_ref[...] = jnp.cos(ang)
    sin_ref[...] = jnp.where(lane % MLA_ROPE < MLA_ROPE // 2, -s, s)


def _rope_pair(r, cos2, sin2):
    lane = lax.broadcasted_iota(jnp.int32, r.shape, 1)
    half = MLA_ROPE // 2
    partner = jnp.where(lane % MLA_ROPE < half, pltpu.roll(r, r.shape[1] - half, 1), pltpu.roll(r, half, 1))
    return r * cos2 + partner * sin2


def _rope_tables(pos_col, freqs2):
    m = pos_col.shape[0]
    d = freqs2.shape[1]
    tm = 1024
    return pl.pallas_call(
        _rope_table_kernel,
        out_shape=(jax.ShapeDtypeStruct((m, d), F32), jax.ShapeDtypeStruct((m, d), F32)),
        grid=(m // tm,),
        in_specs=[pl.BlockSpec((tm, 1), lambda i: (i, 0)), pl.BlockSpec((1, d), lambda i: (0, 0))],
        out_specs=(pl.BlockSpec((tm, d), lambda i: (i, 0)), pl.BlockSpec((tm, d), lambda i: (i, 0))),
        name="rope_tables",
    )(pos_col, freqs2)


def _mla_down_kernel(x_ref, g_ref, wdq_ref, qn_ref, wdkv_ref, kvn_ref, cos_ref, sin_ref,
                     cq_ref, c_ref, ct_ref, kr_ref):
    hn = _rms(x_ref[...], g_ref[...]).astype(BF16)
    cq = jnp.dot(hn, wdq_ref[...], preferred_element_type=F32)
    cq_ref[...] = _rms(cq, qn_ref[...]).astype(BF16)
    ckv = jnp.dot(hn, wdkv_ref[...], preferred_element_type=F32)
    lora = c_ref.shape[-1]
    c = _rms(ckv[:, :lora], kvn_ref[...])
    c_ref[...] = c.astype(BF16)
    ct_ref[...] = c.T.astype(BF16)
    kr_ref[...] = _rope_pair(ckv[:, lora:], cos_ref[...], sin_ref[...]).astype(BF16)


def _mla_down(x2, g, wdq, qn, wdkv, kvn, cosf, sinf):
    m, d = x2.shape
    nq = wdq.shape[1]
    nkv = wdkv.shape[1]
    lora = kvn.shape[1]
    rd = nkv - lora
    tm = 256
    row = lambda i: (i, 0)
    fixed = lambda i: (0, 0)
    blocks = [((tm, d), F32), ((d, nq), BF16), ((d, nkv), BF16), ((tm, nq), BF16), ((tm, lora), BF16)]
    return pl.pallas_call(
        _mla_down_kernel,
        out_shape=(jax.ShapeDtypeStruct((m, nq), BF16), jax.ShapeDtypeStruct((m, lora), BF16),
                   jax.ShapeDtypeStruct((lora, m), BF16), jax.ShapeDtypeStruct((m, rd), BF16)),
        grid=(m // tm,),
        in_specs=[pl.BlockSpec((tm, d), row), pl.BlockSpec((1, d), fixed),
                  pl.BlockSpec((d, nq), fixed), pl.BlockSpec((1, nq), fixed),
                  pl.BlockSpec((d, nkv), fixed), pl.BlockSpec((1, lora), fixed),
                  pl.BlockSpec((tm, rd), row), pl.BlockSpec((tm, rd), row)],
        out_specs=(pl.BlockSpec((tm, nq), row), pl.BlockSpec((tm, lora), row),
                   pl.BlockSpec((lora, tm), lambda i: (0, i)), pl.BlockSpec((tm, rd), row)),
        compiler_params=_params(("arbitrary",), blocks, extra=4 * _nbytes((tm, d), F32)),
        name="mla_down",
    )(x2, g, wdq, qn, wdkv, kvn, cosf, sinf)


def _mla_q_up_kernel(cq_ref, w_ref, cos_ref, sin_ref, q_ref, *, scale):
    res = jnp.dot(cq_ref[...], w_ref[0], preferred_element_type=F32)
    hpg = q_ref.shape[1]
    cos2 = cos_ref[...]
    sin2 = sin_ref[...]
    for hh in range(hpg):
        q_ref[0, hh, :, :MLA_NOPE] = (res[:, hh * MLA_NOPE:(hh + 1) * MLA_NOPE] * scale).astype(BF16)
    low = lax.broadcasted_iota(jnp.int32, cos2.shape, 1) < MLA_ROPE
    for p in range(hpg // 2):
        base = hpg * MLA_NOPE + p * 2 * MLA_ROPE
        r = _rope_pair(res[:, base:base + 2 * MLA_ROPE], cos2, sin2) * scale
        q_ref[0, 2 * p, :, MLA_NOPE:] = jnp.where(low, r, 0.0).astype(BF16)
        q_ref[0, 2 * p + 1, :, MLA_NOPE:] = jnp.where(low, 0.0, r).astype(BF16)


def _mla_q_up(cq, w3, cos2, sin2, batch, seq, scale):
    m, k = cq.shape
    groups, _, gn = w3.shape
    hpg = gn // MLA_QK
    heads = groups * hpg
    tm = 512
    nsb = seq // tm
    blocks = [((tm, k), BF16), ((k, gn), BF16), ((hpg, tm, MLA_QK_PAD), BF16)]
    return pl.pallas_call(
        functools.partial(_mla_q_up_kernel, scale=scale),
        out_shape=jax.ShapeDtypeStruct((batch, heads, seq, MLA_QK_PAD), BF16),
        grid=(m // tm, groups),
        in_specs=[pl.BlockSpec((tm, k), lambda i, g: (i, 0)),
                  pl.BlockSpec((1, k, gn), lambda i, g: (g, 0, 0)),
                  pl.BlockSpec((tm, 2 * MLA_ROPE), lambda i, g: (i, 0)),
                  pl.BlockSpec((tm, 2 * MLA_ROPE), lambda i, g: (i, 0))],
        out_specs=pl.BlockSpec((1, hpg, tm, MLA_QK_PAD), lambda i, g: (i // nsb, g, i % nsb, 0)),
        compiler_params=_params(("arbitrary", "arbitrary"), blocks, extra=2 * _nbytes((tm, gn), F32)),
        name="mla_q_up",
    )(cq, w3, cos2, sin2)


def _mla_kv_up_kernel(c_ref, ct_ref, wk_ref, wvt_ref, kr_ref, k_ref, vt_ref):
    kn = jnp.dot(c_ref[...], wk_ref[0], preferred_element_type=F32)
    vt = jnp.dot(wvt_ref[0], ct_ref[...], preferred_element_type=F32)
    kr = kr_ref[...]
    for hh in range(k_ref.shape[1]):
        k_ref[0, hh, :, :MLA_NOPE] = kn[:, hh * MLA_NOPE:(hh + 1) * MLA_NOPE].astype(BF16)
        k_ref[0, hh, :, MLA_NOPE:] = kr
        vt_ref[0, hh] = vt[hh * MLA_V:(hh + 1) * MLA_V, :].astype(BF16)


def _mla_kv_up(c, ct, wk3, wvt3, kr, batch, seq):
    m, k = c.shape
    groups, _, gk = wk3.shape
    hpg = gk // MLA_NOPE
    heads = groups * hpg
    tm = 512
    nsb = seq // tm
    blocks = [((tm, k), BF16), ((k, tm), BF16), ((k, gk), BF16), ((hpg * MLA_V, k), BF16),
              ((hpg, tm, MLA_QK_PAD), BF16), ((hpg, tm, MLA_V), BF16)]
    omap = lambda i, g: (i // nsb, g, i % nsb, 0)
    vmap = lambda i, g: (i // nsb, g, 0, i % nsb)
    return pl.pallas_call(
        _mla_kv_up_kernel,
        out_shape=(jax.ShapeDtypeStruct((batch, heads, seq, MLA_QK_PAD), BF16),
                   jax.ShapeDtypeStruct((batch, heads, MLA_V, seq), BF16)),
        grid=(m // tm, groups),
        in_specs=[pl.BlockSpec((tm, k), lambda i, g: (i, 0)),
                  pl.BlockSpec((k, tm), lambda i, g: (0, i)),
                  pl.BlockSpec((1, k, gk), lambda i, g: (g, 0, 0)),
                  pl.BlockSpec((1, hpg * MLA_V, k), lambda i, g: (g, 0, 0)),
                  pl.BlockSpec((tm, 2 * MLA_ROPE), lambda i, g: (i, 0))],
        out_specs=(pl.BlockSpec((1, hpg, tm, MLA_QK_PAD), omap), pl.BlockSpec((1, hpg, MLA_V, tm), vmap)),
        compiler_params=_params(("arbitrary", "arbitrary"), blocks, extra=4 * _nbytes((tm, gk), F32)),
        name="mla_kv_up",
    )(c, ct, wk3, wvt3, kr)


def _cast_riders(weights, layer, nsteps, step_of):
    in_specs, out_shapes, out_specs, blocks = [], [], [], []
    for w in weights:
        _, rows, cols = w.shape
        slab = rows // nsteps
        assert slab * nsteps == rows and slab % (2 * SUBLANES) == 0, (w.shape, nsteps)
        in_specs.append(pl.BlockSpec((None, slab, cols), lambda *g: (layer, step_of(*g), 0)))
        out_shapes.append(jax.ShapeDtypeStruct((rows, cols), BF16))
        out_specs.append(pl.BlockSpec((slab, cols), lambda *g: (step_of(*g), 0)))
        blocks += [((slab, cols), F32), ((slab, cols), BF16)]
    return in_specs, out_shapes, out_specs, blocks


def _run_cast_riders(src_refs, dst_refs):
    for src, dst in zip(src_refs, dst_refs):
        dst[...] = src[...].astype(dst.dtype)


def _mla_attn_kernel(*refs, tk, n_riders):
    q_ref, k_ref, vt_ref = refs[:3]
    o_ref = refs[3 + n_riders]
    sa_ref, sb_ref, m_ref, l_ref, acc_ref = refs[4 + 2 * n_riders:]
    _run_cast_riders(refs[3:3 + n_riders], refs[4 + n_riders:4 + 2 * n_riders])
    i = pl.program_id(2)
    hp = q_ref.shape[1]
    dv = acc_ref.shape[1]
    m_ref[...] = jnp.full(m_ref.shape, -jnp.inf, F32)
    l_ref[...] = jnp.zeros(l_ref.shape, F32)
    acc_ref[...] = jnp.zeros(acc_ref.shape, F32)

    def scores(j, s_ref):
        start = pl.multiple_of(j * tk, tk)
        for hh in range(hp):
            k = k_ref[0, hh, pl.ds(start, tk), :]
            s_ref[hh] = lax.dot_general(k, q_ref[0, hh], (((1,), (1,)), ((), ())),
                                        preferred_element_type=F32)

    def update(j, s_ref, masked):
        start = pl.multiple_of(j * tk, tk)
        for hh in range(hp):
            st = s_ref[hh]
            if masked:
                kj = lax.broadcasted_iota(jnp.int32, st.shape, 0)
                qi = lax.broadcasted_iota(jnp.int32, st.shape, 1)
                st = jnp.where(kj <= qi, st, -jnp.inf)
            vt = vt_ref[0, hh, :, pl.ds(start, tk)]
            m_prev = m_ref[hh]
            m_new = jnp.maximum(m_prev, jnp.max(st, axis=0, keepdims=True))
            alpha = jnp.exp2(m_prev - m_new)
            pt = jnp.exp2(st - m_new)
            l_ref[hh] = alpha * l_ref[hh] + jnp.sum(pt, axis=0, keepdims=True)
            acc_ref[hh] = alpha * acc_ref[hh] + jnp.dot(vt, pt.astype(BF16), preferred_element_type=F32)
            m_ref[hh] = m_new

    def finish():
        for hh in range(hp):
            o = acc_ref[hh] / l_ref[hh]
            o_ref[0, :, hh * dv:(hh + 1) * dv] = o.T.astype(o_ref.dtype)

    scores(0, sa_ref)

    def pair(jj, carry):
        j = 2 * jj
        scores(j + 1, sb_ref)
        update(j, sa_ref, False)
        scores(j + 2, sa_ref)
        update(j + 1, sb_ref, False)
        return carry

    lax.fori_loop(0, i // 2, pair, 0)

    @pl.when(i % 2 == 0)
    def _():
        update(i, sa_ref, True)
        finish()

    @pl.when(i % 2 == 1)
    def _():
        scores(i, sb_ref)
        update(i - 1, sa_ref, False)
        update(i, sb_ref, True)
        finish()


def _mla_attention(q, k, vt, hp, ride_weights, ride_layer):
    batch, heads, seq, dqk = q.shape
    dv = vt.shape[2]
    tq = 512
    nh, nq = heads // hp, seq // tq
    r_in, r_shapes, r_out, r_blocks = _cast_riders(ride_weights, ride_layer, batch * nh * nq,
                                                   lambda b, h, i: (b * nh + h) * nq + i)
    blocks = [((hp, tq, 256), BF16), ((hp, seq, 256), BF16), ((hp, dv, seq), BF16), ((tq, hp * dv), BF16)]
    return pl.pallas_call(
        functools.partial(_mla_attn_kernel, tk=tq, n_riders=len(ride_weights)),
        out_shape=(jax.ShapeDtypeStruct((batch, seq, heads * dv), BF16), *r_shapes),
        grid=(batch, nh, nq),
        in_specs=[pl.BlockSpec((1, hp, tq, dqk), lambda b, h, i: (b, h, i, 0)),
                  pl.BlockSpec((1, hp, seq, dqk), lambda b, h, i: (b, h, 0, 0)),
                  pl.BlockSpec((1, hp, dv, seq), lambda b, h, i: (b, h, 0, 0)), *r_in],
        out_specs=(pl.BlockSpec((1, tq, hp * dv), lambda b, h, i: (b, i, h)), *r_out),
        scratch_shapes=[pltpu.VMEM((hp, tq, tq), F32), pltpu.VMEM((hp, tq, tq), F32),
                        pltpu.VMEM((hp, 1, tq), F32), pltpu.VMEM((hp, 1, tq), F32),
                        pltpu.VMEM((hp, dv, tq), F32)],
        compiler_params=_params(("arbitrary", "arbitrary", "arbitrary"), blocks + r_blocks,
                                extra=8 * hp * _nbytes((tq, tq), F32)),
        name="mla_attention",
    )(q, k, vt, *ride_weights)


def _matmul_kernel(x_ref, w_ref, o_ref, *, scale):
    acc = jnp.dot(x_ref[...], w_ref[...], preferred_element_type=F32)
    if scale is not None:
        acc = acc * scale
    o_ref[...] = acc.astype(o_ref.dtype)


def _matmul(x, w, out_dtype, tm, tn, scale=None, name="matmul"):
    m, k = x.shape
    n = w.shape[1]
    blocks = [((tm, k), x.dtype), ((k, tn), w.dtype), ((tm, tn), out_dtype)]
    return pl.pallas_call(
        functools.partial(_matmul_kernel, scale=scale),
        out_shape=jax.ShapeDtypeStruct((m, n), out_dtype),
        grid=(m // tm, n // tn),
        in_specs=[pl.BlockSpec((tm, k), lambda i, j: (i, 0)), pl.BlockSpec((k, tn), lambda i, j: (0, j))],
        out_specs=pl.BlockSpec((tm, tn), lambda i, j: (i, j)),
        compiler_params=_params(("arbitrary", "arbitrary"), blocks, extra=2 * _nbytes((tm, tn), F32)),
        name=name,
    )(x, w)


def _residual_norm_kernel(*refs, n_next, interleave):
    a_ref, h_ref, gp_ref = refs[:3]
    gn_refs = refs[3:3 + n_next]
    perm_ref = refs[3 + n_next] if interleave else None
    out_refs = refs[3 + n_next + (1 if interleave else 0):]
    h_new = h_ref[...] + _rms(a_ref[...].astype(F32), gp_ref[...])
    out_refs[0][...] = h_new
    for gn_ref, o_ref in zip(gn_refs, out_refs[1:]):
        hn = _rms(h_new, gn_ref[...]).astype(o_ref.dtype)
        if interleave:
            hn = jnp.dot(perm_ref[...], hn, preferred_element_type=F32).astype(o_ref.dtype)
        o_ref[...] = hn


def _residual_norm(a, h, g_post, g_next, perm=None):
    m, d = a.shape
    n_next = len(g_next)
    tm = PERM_UNIT
    row = lambda i: (i, 0)
    fixed = lambda i: (0, 0)
    interleave = perm is not None
    out_shape = [jax.ShapeDtypeStruct((m, d), F32)] + [jax.ShapeDtypeStruct((m, d), BF16)] * n_next
    out_specs = [pl.BlockSpec((tm, d), row)] * (1 + n_next)
    blocks = [((tm, d), F32)] * (4 + n_next)
    perm_specs = [pl.BlockSpec((tm, tm), fixed)] if interleave else []
    perm_args = [perm] if interleave else []
    return pl.pallas_call(
        functools.partial(_residual_norm_kernel, n_next=n_next, interleave=interleave),
        out_shape=tuple(out_shape),
        grid=(m // tm,),
        in_specs=[pl.BlockSpec((tm, d), row), pl.BlockSpec((tm, d), row)]
        + [pl.BlockSpec((1, d), fixed)] * (1 + n_next) + perm_specs,
        out_specs=tuple(out_specs),
        compiler_params=_params(("arbitrary",), blocks),
        name="residual_norm",
    )(a, h, g_post, *g_next, *perm_args)


def _interleave_permutation():
    r = jnp.arange(PERM_UNIT)
    t = (r % SUBLANES) * PERM_GROUPS + r // SUBLANES
    return (t[:, None] == jnp.arange(PERM_UNIT)[None, :]).astype(BF16)


def _ffn_up_kernel(x_ref, wg_ref, wv_ref, cwg_ref, cwv_ref, cbg_ref, cbv_ref, pt_ref, wdn_ref, o_ref,
                   wdn_bf_ref, carry_ref, *, blocks_per_seq):
    _run_cast_riders([wdn_ref], [wdn_bf_ref])
    i = pl.program_id(0)
    j = pl.program_id(1)
    tn = wg_ref.shape[1]
    tm = x_ref.shape[0]
    grp = SUBLANES

    @pl.when((i % blocks_per_seq) == 0)
    def _():
        carry_ref[j] = jnp.zeros(carry_ref.shape[1:], F32)

    w = jnp.concatenate([wg_ref[...], wv_ref[...]], axis=1)
    cw = jnp.concatenate([cwg_ref[...], cwv_ref[...]], axis=1)
    cb = jnp.concatenate([cbg_ref[...], cbv_ref[...]], axis=1)
    first_sublane = lax.broadcasted_iota(jnp.int32, (grp, 2 * tn), 0) == 0
    tail = carry_ref[j]

    def project(c):
        return jnp.dot(x_ref[c * PERM_UNIT:(c + 1) * PERM_UNIT, :], w, preferred_element_type=F32)

    nunits = tm // PERM_UNIT
    u_next = project(0)
    for c in range(nunits):
        rows = slice(c * PERM_UNIT, (c + 1) * PERM_UNIT)
        u = u_next
        if c + 1 < nunits:
            u_next = project(c + 1)
        new_tail = u[PERM_UNIT - 2 * grp:, :]
        wrapped = jnp.where(jnp.concatenate([first_sublane, first_sublane], axis=0),
                            jnp.concatenate([pltpu.roll(tail[:grp], 1, 0), pltpu.roll(tail[grp:], 1, 0)], axis=0),
                            jnp.concatenate([pltpu.roll(new_tail[:grp], 1, 0), pltpu.roll(new_tail[grp:], 1, 0)],
                                            axis=0))
        ext = jnp.concatenate([wrapped, u], axis=0)
        acc = cw[0:1, :] * ext[0:PERM_UNIT] + cw[1:2, :] * ext[grp:grp + PERM_UNIT] + cw[2:3, :] * u
        conv = cb + acc
        gate, val = conv[:, :tn], conv[:, tn:]
        act = (gate * (1.0 / (1.0 + jnp.exp(-gate))) * val).astype(o_ref.dtype)
        o_ref[rows, :] = jnp.dot(pt_ref[...], act, preferred_element_type=F32).astype(o_ref.dtype)
        tail = new_tail
    carry_ref[j] = tail


def _ffn_up(hn, w_in, conv_w, conv_b, perm_t, w_out_all, layer, seq):
    m, d = hn.shape
    d_ff = w_in.shape[1] // 2
    tn = 256
    nj = d_ff // tn
    tm = 2048
    r_in, r_shapes, r_out, r_blocks = _cast_riders([w_out_all], layer, (m // tm) * nj, lambda i, j: i * nj + j)
    blocks = [((tm, d), BF16), ((d, tn), BF16), ((d, tn), BF16), ((tm, tn), BF16)] + r_blocks
    scratch = _nbytes((nj, 2 * SUBLANES, 2 * tn), F32)
    gate_col = lambda i, j: (0, j)
    val_col = lambda i, j: (0, j + nj)
    return pl.pallas_call(
        functools.partial(_ffn_up_kernel, blocks_per_seq=seq // tm),
        out_shape=(jax.ShapeDtypeStruct((m, d_ff), BF16), *r_shapes),
        grid=(m // tm, nj),
        in_specs=[pl.BlockSpec((tm, d), lambda i, j: (i, 0)),
                  pl.BlockSpec((d, tn), gate_col), pl.BlockSpec((d, tn), val_col),
                  pl.BlockSpec((CONV_W, tn), gate_col), pl.BlockSpec((CONV_W, tn), val_col),
                  pl.BlockSpec((1, tn), gate_col), pl.BlockSpec((1, tn), val_col),
                  pl.BlockSpec((PERM_UNIT, PERM_UNIT), lambda i, j: (0, 0)), *r_in],
        out_specs=(pl.BlockSpec((tm, tn), lambda i, j: (i, j)), *r_out),
        scratch_shapes=[pltpu.VMEM((nj, 2 * SUBLANES, 2 * tn), F32)],
        compiler_params=_params(("arbitrary", "arbitrary"), blocks,
                                extra=scratch + 10 * _nbytes((PERM_UNIT, 2 * tn), F32)),
        name="ffn_up",
    )(hn, w_in, w_in, conv_w, conv_w, conv_b, conv_b, perm_t, w_out_all)


def _swa_kv_kernel(x_ref, w_ref, k_ref, vt_ref):
    res = jnp.dot(x_ref[...], w_ref[...], preferred_element_type=F32)
    groups = k_ref.shape[0]
    hd = SWA_HEAD_DIM
    for t in range(groups):
        k_ref[t] = res[:, t * hd:(t + 1) * hd].astype(k_ref.dtype)
    vt = res[:, groups * hd:].T
    for t in range(groups):
        vt_ref[t] = vt[t * hd:(t + 1) * hd, :].astype(vt_ref.dtype)


def _swa_kv(x, w):
    m, k = x.shape
    n = w.shape[1]
    groups = n // (2 * SWA_HEAD_DIM)
    tm = 512
    blocks = [((tm, k), BF16), ((k, n), BF16), ((groups, tm, 128), BF16), ((groups, SWA_HEAD_DIM, tm), BF16)]
    return pl.pallas_call(
        _swa_kv_kernel,
        out_shape=(jax.ShapeDtypeStruct((groups, m, SWA_HEAD_DIM), BF16),
                   jax.ShapeDtypeStruct((groups, SWA_HEAD_DIM, m), BF16)),
        grid=(m // tm,),
        in_specs=[pl.BlockSpec((tm, k), lambda i: (i, 0)), pl.BlockSpec((k, n), lambda i: (0, 0))],
        out_specs=(pl.BlockSpec((groups, tm, SWA_HEAD_DIM), lambda i: (0, i, 0)),
                   pl.BlockSpec((groups, SWA_HEAD_DIM, tm), lambda i: (0, 0, i))),
        compiler_params=_params(("arbitrary",), blocks, extra=3 * _nbytes((tm, n), F32)),
        name="swa_kv",
    )(x, w)


def _swa_attn_kernel(*refs, rep, n_riders):
    bias_ref, sink_ref, q_ref, kp_ref, kc_ref, vtp_ref, vtc_ref = refs[:7]
    o_ref = refs[7 + n_riders]
    _run_cast_riders(refs[7:7 + n_riders], refs[8 + n_riders:8 + 2 * n_riders])
    n = pl.program_id(2)
    w = WINDOW
    hd = SWA_HEAD_DIM
    nq = q_ref.shape[0] // w
    kfull = jnp.concatenate([kp_ref[0], kc_ref[0]], axis=0)
    vtfull = jnp.concatenate([vtp_ref[0], vtc_ref[0]], axis=1)
    sink = sink_ref[0]
    first = jnp.minimum(n, 1)
    for c in range(nq):
        qt = q_ref[c * w:(c + 1) * w, :].astype(F32).T
        qt = jnp.concatenate([qt[r * hd:(r + 1) * hd, :] for r in range(rep)], axis=1).astype(BF16)
        bias = bias_ref[0, first] if c == 0 else bias_ref[0, 1]
        st = jnp.dot(kfull[c * w:(c + 2) * w], qt, preferred_element_type=F32) + bias
        mx = jnp.maximum(jnp.max(st, axis=0, keepdims=True), sink)
        e = jnp.exp2(st - mx)
        denom = jnp.sum(e, axis=0, keepdims=True) + jnp.exp2(sink - mx)
        ot = jnp.dot(vtfull[:, c * w:(c + 2) * w], e.astype(BF16), preferred_element_type=F32)
        ot = ot * (1.0 / denom)
        o = jnp.concatenate([ot[:, r * w:(r + 1) * w] for r in range(rep)], axis=0)
        o_ref[c * w:(c + 1) * w, :] = o.T.astype(o_ref.dtype)


def _swa_attention(q, k, vt, bias, sink_lanes, batch, seq, ride_weights, ride_layer):
    m, dq = q.shape
    groups = k.shape[0]
    rep = dq // (groups * SWA_HEAD_DIM)
    w = WINDOW
    nq = 4
    tq = nq * w
    nb = seq // tq
    gw = rep * SWA_HEAD_DIM
    prev_blk = lambda b, n: jnp.maximum((b * nb + n) * nq - 1, b * nb * nq)
    r_in, r_shapes, r_out, r_blocks = _cast_riders(ride_weights, ride_layer, groups * batch * nb,
                                                   lambda g, b, n: (g * batch + b) * nb + n)
    blocks = [((2, 2 * w, rep * w), F32), ((tq, gw), BF16), ((tq, gw), BF16)] + r_blocks
    return pl.pallas_call(
        functools.partial(_swa_attn_kernel, rep=rep, n_riders=len(ride_weights)),
        out_shape=(jax.ShapeDtypeStruct((m, dq), BF16), *r_shapes),
        grid=(groups, batch, nb),
        in_specs=[pl.BlockSpec((1, 2, 2 * w, rep * w), lambda g, b, n: (g, 0, 0, 0)),
                  pl.BlockSpec((1, 1, rep * w), lambda g, b, n: (g, 0, 0)),
                  pl.BlockSpec((tq, gw), lambda g, b, n: (b * nb + n, g)),
                  pl.BlockSpec((1, w, SWA_HEAD_DIM), lambda g, b, n: (g, prev_blk(b, n), 0)),
                  pl.BlockSpec((1, tq, SWA_HEAD_DIM), lambda g, b, n: (g, b * nb + n, 0)),
                  pl.BlockSpec((1, SWA_HEAD_DIM, w), lambda g, b, n: (g, 0, prev_blk(b, n))),
                  pl.BlockSpec((1, SWA_HEAD_DIM, tq), lambda g, b, n: (g, 0, b * nb + n)), *r_in],
        out_specs=(pl.BlockSpec((tq, gw), lambda g, b, n: (b * nb + n, g)), *r_out),
        compiler_params=_params(("arbitrary", "arbitrary", "arbitrary"), blocks,
                                extra=6 * nq * _nbytes((2 * w, rep * w), F32)),
        name="swa_attention",
    )(bias, sink_lanes, q, k, k, vt, vt, *ride_weights)


def _swa_bias(swa_heads, groups):
    w = WINDOW
    rep = swa_heads // groups
    slopes = 2.0 ** (-8.0 * jnp.arange(1, swa_heads + 1, dtype=F32) / swa_heads)
    kj = jnp.arange(2 * w)[:, None]
    qi = jnp.arange(w)[None, :]
    dist = qi + w - kj
    in_window = (dist >= 0) & (dist < w)
    valid = jnp.stack([in_window & (kj >= w), in_window])
    alibi = slopes.reshape(groups, 1, 1, rep, 1) * dist.astype(F32)[None, None, :, None, :]
    bias = jnp.where(valid[None, :, :, None, :], -alibi * LOG2_E, -jnp.inf)
    return bias.reshape(groups, 2, 2 * w, rep * w)


def kernel(x, positions, norm_mix_pre, norm_mix_post, norm_ffn_pre, norm_ffn_post, mla_w_dq, mla_q_norm,
           mla_w_uq, mla_w_dkv, mla_kv_norm, mla_w_ukv, mla_w_o, shared_kv_norm, swa_w_k, swa_w_v, swa_w_q,
           swa_sinks, swa_w_o, ffn_w_in, ffn_conv_w, ffn_conv_b, ffn_w_out):
    batch, seq, d = x.shape
    m = batch * seq
    depth = norm_mix_pre.shape[0]
    n_a = mla_w_dq.shape[0]
    heads = mla_w_o.shape[1] // MLA_V
    swa_heads = swa_w_q.shape[2] // SWA_HEAD_DIM
    hpg = 4

    half = MLA_ROPE // 2
    freqs = ROPE_THETA ** (-jnp.arange(half, dtype=F32) / half)
    cos2, sin2 = _rope_tables(positions.reshape(m, 1), jnp.tile(freqs, 4)[None, :])

    row = lambda v: v.reshape(1, -1)
    h = x.reshape(m, d)
    assert depth == 2 and n_a == 1, "wired for one MLA layer followed by one sliding-window layer"

    perm = _interleave_permutation()

    def ffn(l, h, hn_ffn, w_in, g_next):
        act, w_out = _ffn_up(hn_ffn, w_in, ffn_conv_w[l], row(ffn_conv_b[l]), perm.T, ffn_w_out, l, seq)
        f = _matmul(act, w_out, BRANCH_DTYPE, 512, 512, name="ffn_down")
        return _residual_norm(f, h, row(norm_ffn_post[l]), g_next)

    w_uq = mla_w_uq[0].astype(BF16).reshape(-1, heads // hpg, hpg, MLA_QK)
    w_uq3 = jnp.concatenate([w_uq[..., :MLA_NOPE].reshape(-1, heads // hpg, hpg * MLA_NOPE),
                             w_uq[..., MLA_NOPE:].reshape(-1, heads // hpg, hpg * MLA_ROPE)],
                            axis=2).transpose(1, 0, 2)
    w_ukv = mla_w_ukv[0].astype(BF16).reshape(-1, heads // hpg, hpg, MLA_NOPE + MLA_V)
    w_uk3 = w_ukv[..., :MLA_NOPE].reshape(-1, heads // hpg, hpg * MLA_NOPE).transpose(1, 0, 2)
    w_uvt3 = w_ukv[..., MLA_NOPE:].reshape(-1, heads // hpg, hpg * MLA_V).transpose(1, 2, 0)
    lora = mla_kv_norm.shape[1]
    w_dkv = mla_w_dkv[0].astype(BF16)
    w_dkv2 = jnp.concatenate([w_dkv, w_dkv[:, lora:]], axis=1)
    cq, c, ct, kr = _mla_down(h, row(norm_mix_pre[0]), mla_w_dq[0].astype(BF16), row(mla_q_norm[0]),
                              w_dkv2, row(mla_kv_norm[0]), cos2, sin2)
    q = _mla_q_up(cq, w_uq3, cos2, sin2, batch, seq, MLA_QK ** -0.5 * LOG2_E)
    k, vt = _mla_kv_up(c, ct, w_uk3, w_uvt3, kr, batch, seq)
    o, w_in0, mla_wo, swa_wq, swa_wo = _mla_attention(q, k, vt, 4, [ffn_w_in, mla_w_o, swa_w_q, swa_w_o], 0)
    a = _matmul(o.reshape(m, heads * MLA_V), mla_wo, BRANCH_DTYPE, 1024, 512, name="mla_out")
    h, hn_ffn = _residual_norm(a, h, row(norm_mix_post[0]), [row(norm_ffn_pre[0])], perm=perm)
    h, hn_mix, hn_kv = ffn(0, h, hn_ffn, w_in0, [row(norm_mix_pre[1]), row(shared_kv_norm)])

    w_kv = jnp.concatenate([swa_w_k, swa_w_v], axis=1).astype(BF16)
    k_shared, vt_shared = _swa_kv(hn_kv, w_kv)
    groups = k_shared.shape[0]
    sink_lanes = jnp.repeat(swa_sinks[0].reshape(groups, 1, swa_heads // groups) * LOG2_E, WINDOW, axis=2)
    q = _matmul(hn_mix, swa_wq, BF16, 1024, 512, scale=SWA_HEAD_DIM ** -0.5 * LOG2_E, name="swa_q")
    o, w_in1 = _swa_attention(q, k_shared, vt_shared, _swa_bias(swa_heads, groups), sink_lanes,
                              batch, seq, [ffn_w_in], 1)
    a = _matmul(o, swa_wo, BRANCH_DTYPE, 1024, 512, name="swa_out")
    h, hn_ffn = _residual_norm(a, h, row(norm_mix_post[1]), [row(norm_ffn_pre[1])], perm=perm)
    (h,) = ffn(1, h, hn_ffn, w_in1, [])
    return h.reshape(batch, seq, d)
```

```python
import functools

import jax
import jax.numpy as jnp
from jax import lax
from jax.experimental import pallas as pl
from jax.experimental.pallas import tpu as pltpu

EPS = 1e-6
LOG2_E = 1.4426950408889634
ROPE_THETA = 10000.0
MLA_NOPE = 128
MLA_ROPE = 64
MLA_V = 128
MLA_QK = MLA_NOPE + MLA_ROPE
MLA_QK_PAD = MLA_NOPE + 2 * MLA_ROPE
SWA_HEAD_DIM = 64
SWA_KV_HEADS = 8
WINDOW = 128
CONV_W = 3

V7X_VMEM_BYTES = 64 * 1024 * 1024
VMEM_LIMIT_CAP = V7X_VMEM_BYTES - 8 * 1024 * 1024
SUBLANES = 8
PERM_UNIT = 256
PERM_GROUPS = PERM_UNIT // SUBLANES

BF16 = jnp.bfloat16
F32 = jnp.float32
BRANCH_DTYPE = BF16


def _nbytes(shape, dtype):
    n = 1
    for s in shape:
        n *= s
    return n * jnp.dtype(dtype).itemsize


def _params(semantics, blocks, extra=0):
    est = 2 * sum(_nbytes(s, d) for s, d in blocks) + extra
    limit = min(VMEM_LIMIT_CAP, max(est + est // 4, 16 * 1024 * 1024))
    return pltpu.CompilerParams(dimension_semantics=semantics, vmem_limit_bytes=limit)


def _rms(x, g):
    r = lax.rsqrt(jnp.mean(x * x, axis=-1, keepdims=True) + EPS)
    return (x * r) * g


def _rope_table_kernel(pos_ref, freq_ref, cos_ref, sin_ref):
    ang = pos_ref[...].astype(F32) * freq_ref[...]
    lane = lax.broadcasted_iota(jnp.int32, ang.shape, 1)
    s = jnp.sin(ang)
    cos_ref[...] = jnp.cos(ang)
    sin_ref[...] = jnp.where(lane % MLA_ROPE < MLA_ROPE // 2, -s, s)


def _rope_pair(r, cos2, sin2):
    lane = lax.broadcasted_iota(jnp.int32, r.shape, 1)
    half = MLA_ROPE // 2
    partner = jnp.where(lane % MLA_ROPE < half, pltpu.roll(r, r.shape[1] - half, 1), pltpu.roll(r, half, 1))
    return r * cos2 + partner * sin2


def _rope_tables(pos_col, freqs2):
    m = pos_col.shape[0]
    d = freqs2.shape[1]
    tm = 1024
    return pl.pallas_call(
        _rope_table_kernel,
        out_shape=(jax.ShapeDtypeStruct((m, d), F32), jax.ShapeDtypeStruct((m, d), F32)),
        grid=(m // tm,),
        in_specs=[pl.BlockSpec((tm, 1), lambda i: (i, 0)), pl.BlockSpec((1, d), lambda i: (0, 0))],
        out_specs=(pl.BlockSpec((tm, d), lambda i: (i, 0)), pl.BlockSpec((tm, d), lambda i: (i, 0))),
        name="rope_tables",
    )(pos_col, freqs2)


def _mla_down_kernel(x_ref, g_ref, wdq_ref, qn_ref, wdkv_ref, kvn_ref, cos_ref, sin_ref,
                     cq_ref, c_ref, ct_ref, kr_ref):
    hn = _rms(x_ref[...], g_ref[...]).astype(BF16)
    cq = jnp.dot(hn, wdq_ref[...], preferred_element_type=F32)
    cq_ref[...] = _rms(cq, qn_ref[...]).astype(BF16)
    ckv = jnp.dot(hn, wdkv_ref[...], preferred_element_type=F32)
    lora = c_ref.shape[-1]
    c = _rms(ckv[:, :lora], kvn_ref[...])
    c_ref[...] = c.astype(BF16)
    ct_ref[...] = c.T.astype(BF16)
    kr_ref[...] = _rope_pair(ckv[:, lora:], cos_ref[...], sin_ref[...]).astype(BF16)


def _mla_down(x2, g, wdq, qn, wdkv, kvn, cosf, sinf):
    m, d = x2.shape
    nq = wdq.shape[1]
    nkv = wdkv.shape[1]
    lora = kvn.shape[1]
    rd = nkv - lora
    tm = 256
    row = lambda i: (i, 0)
    fixed = lambda i: (0, 0)
    blocks = [((tm, d), F32), ((d, nq), BF16), ((d, nkv), BF16), ((tm, nq), BF16), ((tm, lora), BF16)]
    return pl.pallas_call(
        _mla_down_kernel,
        out_shape=(jax.ShapeDtypeStruct((m, nq), BF16), jax.ShapeDtypeStruct((m, lora), BF16),
                   jax.ShapeDtypeStruct((lora, m), BF16), jax.ShapeDtypeStruct((m, rd), BF16)),
        grid=(m // tm,),
        in_specs=[pl.BlockSpec((tm, d), row), pl.BlockSpec((1, d), fixed),
                  pl.BlockSpec((d, nq), fixed), pl.BlockSpec((1, nq), fixed),
                  pl.BlockSpec((d, nkv), fixed), pl.BlockSpec((1, lora), fixed),
                  pl.BlockSpec((tm, rd), row), pl.BlockSpec((tm, rd), row)],
        out_specs=(pl.BlockSpec((tm, nq), row), pl.BlockSpec((tm, lora), row),
                   pl.BlockSpec((lora, tm), lambda i: (0, i)), pl.BlockSpec((tm, rd), row)),
        compiler_params=_params(("arbitrary",), blocks, extra=4 * _nbytes((tm, d), F32)),
        name="mla_down",
    )(x2, g, wdq, qn, wdkv, kvn, cosf, sinf)


def _mla_q_up_kernel(cq_ref, w_ref, cos_ref, sin_ref, q_ref, *, scale):
    res = jnp.dot(cq_ref[...], w_ref[0], preferred_element_type=F32)
    hpg = q_ref.shape[1]
    cos2 = cos_ref[...]
    sin2 = sin_ref[...]
    for hh in range(hpg):
        q_ref[0, hh, :, :MLA_NOPE] = (res[:, hh * MLA_NOPE:(hh + 1) * MLA_NOPE] * scale).astype(BF16)
    low = lax.broadcasted_iota(jnp.int32, cos2.shape, 1) < MLA_ROPE
    for p in range(hpg // 2):
        base = hpg * MLA_NOPE + p * 2 * MLA_ROPE
        r = _rope_pair(res[:, base:base + 2 * MLA_ROPE], cos2, sin2) * scale
        q_ref[0, 2 * p, :, MLA_NOPE:] = jnp.where(low, r, 0.0).astype(BF16)
        q_ref[0, 2 * p + 1, :, MLA_NOPE:] = jnp.where(low, 0.0, r).astype(BF16)


def _mla_q_up(cq, w3, cos2, sin2, batch, seq, scale):
    m, k = cq.shape
    groups, _, gn = w3.shape
    hpg = gn // MLA_QK
    heads = groups * hpg
    tm = 512
    nsb = seq // tm
    blocks = [((tm, k), BF16), ((k, gn), BF16), ((hpg, tm, MLA_QK_PAD), BF16)]
    return pl.pallas_call(
        functools.partial(_mla_q_up_kernel, scale=scale),
        out_shape=jax.ShapeDtypeStruct((batch, heads, seq, MLA_QK_PAD), BF16),
        grid=(m // tm, groups),
        in_specs=[pl.BlockSpec((tm, k), lambda i, g: (i, 0)),
                  pl.BlockSpec((1, k, gn), lambda i, g: (g, 0, 0)),
                  pl.BlockSpec((tm, 2 * MLA_ROPE), lambda i, g: (i, 0)),
                  pl.BlockSpec((tm, 2 * MLA_ROPE), lambda i, g: (i, 0))],
        out_specs=pl.BlockSpec((1, hpg, tm, MLA_QK_PAD), lambda i, g: (i // nsb, g, i % nsb, 0)),
        compiler_params=_params(("arbitrary", "arbitrary"), blocks, extra=2 * _nbytes((tm, gn), F32)),
        name="mla_q_up",
    )(cq, w3, cos2, sin2)


def _mla_kv_up_kernel(c_ref, ct_ref, wk_ref, wvt_ref, kr_ref, k_ref, vt_ref):
    kn = jnp.dot(c_ref[...], wk_ref[0], preferred_element_type=F32)
    vt = jnp.dot(wvt_ref[0], ct_ref[...], preferred_element_type=F32)
    kr = kr_ref[...]
    for hh in range(k_ref.shape[1]):
        k_ref[0, hh, :, :MLA_NOPE] = kn[:, hh * MLA_NOPE:(hh + 1) * MLA_NOPE].astype(BF16)
        k_ref[0, hh, :, MLA_NOPE:] = kr
        vt_ref[0, hh] = vt[hh * MLA_V:(hh + 1) * MLA_V, :].astype(BF16)


def _mla_kv_up(c, ct, wk3, wvt3, kr, batch, seq):
    m, k = c.shape
    groups, _, gk = wk3.shape
    hpg = gk // MLA_NOPE
    heads = groups * hpg
    tm = 512
    nsb = seq // tm
    blocks = [((tm, k), BF16), ((k, tm), BF16), ((k, gk), BF16), ((hpg * MLA_V, k), BF16),
              ((hpg, tm, MLA_QK_PAD), BF16), ((hpg, tm, MLA_V), BF16)]
    omap = lambda i, g: (i // nsb, g, i % nsb, 0)
    vmap = lambda i, g: (i // nsb, g, 0, i % nsb)
    return pl.pallas_call(
        _mla_kv_up_kernel,
        out_shape=(jax.ShapeDtypeStruct((batch, heads, seq, MLA_QK_PAD), BF16),
                   jax.ShapeDtypeStruct((batch, heads, MLA_V, seq), BF16)),
        grid=(m // tm, groups),
        in_specs=[pl.BlockSpec((tm, k), lambda i, g: (i, 0)),
                  pl.BlockSpec((k, tm), lambda i, g: (0, i)),
                  pl.BlockSpec((1, k, gk), lambda i, g: (g, 0, 0)),
                  pl.BlockSpec((1, hpg * MLA_V, k), lambda i, g: (g, 0, 0)),
                  pl.BlockSpec((tm, 2 * MLA_ROPE), lambda i, g: (i, 0))],
        out_specs=(pl.BlockSpec((1, hpg, tm, MLA_QK_PAD), omap), pl.BlockSpec((1, hpg, MLA_V, tm), vmap)),
        compiler_params=_params(("arbitrary", "arbitrary"), blocks, extra=4 * _nbytes((tm, gk), F32)),
        name="mla_kv_up",
    )(c, ct, wk3, wvt3, kr)


def _cast_riders(weights, layer, nsteps, step_of):
    in_specs, out_shapes, out_specs, blocks = [], [], [], []
    for w in weights:
        _, rows, cols = w.shape
        slab = rows // nsteps
        assert slab * nsteps == rows and slab % (2 * SUBLANES) == 0, (w.shape, nsteps)
        in_specs.append(pl.BlockSpec((None, slab, cols), lambda *g: (layer, step_of(*g), 0)))
        out_shapes.append(jax.ShapeDtypeStruct((rows, cols), BF16))
        out_specs.append(pl.BlockSpec((slab, cols), lambda *g: (step_of(*g), 0)))
        blocks += [((slab, cols), F32), ((slab, cols), BF16)]
    return in_specs, out_shapes, out_specs, blocks


def _run_cast_riders(src_refs, dst_refs):
    for src, dst in zip(src_refs, dst_refs):
        dst[...] = src[...].astype(dst.dtype)


def _mla_attn_kernel(*refs, tk, n_riders):
    q_ref, k_ref, vt_ref = refs[:3]
    o_ref = refs[3 + n_riders]
    sa_ref, sb_ref, m_ref, l_ref, acc_ref = refs[4 + 2 * n_riders:]
    _run_cast_riders(refs[3:3 + n_riders], refs[4 + n_riders:4 + 2 * n_riders])
    i = pl.program_id(2)
    hp = q_ref.shape[1]
    dv = acc_ref.shape[1]
    m_ref[...] = jnp.full(m_ref.shape, -jnp.inf, F32)
    l_ref[...] = jnp.zeros(l_ref.shape, F32)
    acc_ref[...] = jnp.zeros(acc_ref.shape, F32)

    def scores(j, s_ref):
        start = pl.multiple_of(j * tk, tk)
        for hh in range(hp):
            k = k_ref[0, hh, pl.ds(start, tk), :]
            s_ref[hh] = lax.dot_general(k, q_ref[0, hh], (((1,), (1,)), ((), ())),
                                        preferred_element_type=F32)

    def update(j, s_ref, masked):
        start = pl.multiple_of(j * tk, tk)
        for hh in range(hp):
            st = s_ref[hh]
            if masked:
                kj = lax.broadcasted_iota(jnp.int32, st.shape, 0)
                qi = lax.broadcasted_iota(jnp.int32, st.shape, 1)
                st = jnp.where(kj <= qi, st, -jnp.inf)
            vt = vt_ref[0, hh, :, pl.ds(start, tk)]
            m_prev = m_ref[hh]
            m_new = jnp.maximum(m_prev, jnp.max(st, axis=0, keepdims=True))
            alpha = jnp.exp2(m_prev - m_new)
            pt = jnp.exp2(st - m_new)
            l_ref[hh] = alpha * l_ref[hh] + jnp.sum(pt, axis=0, keepdims=True)
            acc_ref[hh] = alpha * acc_ref[hh] + jnp.dot(vt, pt.astype(BF16), preferred_element_type=F32)
            m_ref[hh] = m_new

    def finish():
        for hh in range(hp):
            o = acc_ref[hh] / l_ref[hh]
            o_ref[:, hh * dv:(hh + 1) * dv] = o.T.astype(o_ref.dtype)

    scores(0, sa_ref)

    def pair(jj, carry):
        j = 2 * jj
        scores(j + 1, sb_ref)
        update(j, sa_ref, False)
        scores(j + 2, sa_ref)
        update(j + 1, sb_ref, False)
        return carry

    lax.fori_loop(0, i // 2, pair, 0)

    @pl.when(i % 2 == 0)
    def _():
        update(i, sa_ref, True)
        finish()

    @pl.when(i % 2 == 1)
    def _():
        scores(i, sb_ref)
        update(i - 1, sa_ref, False)
        update(i, sb_ref, True)
        finish()


def _mla_attention(q, k, vt, hp, ride_weights, ride_layer):
    batch, heads, seq, dqk = q.shape
    dv = vt.shape[2]
    tq = 512
    nh, nq = heads // hp, seq // tq
    r_in, r_shapes, r_out, r_blocks = _cast_riders(ride_weights, ride_layer, batch * nh * nq,
                                                   lambda b, h, i: (b * nh + h) * nq + i)
    blocks = [((hp, tq, 256), BF16), ((hp, seq, 256), BF16), ((hp, dv, seq), BF16), ((tq, hp * dv), BF16)]
    return pl.pallas_call(
        functools.partial(_mla_attn_kernel, tk=tq, n_riders=len(ride_weights)),
        out_shape=(jax.ShapeDtypeStruct((batch * seq, heads * dv), BF16), *r_shapes),
        grid=(batch, nh, nq),
        in_specs=[pl.BlockSpec((1, hp, tq, dqk), lambda b, h, i: (b, h, i, 0)),
                  pl.BlockSpec((1, hp, seq, dqk), lambda b, h, i: (b, h, 0, 0)),
                  pl.BlockSpec((1, hp, dv, seq), lambda b, h, i: (b, h, 0, 0)), *r_in],
        out_specs=(pl.BlockSpec((tq, hp * dv), lambda b, h, i: (b * nq + i, h)), *r_out),
        scratch_shapes=[pltpu.VMEM((hp, tq, tq), F32), pltpu.VMEM((hp, tq, tq), F32),
                        pltpu.VMEM((hp, 1, tq), F32), pltpu.VMEM((hp, 1, tq), F32),
                        pltpu.VMEM((hp, dv, tq), F32)],
        compiler_params=_params(("arbitrary", "arbitrary", "arbitrary"), blocks + r_blocks,
                                extra=8 * hp * _nbytes((tq, tq), F32)),
        name="mla_attention",
    )(q, k, vt, *ride_weights)


def _matmul_kernel(x_ref, w_ref, o_ref, *, scale):
    acc = jnp.dot(x_ref[...], w_ref[...], preferred_element_type=F32)
    if scale is not None:
        acc = acc * scale
    o_ref[...] = acc.astype(o_ref.dtype)


def _matmul(x, w, out_dtype, tm, tn, scale=None, name="matmul"):
    m, k = x.shape
    n = w.shape[1]
    blocks = [((tm, k), x.dtype), ((k, tn), w.dtype), ((tm, tn), out_dtype)]
    return pl.pallas_call(
        functools.partial(_matmul_kernel, scale=scale),
        out_shape=jax.ShapeDtypeStruct((m, n), out_dtype),
        grid=(m // tm, n // tn),
        in_specs=[pl.BlockSpec((tm, k), lambda i, j: (i, 0)), pl.BlockSpec((k, tn), lambda i, j: (0, j))],
        out_specs=pl.BlockSpec((tm, tn), lambda i, j: (i, j)),
        compiler_params=_params(("arbitrary", "arbitrary"), blocks, extra=2 * _nbytes((tm, tn), F32)),
        name=name,
    )(x, w)


def _residual_norm_kernel(*refs, n_next, a_interleaved, out_interleaved):
    a_ref, h_ref, gp_ref = refs[:3]
    gn_refs = refs[3:3 + n_next]
    use_perm = a_interleaved or out_interleaved
    perm_ref = refs[3 + n_next] if use_perm else None
    out_refs = refs[3 + n_next + (1 if use_perm else 0):]
    a = a_ref[...]
    if a_interleaved:
        a = jnp.dot(perm_ref[...], a, preferred_element_type=F32)
    h_new = h_ref[...] + _rms(a.astype(F32), gp_ref[...])
    out_refs[0][...] = h_new
    for gn_ref, o_ref in zip(gn_refs, out_refs[1:]):
        hn = _rms(h_new, gn_ref[...]).astype(o_ref.dtype)
        if out_interleaved:
            hn = jnp.dot(perm_ref[...], hn, preferred_element_type=F32).astype(o_ref.dtype)
        o_ref[...] = hn


def _residual_norm(a, h, g_post, g_next, perm, a_interleaved=False, out_interleaved=False):
    assert not (a_interleaved and out_interleaved)
    m, d = a.shape
    n_next = len(g_next)
    tm = PERM_UNIT
    row = lambda i: (i, 0)
    fixed = lambda i: (0, 0)
    interleave = a_interleaved or out_interleaved
    assert not a_interleaved or a.dtype == BF16, "the permutation matmul is exact only for bf16 rows"
    out_shape = [jax.ShapeDtypeStruct((m, d), F32)] + [jax.ShapeDtypeStruct((m, d), BF16)] * n_next
    out_specs = [pl.BlockSpec((tm, d), row)] * (1 + n_next)
    blocks = [((tm, d), F32)] * (4 + n_next)
    perm_specs = [pl.BlockSpec((tm, tm), fixed)] if interleave else []
    perm_args = [perm] if interleave else []
    return pl.pallas_call(
        functools.partial(_residual_norm_kernel, n_next=n_next, a_interleaved=a_interleaved,
                          out_interleaved=out_interleaved),
        out_shape=tuple(out_shape),
        grid=(m // tm,),
        in_specs=[pl.BlockSpec((tm, d), row), pl.BlockSpec((tm, d), row)]
        + [pl.BlockSpec((1, d), fixed)] * (1 + n_next) + perm_specs,
        out_specs=tuple(out_specs),
        compiler_params=_params(("arbitrary",), blocks),
        name="residual_norm",
    )(a, h, g_post, *g_next, *perm_args)


def _interleave_permutation():
    r = jnp.arange(PERM_UNIT)
    t = (r % SUBLANES) * PERM_GROUPS + r // SUBLANES
    return (t[:, None] == jnp.arange(PERM_UNIT)[None, :]).astype(BF16)


def _ffn_up_kernel(x_ref, wg_ref, wv_ref, cwg_ref, cwv_ref, cbg_ref, cbv_ref, wdn_ref, o_ref,
                   wdn_bf_ref, carry_ref, *, blocks_per_seq):
    _run_cast_riders([wdn_ref], [wdn_bf_ref])
    i = pl.program_id(0)
    j = pl.program_id(1)
    tn = wg_ref.shape[1]
    tm = x_ref.shape[0]
    grp = SUBLANES

    @pl.when((i % blocks_per_seq) == 0)
    def _():
        carry_ref[j] = jnp.zeros(carry_ref.shape[1:], F32)

    w = jnp.concatenate([wg_ref[...], wv_ref[...]], axis=1)
    cw = jnp.concatenate([cwg_ref[...], cwv_ref[...]], axis=1)
    cb = jnp.concatenate([cbg_ref[...], cbv_ref[...]], axis=1)
    first_sublane = lax.broadcasted_iota(jnp.int32, (grp, 2 * tn), 0) == 0
    tail = carry_ref[j]

    def project(c):
        return jnp.dot(x_ref[c * PERM_UNIT:(c + 1) * PERM_UNIT, :], w, preferred_element_type=F32)

    nunits = tm // PERM_UNIT
    u_next = project(0)
    for c in range(nunits):
        rows = slice(c * PERM_UNIT, (c + 1) * PERM_UNIT)
        u = u_next
        if c + 1 < nunits:
            u_next = project(c + 1)
        new_tail = u[PERM_UNIT - 2 * grp:, :]
        wrapped = jnp.where(jnp.concatenate([first_sublane, first_sublane], axis=0),
                            jnp.concatenate([pltpu.roll(tail[:grp], 1, 0), pltpu.roll(tail[grp:], 1, 0)], axis=0),
                            jnp.concatenate([pltpu.roll(new_tail[:grp], 1, 0), pltpu.roll(new_tail[grp:], 1, 0)],
                                            axis=0))
        ext = jnp.concatenate([wrapped, u], axis=0)
        acc = cw[0:1, :] * ext[0:PERM_UNIT] + cw[1:2, :] * ext[grp:grp + PERM_UNIT] + cw[2:3, :] * u
        conv = cb + acc
        gate, val = conv[:, :tn], conv[:, tn:]
        o_ref[rows, :] = (gate * (1.0 / (1.0 + jnp.exp(-gate))) * val).astype(o_ref.dtype)
        tail = new_tail
    carry_ref[j] = tail


def _ffn_up(hn, w_in, conv_w, conv_b, w_out_all, layer, seq):
    m, d = hn.shape
    d_ff = w_in.shape[1] // 2
    tn = 256
    nj = d_ff // tn
    tm = 2048
    r_in, r_shapes, r_out, r_blocks = _cast_riders([w_out_all], layer, (m // tm) * nj, lambda i, j: i * nj + j)
    blocks = [((tm, d), BF16), ((d, tn), BF16), ((d, tn), BF16), ((tm, tn), BF16)] + r_blocks
    scratch = _nbytes((nj, 2 * SUBLANES, 2 * tn), F32)
    gate_col = lambda i, j: (0, j)
    val_col = lambda i, j: (0, j + nj)
    return pl.pallas_call(
        functools.partial(_ffn_up_kernel, blocks_per_seq=seq // tm),
        out_shape=(jax.ShapeDtypeStruct((m, d_ff), BF16), *r_shapes),
        grid=(m // tm, nj),
        in_specs=[pl.BlockSpec((tm, d), lambda i, j: (i, 0)),
                  pl.BlockSpec((d, tn), gate_col), pl.BlockSpec((d, tn), val_col),
                  pl.BlockSpec((CONV_W, tn), gate_col), pl.BlockSpec((CONV_W, tn), val_col),
                  pl.BlockSpec((1, tn), gate_col), pl.BlockSpec((1, tn), val_col), *r_in],
        out_specs=(pl.BlockSpec((tm, tn), lambda i, j: (i, j)), *r_out),
        scratch_shapes=[pltpu.VMEM((nj, 2 * SUBLANES, 2 * tn), F32)],
        compiler_params=_params(("arbitrary", "arbitrary"), blocks,
                                extra=scratch + 10 * _nbytes((PERM_UNIT, 2 * tn), F32)),
        name="ffn_up",
    )(hn, w_in, w_in, conv_w, conv_w, conv_b, conv_b, w_out_all)


def _swa_kv_kernel(x_ref, w_ref, k_ref, vt_ref):
    res = jnp.dot(x_ref[...], w_ref[...], preferred_element_type=F32)
    groups = k_ref.shape[0]
    hd = SWA_HEAD_DIM
    for t in range(groups):
        k_ref[t] = res[:, t * hd:(t + 1) * hd].astype(k_ref.dtype)
    vt = res[:, groups * hd:].T
    for t in range(groups):
        vt_ref[t] = vt[t * hd:(t + 1) * hd, :].astype(vt_ref.dtype)


def _swa_kv(x, w):
    m, k = x.shape
    n = w.shape[1]
    groups = n // (2 * SWA_HEAD_DIM)
    tm = 512
    blocks = [((tm, k), BF16), ((k, n), BF16), ((groups, tm, 128), BF16), ((groups, SWA_HEAD_DIM, tm), BF16)]
    return pl.pallas_call(
        _swa_kv_kernel,
        out_shape=(jax.ShapeDtypeStruct((groups, m, SWA_HEAD_DIM), BF16),
                   jax.ShapeDtypeStruct((groups, SWA_HEAD_DIM, m), BF16)),
        grid=(m // tm,),
        in_specs=[pl.BlockSpec((tm, k), lambda i: (i, 0)), pl.BlockSpec((k, n), lambda i: (0, 0))],
        out_specs=(pl.BlockSpec((groups, tm, SWA_HEAD_DIM), lambda i: (0, i, 0)),
                   pl.BlockSpec((groups, SWA_HEAD_DIM, tm), lambda i: (0, 0, i))),
        compiler_params=_params(("arbitrary",), blocks, extra=3 * _nbytes((tm, n), F32)),
        name="swa_kv",
    )(x, w)


def _swa_attn_kernel(*refs, rep, n_riders):
    bias_ref, sink_ref, q_ref, kp_ref, kc_ref, vtp_ref, vtc_ref = refs[:7]
    o_ref = refs[7 + n_riders]
    _run_cast_riders(refs[7:7 + n_riders], refs[8 + n_riders:8 + 2 * n_riders])
    n = pl.program_id(2)
    w = WINDOW
    hd = SWA_HEAD_DIM
    nq = q_ref.shape[0] // w
    kfull = jnp.concatenate([kp_ref[0], kc_ref[0]], axis=0)
    vtfull = jnp.concatenate([vtp_ref[0], vtc_ref[0]], axis=1)
    sink = sink_ref[0]
    first = jnp.minimum(n, 1)
    for c in range(nq):
        qt = q_ref[c * w:(c + 1) * w, :].astype(F32).T
        qt = jnp.concatenate([qt[r * hd:(r + 1) * hd, :] for r in range(rep)], axis=1).astype(BF16)
        bias = bias_ref[0, first] if c == 0 else bias_ref[0, 1]
        st = jnp.dot(kfull[c * w:(c + 2) * w], qt, preferred_element_type=F32) + bias
        mx = jnp.maximum(jnp.max(st, axis=0, keepdims=True), sink)
        e = jnp.exp2(st - mx)
        denom = jnp.sum(e, axis=0, keepdims=True) + jnp.exp2(sink - mx)
        ot = jnp.dot(vtfull[:, c * w:(c + 2) * w], e.astype(BF16), preferred_element_type=F32)
        ot = ot * (1.0 / denom)
        o = jnp.concatenate([ot[:, r * w:(r + 1) * w] for r in range(rep)], axis=0)
        o_ref[c * w:(c + 1) * w, :] = o.T.astype(o_ref.dtype)


def _swa_attention(q, k, vt, bias, sink_lanes, batch, seq, ride_weights, ride_layer):
    m, dq = q.shape
    groups = k.shape[0]
    rep = dq // (groups * SWA_HEAD_DIM)
    w = WINDOW
    nq = 4
    tq = nq * w
    nb = seq // tq
    gw = rep * SWA_HEAD_DIM
    prev_blk = lambda b, n: jnp.maximum((b * nb + n) * nq - 1, b * nb * nq)
    r_in, r_shapes, r_out, r_blocks = _cast_riders(ride_weights, ride_layer, groups * batch * nb,
                                                   lambda g, b, n: (g * batch + b) * nb + n)
    blocks = [((2, 2 * w, rep * w), F32), ((tq, gw), BF16), ((tq, gw), BF16)] + r_blocks
    return pl.pallas_call(
        functools.partial(_swa_attn_kernel, rep=rep, n_riders=len(ride_weights)),
        out_shape=(jax.ShapeDtypeStruct((m, dq), BF16), *r_shapes),
        grid=(groups, batch, nb),
        in_specs=[pl.BlockSpec((1, 2, 2 * w, rep * w), lambda g, b, n: (g, 0, 0, 0)),
                  pl.BlockSpec((1, 1, rep * w), lambda g, b, n: (g, 0, 0)),
                  pl.BlockSpec((tq, gw), lambda g, b, n: (b * nb + n, g)),
                  pl.BlockSpec((1, w, SWA_HEAD_DIM), lambda g, b, n: (g, prev_blk(b, n), 0)),
                  pl.BlockSpec((1, tq, SWA_HEAD_DIM), lambda g, b, n: (g, b * nb + n, 0)),
                  pl.BlockSpec((1, SWA_HEAD_DIM, w), lambda g, b, n: (g, 0, prev_blk(b, n))),
                  pl.BlockSpec((1, SWA_HEAD_DIM, tq), lambda g, b, n: (g, 0, b * nb + n)), *r_in],
        out_specs=(pl.BlockSpec((tq, gw), lambda g, b, n: (b * nb + n, g)), *r_out),
        compiler_params=_params(("arbitrary", "arbitrary", "arbitrary"), blocks,
                                extra=6 * nq * _nbytes((2 * w, rep * w), F32)),
        name="swa_attention",
    )(bias, sink_lanes, q, k, k, vt, vt, *ride_weights)


def _swa_bias(swa_heads, groups):
    w = WINDOW
    rep = swa_heads // groups
    slopes = 2.0 ** (-8.0 * jnp.arange(1, swa_heads + 1, dtype=F32) / swa_heads)
    kj = jnp.arange(2 * w)[:, None]
    qi = jnp.arange(w)[None, :]
    dist = qi + w - kj
    in_window = (dist >= 0) & (dist < w)
    valid = jnp.stack([in_window & (kj >= w), in_window])
    alibi = slopes.reshape(groups, 1, 1, rep, 1) * dist.astype(F32)[None, None, :, None, :]
    bias = jnp.where(valid[None, :, :, None, :], -alibi * LOG2_E, -jnp.inf)
    return bias.reshape(groups, 2, 2 * w, rep * w)


def kernel(x, positions, norm_mix_pre, norm_mix_post, norm_ffn_pre, norm_ffn_post, mla_w_dq, mla_q_norm,
           mla_w_uq, mla_w_dkv, mla_kv_norm, mla_w_ukv, mla_w_o, shared_kv_norm, swa_w_k, swa_w_v, swa_w_q,
           swa_sinks, swa_w_o, ffn_w_in, ffn_conv_w, ffn_conv_b, ffn_w_out):
    batch, seq, d = x.shape
    m = batch * seq
    depth = norm_mix_pre.shape[0]
    n_a = mla_w_dq.shape[0]
    heads = mla_w_o.shape[1] // MLA_V
    swa_heads = swa_w_q.shape[2] // SWA_HEAD_DIM
    hpg = 4

    half = MLA_ROPE // 2
    freqs = ROPE_THETA ** (-jnp.arange(half, dtype=F32) / half)
    cos2, sin2 = _rope_tables(positions.reshape(m, 1), jnp.tile(freqs, 4)[None, :])

    row = lambda v: v.reshape(1, -1)
    h = x.reshape(m, d)
    assert depth == 2 and n_a == 1, "wired for one MLA layer followed by one sliding-window layer"

    perm = _interleave_permutation()

    def ffn(l, h, hn_ffn, w_in, g_next):
        act, w_out = _ffn_up(hn_ffn, w_in, ffn_conv_w[l], row(ffn_conv_b[l]), ffn_w_out, l, seq)
        f = _matmul(act, w_out, BRANCH_DTYPE, 512, 512, name="ffn_down")
        return _residual_norm(f, h, row(norm_ffn_post[l]), g_next, perm.T, a_interleaved=True)

    w_uq = mla_w_uq[0].astype(BF16).reshape(-1, heads // hpg, hpg, MLA_QK)
    w_uq3 = jnp.concatenate([w_uq[..., :MLA_NOPE].reshape(-1, heads // hpg, hpg * MLA_NOPE),
                             w_uq[..., MLA_NOPE:].reshape(-1, heads // hpg, hpg * MLA_ROPE)],
                            axis=2).transpose(1, 0, 2)
    w_ukv = mla_w_ukv[0].astype(BF16).reshape(-1, heads // hpg, hpg, MLA_NOPE + MLA_V)
    w_uk3 = w_ukv[..., :MLA_NOPE].reshape(-1, heads // hpg, hpg * MLA_NOPE).transpose(1, 0, 2)
    w_uvt3 = w_ukv[..., MLA_NOPE:].reshape(-1, heads // hpg, hpg * MLA_V).transpose(1, 2, 0)
    lora = mla_kv_norm.shape[1]
    w_dkv = mla_w_dkv[0].astype(BF16)
    w_dkv2 = jnp.concatenate([w_dkv, w_dkv[:, lora:]], axis=1)
    cq, c, ct, kr = _mla_down(h, row(norm_mix_pre[0]), mla_w_dq[0].astype(BF16), row(mla_q_norm[0]),
                              w_dkv2, row(mla_kv_norm[0]), cos2, sin2)
    q = _mla_q_up(cq, w_uq3, cos2, sin2, batch, seq, MLA_QK ** -0.5 * LOG2_E)
    k, vt = _mla_kv_up(c, ct, w_uk3, w_uvt3, kr, batch, seq)
    o, w_in0, mla_wo, swa_wq, swa_wo = _mla_attention(q, k, vt, 4, [ffn_w_in, mla_w_o, swa_w_q, swa_w_o], 0)
    a = _matmul(o, mla_wo, BRANCH_DTYPE, 2048, 512, name="mla_out")
    h, hn_ffn = _residual_norm(a, h, row(norm_mix_post[0]), [row(norm_ffn_pre[0])], perm, out_interleaved=True)
    h, hn_mix, hn_kv = ffn(0, h, hn_ffn, w_in0, [row(norm_mix_pre[1]), row(shared_kv_norm)])

    w_kv = jnp.concatenate([swa_w_k, swa_w_v], axis=1).astype(BF16)
    k_shared, vt_shared = _swa_kv(hn_kv, w_kv)
    groups = k_shared.shape[0]
    sink_lanes = jnp.repeat(swa_sinks[0].reshape(groups, 1, swa_heads // groups) * LOG2_E, WINDOW, axis=2)
    q = _matmul(hn_mix, swa_wq, BF16, 2048, 512, scale=SWA_HEAD_DIM ** -0.5 * LOG2_E, name="swa_q")
    o, w_in1 = _swa_attention(q, k_shared, vt_shared, _swa_bias(swa_heads, groups), sink_lanes,
                              batch, seq, [ffn_w_in], 1)
    a = _matmul(o, swa_wo, BRANCH_DTYPE, 2048, 512, name="swa_out")
    h, hn_ffn = _residual_norm(a, h, row(norm_mix_post[1]), [row(norm_ffn_pre[1])], perm, out_interleaved=True)
    (h,) = ffn(1, h, hn_ffn, w_in1, [])
    return h.reshape(batch, seq, d)
```

```python
import functools

import jax
import jax.numpy as jnp
from jax import lax
from jax.experimental import pallas as pl
from jax.experimental.pallas import tpu as pltpu

EPS = 1e-6
LOG2_E = 1.4426950408889634
ROPE_THETA = 10000.0
MLA_NOPE = 128
MLA_ROPE = 64
MLA_V = 128
MLA_QK = MLA_NOPE + MLA_ROPE
MLA_QK_PAD = MLA_NOPE + 2 * MLA_ROPE
SWA_HEAD_DIM = 64
SWA_KV_HEADS = 8
WINDOW = 128
CONV_W = 3

V7X_VMEM_BYTES = 64 * 1024 * 1024
VMEM_LIMIT_CAP = V7X_VMEM_BYTES - 8 * 1024 * 1024
SUBLANES = 8
PERM_UNIT = 256
PERM_GROUPS = PERM_UNIT // SUBLANES

BF16 = jnp.bfloat16
F32 = jnp.float32
BRANCH_DTYPE = BF16


def _nbytes(shape, dtype):
    n = 1
    for s in shape:
        n *= s
    return n * jnp.dtype(dtype).itemsize


def _params(semantics, blocks, extra=0):
    est = 2 * sum(_nbytes(s, d) for s, d in blocks) + extra
    limit = min(VMEM_LIMIT_CAP, max(est + est // 4, 16 * 1024 * 1024))
    return pltpu.CompilerParams(dimension_semantics=semantics, vmem_limit_bytes=limit)


def _rms(x, g):
    r = lax.rsqrt(jnp.mean(x * x, axis=-1, keepdims=True) + EPS)
    return (x * r) * g


def _rope_table_kernel(pos_ref, freq_ref, cos_ref, sin_ref):
    ang = pos_ref[...].astype(F32) * freq_ref[...]
    lane = lax.broadcasted_iota(jnp.int32, ang.shape, 1)
    s = jnp.sin(ang)
    cos_ref[...] = jnp.cos(ang)
    sin_ref[...] = jnp.where(lane % MLA_ROPE < MLA_ROPE // 2, -s, s)


def _rope_pair(r, cos2, sin2):
    lane = lax.broadcasted_iota(jnp.int32, r.shape, 1)
    half = MLA_ROPE // 2
    partner = jnp.where(lane % MLA_ROPE < half, pltpu.roll(r, r.shape[1] - half, 1), pltpu.roll(r, half, 1))
    return r * cos2 + partner * sin2


def _rope_tables(pos_col, freqs2):
    m = pos_col.shape[0]
    d = freqs2.shape[1]
    tm = 1024
    return pl.pallas_call(
        _rope_table_kernel,
        out_shape=(jax.ShapeDtypeStruct((m, d), F32), jax.ShapeDtypeStruct((m, d), F32)),
        grid=(m // tm,),
        in_specs=[pl.BlockSpec((tm, 1), lambda i: (i, 0)), pl.BlockSpec((1, d), lambda i: (0, 0))],
        out_specs=(pl.BlockSpec((tm, d), lambda i: (i, 0)), pl.BlockSpec((tm, d), lambda i: (i, 0))),
        name="rope_tables",
    )(pos_col, freqs2)


def _mla_down_kernel(x_ref, g_ref, wdq_ref, qn_ref, wdkv_ref, kvn_ref, cos_ref, sin_ref,
                     cq_ref, c_ref, ct_ref, kr_ref):
    hn = _rms(x_ref[...], g_ref[...]).astype(BF16)
    cq = jnp.dot(hn, wdq_ref[...], preferred_element_type=F32)
    cq_ref[...] = _rms(cq, qn_ref[...]).astype(BF16)
    ckv = jnp.dot(hn, wdkv_ref[...], preferred_element_type=F32)
    lora = c_ref.shape[-1]
    c = _rms(ckv[:, :lora], kvn_ref[...])
    c_ref[...] = c.astype(BF16)
    ct_ref[...] = c.T.astype(BF16)
    kr_ref[...] = _rope_pair(ckv[:, lora:], cos_ref[...], sin_ref[...]).astype(BF16)


def _mla_down(x2, g, wdq, qn, wdkv, kvn, cosf, sinf):
    m, d = x2.shape
    nq = wdq.shape[1]
    nkv = wdkv.shape[1]
    lora = kvn.shape[1]
    rd = nkv - lora
    tm = 256
    row = lambda i: (i, 0)
    fixed = lambda i: (0, 0)
    blocks = [((tm, d), F32), ((d, nq), BF16), ((d, nkv), BF16), ((tm, nq), BF16), ((tm, lora), BF16)]
    return pl.pallas_call(
        _mla_down_kernel,
        out_shape=(jax.ShapeDtypeStruct((m, nq), BF16), jax.ShapeDtypeStruct((m, lora), BF16),
                   jax.ShapeDtypeStruct((lora, m), BF16), jax.ShapeDtypeStruct((m, rd), BF16)),
        grid=(m // tm,),
        in_specs=[pl.BlockSpec((tm, d), row), pl.BlockSpec((1, d), fixed),
                  pl.BlockSpec((d, nq), fixed), pl.BlockSpec((1, nq), fixed),
                  pl.BlockSpec((d, nkv), fixed), pl.BlockSpec((1, lora), fixed),
                  pl.BlockSpec((tm, rd), row), pl.BlockSpec((tm, rd), row)],
        out_specs=(pl.BlockSpec((tm, nq), row), pl.BlockSpec((tm, lora), row),
                   pl.BlockSpec((lora, tm), lambda i: (0, i)), pl.BlockSpec((tm, rd), row)),
        compiler_params=_params(("arbitrary",), blocks, extra=4 * _nbytes((tm, d), F32)),
        name="mla_down",
    )(x2, g, wdq, qn, wdkv, kvn, cosf, sinf)


def _mla_q_up_kernel(cq_ref, w_ref, cos_ref, sin_ref, q_ref, *, scale):
    res = jnp.dot(cq_ref[...], w_ref[0], preferred_element_type=F32)
    hpg = q_ref.shape[1]
    cos2 = cos_ref[...]
    sin2 = sin_ref[...]
    for hh in range(hpg):
        q_ref[0, hh, :, :MLA_NOPE] = (res[:, hh * MLA_NOPE:(hh + 1) * MLA_NOPE] * scale).astype(BF16)
    low = lax.broadcasted_iota(jnp.int32, cos2.shape, 1) < MLA_ROPE
    for p in range(hpg // 2):
        base = hpg * MLA_NOPE + p * 2 * MLA_ROPE
        r = _rope_pair(res[:, base:base + 2 * MLA_ROPE], cos2, sin2) * scale
        q_ref[0, 2 * p, :, MLA_NOPE:] = jnp.where(low, r, 0.0).astype(BF16)
        q_ref[0, 2 * p + 1, :, MLA_NOPE:] = jnp.where(low, 0.0, r).astype(BF16)


def _mla_q_up(cq, w3, cos2, sin2, batch, seq, scale):
    m, k = cq.shape
    groups, _, gn = w3.shape
    hpg = gn // MLA_QK
    heads = groups * hpg
    tm = 1024
    nsb = seq // tm
    blocks = [((tm, k), BF16), ((k, gn), BF16), ((hpg, tm, MLA_QK_PAD), BF16)]
    return pl.pallas_call(
        functools.partial(_mla_q_up_kernel, scale=scale),
        out_shape=jax.ShapeDtypeStruct((batch, heads, seq, MLA_QK_PAD), BF16),
        grid=(m // tm, groups),
        in_specs=[pl.BlockSpec((tm, k), lambda i, g: (i, 0)),
                  pl.BlockSpec((1, k, gn), lambda i, g: (g, 0, 0)),
                  pl.BlockSpec((tm, 2 * MLA_ROPE), lambda i, g: (i, 0)),
                  pl.BlockSpec((tm, 2 * MLA_ROPE), lambda i, g: (i, 0))],
        out_specs=pl.BlockSpec((1, hpg, tm, MLA_QK_PAD), lambda i, g: (i // nsb, g, i % nsb, 0)),
        compiler_params=_params(("arbitrary", "arbitrary"), blocks, extra=2 * _nbytes((tm, gn), F32)),
        name="mla_q_up",
    )(cq, w3, cos2, sin2)


def _mla_kv_up_kernel(c_ref, ct_ref, wk_ref, wvt_ref, kr_ref, k_ref, vt_ref):
    kn = jnp.dot(c_ref[...], wk_ref[0], preferred_element_type=F32)
    vt = jnp.dot(wvt_ref[0], ct_ref[...], preferred_element_type=F32)
    kr = kr_ref[...]
    for hh in range(k_ref.shape[1]):
        k_ref[0, hh, :, :MLA_NOPE] = kn[:, hh * MLA_NOPE:(hh + 1) * MLA_NOPE].astype(BF16)
        k_ref[0, hh, :, MLA_NOPE:] = kr
        vt_ref[0, hh] = vt[hh * MLA_V:(hh + 1) * MLA_V, :].astype(BF16)


def _mla_kv_up(c, ct, wk3, wvt3, kr, batch, seq):
    m, k = c.shape
    groups, _, gk = wk3.shape
    hpg = gk // MLA_NOPE
    heads = groups * hpg
    tm = 1024
    nsb = seq // tm
    blocks = [((tm, k), BF16), ((k, tm), BF16), ((k, gk), BF16), ((hpg * MLA_V, k), BF16),
              ((hpg, tm, MLA_QK_PAD), BF16), ((hpg, tm, MLA_V), BF16)]
    omap = lambda i, g: (i // nsb, g, i % nsb, 0)
    vmap = lambda i, g: (i // nsb, g, 0, i % nsb)
    return pl.pallas_call(
        _mla_kv_up_kernel,
        out_shape=(jax.ShapeDtypeStruct((batch, heads, seq, MLA_QK_PAD), BF16),
                   jax.ShapeDtypeStruct((batch, heads, MLA_V, seq), BF16)),
        grid=(m // tm, groups),
        in_specs=[pl.BlockSpec((tm, k), lambda i, g: (i, 0)),
                  pl.BlockSpec((k, tm), lambda i, g: (0, i)),
                  pl.BlockSpec((1, k, gk), lambda i, g: (g, 0, 0)),
                  pl.BlockSpec((1, hpg * MLA_V, k), lambda i, g: (g, 0, 0)),
                  pl.BlockSpec((tm, 2 * MLA_ROPE), lambda i, g: (i, 0))],
        out_specs=(pl.BlockSpec((1, hpg, tm, MLA_QK_PAD), omap), pl.BlockSpec((1, hpg, MLA_V, tm), vmap)),
        compiler_params=_params(("arbitrary", "arbitrary"), blocks, extra=4 * _nbytes((tm, gk), F32)),
        name="mla_kv_up",
    )(c, ct, wk3, wvt3, kr)


def _cast_riders(weights, layer, nsteps, step_of):
    in_specs, out_shapes, out_specs, blocks = [], [], [], []
    for w in weights:
        _, rows, cols = w.shape
        slab = rows // nsteps
        assert slab * nsteps == rows and slab % (2 * SUBLANES) == 0, (w.shape, nsteps)
        in_specs.append(pl.BlockSpec((None, slab, cols), lambda *g: (layer, step_of(*g), 0)))
        out_shapes.append(jax.ShapeDtypeStruct((rows, cols), BF16))
        out_specs.append(pl.BlockSpec((slab, cols), lambda *g: (step_of(*g), 0)))
        blocks += [((slab, cols), F32), ((slab, cols), BF16)]
    return in_specs, out_shapes, out_specs, blocks


def _run_cast_riders(src_refs, dst_refs):
    for src, dst in zip(src_refs, dst_refs):
        dst[...] = src[...].astype(dst.dtype)


def _mla_attn_kernel(*refs, tk, n_riders):
    q_ref, k_ref, vt_ref = refs[:3]
    o_ref = refs[3 + n_riders]
    sa_ref, sb_ref, m_ref, l_ref, acc_ref = refs[4 + 2 * n_riders:]
    _run_cast_riders(refs[3:3 + n_riders], refs[4 + n_riders:4 + 2 * n_riders])
    i = pl.program_id(2)
    hp = q_ref.shape[1]
    dv = acc_ref.shape[1]
    m_ref[...] = jnp.full(m_ref.shape, -jnp.inf, F32)
    l_ref[...] = jnp.zeros(l_ref.shape, F32)
    acc_ref[...] = jnp.zeros(acc_ref.shape, F32)

    def scores(j, s_ref):
        start = pl.multiple_of(j * tk, tk)
        for hh in range(hp):
            k = k_ref[0, hh, pl.ds(start, tk), :]
            s_ref[hh] = lax.dot_general(k, q_ref[0, hh], (((1,), (1,)), ((), ())),
                                        preferred_element_type=F32)

    def update(j, s_ref, masked):
        start = pl.multiple_of(j * tk, tk)
        for hh in range(hp):
            st = s_ref[hh]
            if masked:
                kj = lax.broadcasted_iota(jnp.int32, st.shape, 0)
                qi = lax.broadcasted_iota(jnp.int32, st.shape, 1)
                st = jnp.where(kj <= qi, st, -jnp.inf)
            vt = vt_ref[0, hh, :, pl.ds(start, tk)]
            m_prev = m_ref[hh]
            m_new = jnp.maximum(m_prev, jnp.max(st, axis=0, keepdims=True))
            alpha = jnp.exp2(m_prev - m_new)
            pt = jnp.exp2(st - m_new)
            l_ref[hh] = alpha * l_ref[hh] + jnp.sum(pt, axis=0, keepdims=True)
            acc_ref[hh] = alpha * acc_ref[hh] + jnp.dot(vt, pt.astype(BF16), preferred_element_type=F32)
            m_ref[hh] = m_new

    def finish():
        for hh in range(hp):
            o = acc_ref[hh] / l_ref[hh]
            o_ref[:, hh * dv:(hh + 1) * dv] = o.T.astype(o_ref.dtype)

    scores(0, sa_ref)

    def pair(jj, carry):
        j = 2 * jj
        scores(j + 1, sb_ref)
        update(j, sa_ref, False)
        scores(j + 2, sa_ref)
        update(j + 1, sb_ref, False)
        return carry

    lax.fori_loop(0, i // 2, pair, 0)

    @pl.when(i % 2 == 0)
    def _():
        update(i, sa_ref, True)
        finish()

    @pl.when(i % 2 == 1)
    def _():
        scores(i, sb_ref)
        update(i - 1, sa_ref, False)
        update(i, sb_ref, True)
        finish()


def _mla_attention(q, k, vt, hp, ride_weights, ride_layer):
    batch, heads, seq, dqk = q.shape
    dv = vt.shape[2]
    tq = 512
    nh, nq = heads // hp, seq // tq
    r_in, r_shapes, r_out, r_blocks = _cast_riders(ride_weights, ride_layer, batch * nh * nq,
                                                   lambda b, h, i: (b * nh + h) * nq + i)
    blocks = [((hp, tq, 256), BF16), ((hp, seq, 256), BF16), ((hp, dv, seq), BF16), ((tq, hp * dv), BF16)]
    return pl.pallas_call(
        functools.partial(_mla_attn_kernel, tk=tq, n_riders=len(ride_weights)),
        out_shape=(jax.ShapeDtypeStruct((batch * seq, heads * dv), BF16), *r_shapes),
        grid=(batch, nh, nq),
        in_specs=[pl.BlockSpec((1, hp, tq, dqk), lambda b, h, i: (b, h, i, 0)),
                  pl.BlockSpec((1, hp, seq, dqk), lambda b, h, i: (b, h, 0, 0)),
                  pl.BlockSpec((1, hp, dv, seq), lambda b, h, i: (b, h, 0, 0)), *r_in],
        out_specs=(pl.BlockSpec((tq, hp * dv), lambda b, h, i: (b * nq + i, h)), *r_out),
        scratch_shapes=[pltpu.VMEM((hp, tq, tq), F32), pltpu.VMEM((hp, tq, tq), F32),
                        pltpu.VMEM((hp, 1, tq), F32), pltpu.VMEM((hp, 1, tq), F32),
                        pltpu.VMEM((hp, dv, tq), F32)],
        compiler_params=_params(("arbitrary", "arbitrary", "arbitrary"), blocks + r_blocks,
                                extra=8 * hp * _nbytes((tq, tq), F32)),
        name="mla_attention",
    )(q, k, vt, *ride_weights)


def _matmul_kernel(x_ref, w_ref, o_ref, *, scale):
    acc = jnp.dot(x_ref[...], w_ref[...], preferred_element_type=F32)
    if scale is not None:
        acc = acc * scale
    o_ref[...] = acc.astype(o_ref.dtype)


def _matmul(x, w, out_dtype, tm, tn, scale=None, name="matmul"):
    m, k = x.shape
    n = w.shape[1]
    blocks = [((tm, k), x.dtype), ((k, tn), w.dtype), ((tm, tn), out_dtype)]
    return pl.pallas_call(
        functools.partial(_matmul_kernel, scale=scale),
        out_shape=jax.ShapeDtypeStruct((m, n), out_dtype),
        grid=(m // tm, n // tn),
        in_specs=[pl.BlockSpec((tm, k), lambda i, j: (i, 0)), pl.BlockSpec((k, tn), lambda i, j: (0, j))],
        out_specs=pl.BlockSpec((tm, tn), lambda i, j: (i, j)),
        compiler_params=_params(("arbitrary", "arbitrary"), blocks, extra=2 * _nbytes((tm, tn), F32)),
        name=name,
    )(x, w)


def _residual_norm_kernel(*refs, n_next, a_interleaved, out_interleaved):
    a_ref, h_ref, gp_ref = refs[:3]
    gn_refs = refs[3:3 + n_next]
    use_perm = a_interleaved or out_interleaved
    perm_ref = refs[3 + n_next] if use_perm else None
    out_refs = refs[3 + n_next + (1 if use_perm else 0):]
    a = a_ref[...]
    if a_interleaved:
        a = jnp.dot(perm_ref[...], a, preferred_element_type=F32)
    h_new = h_ref[...] + _rms(a.astype(F32), gp_ref[...])
    out_refs[0][...] = h_new
    for gn_ref, o_ref in zip(gn_refs, out_refs[1:]):
        hn = _rms(h_new, gn_ref[...]).astype(o_ref.dtype)
        if out_interleaved:
            hn = jnp.dot(perm_ref[...], hn, preferred_element_type=F32).astype(o_ref.dtype)
        o_ref[...] = hn


def _residual_norm(a, h, g_post, g_next, perm, a_interleaved=False, out_interleaved=False):
    assert not (a_interleaved and out_interleaved)
    m, d = a.shape
    n_next = len(g_next)
    tm = PERM_UNIT
    row = lambda i: (i, 0)
    fixed = lambda i: (0, 0)
    interleave = a_interleaved or out_interleaved
    assert not a_interleaved or a.dtype == BF16, "the permutation matmul is exact only for bf16 rows"
    out_shape = [jax.ShapeDtypeStruct((m, d), F32)] + [jax.ShapeDtypeStruct((m, d), BF16)] * n_next
    out_specs = [pl.BlockSpec((tm, d), row)] * (1 + n_next)
    blocks = [((tm, d), F32)] * (4 + n_next)
    perm_specs = [pl.BlockSpec((tm, tm), fixed)] if interleave else []
    perm_args = [perm] if interleave else []
    return pl.pallas_call(
        functools.partial(_residual_norm_kernel, n_next=n_next, a_interleaved=a_interleaved,
                          out_interleaved=out_interleaved),
        out_shape=tuple(out_shape),
        grid=(m // tm,),
        in_specs=[pl.BlockSpec((tm, d), row), pl.BlockSpec((tm, d), row)]
        + [pl.BlockSpec((1, d), fixed)] * (1 + n_next) + perm_specs,
        out_specs=tuple(out_specs),
        compiler_params=_params(("arbitrary",), blocks),
        name="residual_norm",
    )(a, h, g_post, *g_next, *perm_args)


def _interleave_permutation():
    r = jnp.arange(PERM_UNIT)
    t = (r % SUBLANES) * PERM_GROUPS + r // SUBLANES
    return (t[:, None] == jnp.arange(PERM_UNIT)[None, :]).astype(BF16)


def _ffn_up_kernel(x_ref, wg_ref, wv_ref, cwg_ref, cwv_ref, cbg_ref, cbv_ref, pt_ref, wdn_ref, o_ref,
                   wdn_bf_ref, carry_ref, *, blocks_per_seq):
    _run_cast_riders([wdn_ref], [wdn_bf_ref])
    i = pl.program_id(0)
    j = pl.program_id(1)
    tn = wg_ref.shape[1]
    tm = x_ref.shape[0]
    grp = SUBLANES

    @pl.when((i % blocks_per_seq) == 0)
    def _():
        carry_ref[j] = jnp.zeros(carry_ref.shape[1:], F32)

    w = jnp.concatenate([wg_ref[...], wv_ref[...]], axis=1)
    cw = jnp.concatenate([cwg_ref[...], cwv_ref[...]], axis=1)
    cb = jnp.concatenate([cbg_ref[...], cbv_ref[...]], axis=1)
    first_sublane = lax.broadcasted_iota(jnp.int32, (grp, 2 * tn), 0) == 0
    tail = carry_ref[j]

    def project(c):
        return jnp.dot(x_ref[c * PERM_UNIT:(c + 1) * PERM_UNIT, :], w, preferred_element_type=F32)

    nunits = tm // PERM_UNIT
    u_next = project(0)
    for c in range(nunits):
        rows = slice(c * PERM_UNIT, (c + 1) * PERM_UNIT)
        u = u_next
        if c + 1 < nunits:
            u_next = project(c + 1)
        new_tail = u[PERM_UNIT - 2 * grp:, :]
        wrapped = jnp.where(jnp.concatenate([first_sublane, first_sublane], axis=0),
                            jnp.concatenate([pltpu.roll(tail[:grp], 1, 0), pltpu.roll(tail[grp:], 1, 0)], axis=0),
                            jnp.concatenate([pltpu.roll(new_tail[:grp], 1, 0), pltpu.roll(new_tail[grp:], 1, 0)],
                                            axis=0))
        ext = jnp.concatenate([wrapped, u], axis=0)
        acc = cw[0:1, :] * ext[0:PERM_UNIT] + cw[1:2, :] * ext[grp:grp + PERM_UNIT] + cw[2:3, :] * u
        conv = cb + acc
        gate, val = conv[:, :tn], conv[:, tn:]
        act = (gate * (1.0 / (1.0 + jnp.exp(-gate))) * val).astype(o_ref.dtype)
        o_ref[rows, :] = jnp.dot(pt_ref[...], act, preferred_element_type=F32).astype(o_ref.dtype)
        tail = new_tail
    carry_ref[j] = tail


def _ffn_up(hn, w_in, conv_w, conv_b, perm_t, w_out_all, layer, seq):
    m, d = hn.shape
    d_ff = w_in.shape[1] // 2
    tn = 256
    nj = d_ff // tn
    tm = 2048
    r_in, r_shapes, r_out, r_blocks = _cast_riders([w_out_all], layer, (m // tm) * nj, lambda i, j: i * nj + j)
    blocks = [((tm, d), BF16), ((d, tn), BF16), ((d, tn), BF16), ((tm, tn), BF16)] + r_blocks
    scratch = _nbytes((nj, 2 * SUBLANES, 2 * tn), F32)
    gate_col = lambda i, j: (0, j)
    val_col = lambda i, j: (0, j + nj)
    return pl.pallas_call(
        functools.partial(_ffn_up_kernel, blocks_per_seq=seq // tm),
        out_shape=(jax.ShapeDtypeStruct((m, d_ff), BF16), *r_shapes),
        grid=(m // tm, nj),
        in_specs=[pl.BlockSpec((tm, d), lambda i, j: (i, 0)),
                  pl.BlockSpec((d, tn), gate_col), pl.BlockSpec((d, tn), val_col),
                  pl.BlockSpec((CONV_W, tn), gate_col), pl.BlockSpec((CONV_W, tn), val_col),
                  pl.BlockSpec((1, tn), gate_col), pl.BlockSpec((1, tn), val_col),
                  pl.BlockSpec((PERM_UNIT, PERM_UNIT), lambda i, j: (0, 0)), *r_in],
        out_specs=(pl.BlockSpec((tm, tn), lambda i, j: (i, j)), *r_out),
        scratch_shapes=[pltpu.VMEM((nj, 2 * SUBLANES, 2 * tn), F32)],
        compiler_params=_params(("arbitrary", "arbitrary"), blocks,
                                extra=scratch + 10 * _nbytes((PERM_UNIT, 2 * tn), F32)),
        name="ffn_up",
    )(hn, w_in, w_in, conv_w, conv_w, conv_b, conv_b, perm_t, w_out_all)


def _swa_kv_kernel(x_ref, w_ref, k_ref, vt_ref):
    res = jnp.dot(x_ref[...], w_ref[...], preferred_element_type=F32)
    groups = k_ref.shape[0]
    hd = SWA_HEAD_DIM
    for t in range(groups):
        k_ref[t] = res[:, t * hd:(t + 1) * hd].astype(k_ref.dtype)
    vt = res[:, groups * hd:].T
    for t in range(groups):
        vt_ref[t] = vt[t * hd:(t + 1) * hd, :].astype(vt_ref.dtype)


def _swa_kv(x, w):
    m, k = x.shape
    n = w.shape[1]
    groups = n // (2 * SWA_HEAD_DIM)
    tm = 512
    blocks = [((tm, k), BF16), ((k, n), BF16), ((groups, tm, 128), BF16), ((groups, SWA_HEAD_DIM, tm), BF16)]
    return pl.pallas_call(
        _swa_kv_kernel,
        out_shape=(jax.ShapeDtypeStruct((groups, m, SWA_HEAD_DIM), BF16),
                   jax.ShapeDtypeStruct((groups, SWA_HEAD_DIM, m), BF16)),
        grid=(m // tm,),
        in_specs=[pl.BlockSpec((tm, k), lambda i: (i, 0)), pl.BlockSpec((k, n), lambda i: (0, 0))],
        out_specs=(pl.BlockSpec((groups, tm, SWA_HEAD_DIM), lambda i: (0, i, 0)),
                   pl.BlockSpec((groups, SWA_HEAD_DIM, tm), lambda i: (0, 0, i))),
        compiler_params=_params(("arbitrary",), blocks, extra=3 * _nbytes((tm, n), F32)),
        name="swa_kv",
    )(x, w)


def _swa_attn_kernel(*refs, rep, n_riders):
    bias_ref, sink_ref, q_ref, kp_ref, kc_ref, vtp_ref, vtc_ref = refs[:7]
    o_ref = refs[7 + n_riders]
    _run_cast_riders(refs[7:7 + n_riders], refs[8 + n_riders:8 + 2 * n_riders])
    n = pl.program_id(2)
    w = WINDOW
    hd = SWA_HEAD_DIM
    nq = q_ref.shape[0] // w
    kfull = jnp.concatenate([kp_ref[0], kc_ref[0]], axis=0)
    vtfull = jnp.concatenate([vtp_ref[0], vtc_ref[0]], axis=1)
    sink = sink_ref[0]
    first = jnp.minimum(n, 1)
    for c in range(nq):
        qt = q_ref[c * w:(c + 1) * w, :].astype(F32).T
        qt = jnp.concatenate([qt[r * hd:(r + 1) * hd, :] for r in range(rep)], axis=1).astype(BF16)
        bias = bias_ref[0, first] if c == 0 else bias_ref[0, 1]
        st = jnp.dot(kfull[c * w:(c + 2) * w], qt, preferred_element_type=F32) + bias
        mx = jnp.maximum(jnp.max(st, axis=0, keepdims=True), sink)
        e = jnp.exp2(st - mx)
        denom = jnp.sum(e, axis=0, keepdims=True) + jnp.exp2(sink - mx)
        ot = jnp.dot(vtfull[:, c * w:(c + 2) * w], e.astype(BF16), preferred_element_type=F32)
        ot = ot * (1.0 / denom)
        o = jnp.concatenate([ot[:, r * w:(r + 1) * w] for r in range(rep)], axis=0)
        o_ref[c * w:(c + 1) * w, :] = o.T.astype(o_ref.dtype)


def _swa_attention(q, k, vt, bias, sink_lanes, batch, seq, ride_weights, ride_layer):
    m, dq = q.shape
    groups = k.shape[0]
    rep = dq // (groups * SWA_HEAD_DIM)
    w = WINDOW
    nq = 8
    tq = nq * w
    nb = seq // tq
    gw = rep * SWA_HEAD_DIM
    prev_blk = lambda b, n: jnp.maximum((b * nb + n) * nq - 1, b * nb * nq)
    r_in, r_shapes, r_out, r_blocks = _cast_riders(ride_weights, ride_layer, groups * batch * nb,
                                                   lambda g, b, n: (g * batch + b) * nb + n)
    blocks = [((2, 2 * w, rep * w), F32), ((tq, gw), BF16), ((tq, gw), BF16)] + r_blocks
    return pl.pallas_call(
        functools.partial(_swa_attn_kernel, rep=rep, n_riders=len(ride_weights)),
        out_shape=(jax.ShapeDtypeStruct((m, dq), BF16), *r_shapes),
        grid=(groups, batch, nb),
        in_specs=[pl.BlockSpec((1, 2, 2 * w, rep * w), lambda g, b, n: (g, 0, 0, 0)),
                  pl.BlockSpec((1, 1, rep * w), lambda g, b, n: (g, 0, 0)),
                  pl.BlockSpec((tq, gw), lambda g, b, n: (b * nb + n, g)),
                  pl.BlockSpec((1, w, SWA_HEAD_DIM), lambda g, b, n: (g, prev_blk(b, n), 0)),
                  pl.BlockSpec((1, tq, SWA_HEAD_DIM), lambda g, b, n: (g, b * nb + n, 0)),
                  pl.BlockSpec((1, SWA_HEAD_DIM, w), lambda g, b, n: (g, 0, prev_blk(b, n))),
                  pl.BlockSpec((1, SWA_HEAD_DIM, tq), lambda g, b, n: (g, 0, b * nb + n)), *r_in],
        out_specs=(pl.BlockSpec((tq, gw), lambda g, b, n: (b * nb + n, g)), *r_out),
        compiler_params=_params(("arbitrary", "arbitrary", "arbitrary"), blocks,
                                extra=6 * nq * _nbytes((2 * w, rep * w), F32)),
        name="swa_attention",
    )(bias, sink_lanes, q, k, k, vt, vt, *ride_weights)


def _swa_bias(swa_heads, groups):
    w = WINDOW
    rep = swa_heads // groups
    slopes = 2.0 ** (-8.0 * jnp.arange(1, swa_heads + 1, dtype=F32) / swa_heads)
    kj = jnp.arange(2 * w)[:, None]
    qi = jnp.arange(w)[None, :]
    dist = qi + w - kj
    in_window = (dist >= 0) & (dist < w)
    valid = jnp.stack([in_window & (kj >= w), in_window])
    alibi = slopes.reshape(groups, 1, 1, rep, 1) * dist.astype(F32)[None, None, :, None, :]
    bias = jnp.where(valid[None, :, :, None, :], -alibi * LOG2_E, -jnp.inf)
    return bias.reshape(groups, 2, 2 * w, rep * w)


def kernel(x, positions, norm_mix_pre, norm_mix_post, norm_ffn_pre, norm_ffn_post, mla_w_dq, mla_q_norm,
           mla_w_uq, mla_w_dkv, mla_kv_norm, mla_w_ukv, mla_w_o, shared_kv_norm, swa_w_k, swa_w_v, swa_w_q,
           swa_sinks, swa_w_o, ffn_w_in, ffn_conv_w, ffn_conv_b, ffn_w_out):
    batch, seq, d = x.shape
    m = batch * seq
    depth = norm_mix_pre.shape[0]
    n_a = mla_w_dq.shape[0]
    heads = mla_w_o.shape[1] // MLA_V
    swa_heads = swa_w_q.shape[2] // SWA_HEAD_DIM
    hpg = 4

    half = MLA_ROPE // 2
    freqs = ROPE_THETA ** (-jnp.arange(half, dtype=F32) / half)
    cos2, sin2 = _rope_tables(positions.reshape(m, 1), jnp.tile(freqs, 4)[None, :])

    row = lambda v: v.reshape(1, -1)
    h = x.reshape(m, d)
    assert depth == 2 and n_a == 1, "wired for one MLA layer followed by one sliding-window layer"

    perm = _interleave_permutation()

    def ffn(l, h, hn_ffn, w_in, g_next):
        act, w_out = _ffn_up(hn_ffn, w_in, ffn_conv_w[l], row(ffn_conv_b[l]), perm.T, ffn_w_out, l, seq)
        f = _matmul(act, w_out, BRANCH_DTYPE, 512, 512, name="ffn_down")
        return _residual_norm(f, h, row(norm_ffn_post[l]), g_next, None)

    w_uq = mla_w_uq[0].astype(BF16).reshape(-1, heads // hpg, hpg, MLA_QK)
    w_uq3 = jnp.concatenate([w_uq[..., :MLA_NOPE].reshape(-1, heads // hpg, hpg * MLA_NOPE),
                             w_uq[..., MLA_NOPE:].reshape(-1, heads // hpg, hpg * MLA_ROPE)],
                            axis=2).transpose(1, 0, 2)
    w_ukv = mla_w_ukv[0].astype(BF16).reshape(-1, heads // hpg, hpg, MLA_NOPE + MLA_V)
    w_uk3 = w_ukv[..., :MLA_NOPE].reshape(-1, heads // hpg, hpg * MLA_NOPE).transpose(1, 0, 2)
    w_uvt3 = w_ukv[..., MLA_NOPE:].reshape(-1, heads // hpg, hpg * MLA_V).transpose(1, 2, 0)
    lora = mla_kv_norm.shape[1]
    w_dkv = mla_w_dkv[0].astype(BF16)
    w_dkv2 = jnp.concatenate([w_dkv, w_dkv[:, lora:]], axis=1)
    cq, c, ct, kr = _mla_down(h, row(norm_mix_pre[0]), mla_w_dq[0].astype(BF16), row(mla_q_norm[0]),
                              w_dkv2, row(mla_kv_norm[0]), cos2, sin2)
    q = _mla_q_up(cq, w_uq3, cos2, sin2, batch, seq, MLA_QK ** -0.5 * LOG2_E)
    k, vt = _mla_kv_up(c, ct, w_uk3, w_uvt3, kr, batch, seq)
    o, w_in0, mla_wo, swa_wq, swa_wo = _mla_attention(q, k, vt, 4, [ffn_w_in, mla_w_o, swa_w_q, swa_w_o], 0)
    a = _matmul(o, mla_wo, BRANCH_DTYPE, 2048, 512, name="mla_out")
    h, hn_ffn = _residual_norm(a, h, row(norm_mix_post[0]), [row(norm_ffn_pre[0])], perm, out_interleaved=True)
    h, hn_mix, hn_kv = ffn(0, h, hn_ffn, w_in0, [row(norm_mix_pre[1]), row(shared_kv_norm)])

    w_kv = jnp.concatenate([swa_w_k, swa_w_v], axis=1).astype(BF16)
    k_shared, vt_shared = _swa_kv(hn_kv, w_kv)
    groups = k_shared.shape[0]
    sink_lanes = jnp.repeat(swa_sinks[0].reshape(groups, 1, swa_heads // groups) * LOG2_E, WINDOW, axis=2)
    q = _matmul(hn_mix, swa_wq, BF16, 2048, 512, scale=SWA_HEAD_DIM ** -0.5 * LOG2_E, name="swa_q")
    o, w_in1 = _swa_attention(q, k_shared, vt_shared, _swa_bias(swa_heads, groups), sink_lanes,
                              batch, seq, [ffn_w_in], 1)
    a = _matmul(o, swa_wo, BRANCH_DTYPE, 2048, 512, name="swa_out")
    h, hn_ffn = _residual_norm(a, h, row(norm_mix_post[1]), [row(norm_ffn_pre[1])], perm, out_interleaved=True)
    (h,) = ffn(1, h, hn_ffn, w_in1, [])
    return h.reshape(batch, seq, d)
```

```python
import functools

import jax
import jax.numpy as jnp
from jax import lax
from jax.experimental import pallas as pl
from jax.experimental.pallas import tpu as pltpu

EPS = 1e-6
LOG2_E = 1.4426950408889634
ROPE_THETA = 10000.0
MLA_NOPE = 128
MLA_ROPE = 64
MLA_V = 128
MLA_QK = MLA_NOPE + MLA_ROPE
MLA_QK_PAD = MLA_NOPE + 2 * MLA_ROPE
SWA_HEAD_DIM = 64
SWA_KV_HEADS = 8
WINDOW = 128
CONV_W = 3

V7X_VMEM_BYTES = 64 * 1024 * 1024
VMEM_LIMIT_CAP = V7X_VMEM_BYTES - 8 * 1024 * 1024
SUBLANES = 8
PERM_UNIT = 256
PERM_GROUPS = PERM_UNIT // SUBLANES

BF16 = jnp.bfloat16
F32 = jnp.float32
BRANCH_DTYPE = BF16


def _nbytes(shape, dtype):
    n = 1
    for s in shape:
        n *= s
    return n * jnp.dtype(dtype).itemsize


def _params(semantics, blocks, extra=0):
    est = 2 * sum(_nbytes(s, d) for s, d in blocks) + extra
    limit = min(VMEM_LIMIT_CAP, max(est + est // 4, 16 * 1024 * 1024))
    return pltpu.CompilerParams(dimension_semantics=semantics, vmem_limit_bytes=limit)


def _rms(x, g):
    r = lax.rsqrt(jnp.mean(x * x, axis=-1, keepdims=True) + EPS)
    return (x * r) * g


def _rope_table_kernel(pos_ref, freq_ref, cos_ref, sin_ref):
    ang = pos_ref[...].astype(F32) * freq_ref[...]
    lane = lax.broadcasted_iota(jnp.int32, ang.shape, 1)
    s = jnp.sin(ang)
    cos_ref[...] = jnp.cos(ang)
    sin_ref[...] = jnp.where(lane % MLA_ROPE < MLA_ROPE // 2, -s, s)


def _rope_pair(r, cos2, sin2):
    lane = lax.broadcasted_iota(jnp.int32, r.shape, 1)
    half = MLA_ROPE // 2
    partner = jnp.where(lane % MLA_ROPE < half, pltpu.roll(r, r.shape[1] - half, 1), pltpu.roll(r, half, 1))
    return r * cos2 + partner * sin2


def _rope_tables(pos_col, freqs2):
    m = pos_col.shape[0]
    d = freqs2.shape[1]
    tm = 1024
    return pl.pallas_call(
        _rope_table_kernel,
        out_shape=(jax.ShapeDtypeStruct((m, d), F32), jax.ShapeDtypeStruct((m, d), F32)),
        grid=(m // tm,),
        in_specs=[pl.BlockSpec((tm, 1), lambda i: (i, 0)), pl.BlockSpec((1, d), lambda i: (0, 0))],
        out_specs=(pl.BlockSpec((tm, d), lambda i: (i, 0)), pl.BlockSpec((tm, d), lambda i: (i, 0))),
        name="rope_tables",
    )(pos_col, freqs2)


def _mla_down_kernel(x_ref, g_ref, wdq_ref, qn_ref, wdkv_ref, kvn_ref, cos_ref, sin_ref,
                     cq_ref, c_ref, ct_ref, kr_ref):
    hn = _rms(x_ref[...], g_ref[...]).astype(BF16)
    cq = jnp.dot(hn, wdq_ref[...], preferred_element_type=F32)
    cq_ref[...] = _rms(cq, qn_ref[...]).astype(BF16)
    ckv = jnp.dot(hn, wdkv_ref[...], preferred_element_type=F32)
    lora = c_ref.shape[-1]
    c = _rms(ckv[:, :lora], kvn_ref[...])
    c_ref[...] = c.astype(BF16)
    ct_ref[...] = c.T.astype(BF16)
    kr_ref[...] = _rope_pair(ckv[:, lora:], cos_ref[...], sin_ref[...]).astype(BF16)


def _mla_down(x2, g, wdq, qn, wdkv, kvn, cosf, sinf):
    m, d = x2.shape
    nq = wdq.shape[1]
    nkv = wdkv.shape[1]
    lora = kvn.shape[1]
    rd = nkv - lora
    tm = 256
    row = lambda i: (i, 0)
    fixed = lambda i: (0, 0)
    blocks = [((tm, d), F32), ((d, nq), BF16), ((d, nkv), BF16), ((tm, nq), BF16), ((tm, lora), BF16)]
    return pl.pallas_call(
        _mla_down_kernel,
        out_shape=(jax.ShapeDtypeStruct((m, nq), BF16), jax.ShapeDtypeStruct((m, lora), BF16),
                   jax.ShapeDtypeStruct((lora, m), BF16), jax.ShapeDtypeStruct((m, rd), BF16)),
        grid=(m // tm,),
        in_specs=[pl.BlockSpec((tm, d), row), pl.BlockSpec((1, d), fixed),
                  pl.BlockSpec((d, nq), fixed), pl.BlockSpec((1, nq), fixed),
                  pl.BlockSpec((d, nkv), fixed), pl.BlockSpec((1, lora), fixed),
                  pl.BlockSpec((tm, rd), row), pl.BlockSpec((tm, rd), row)],
        out_specs=(pl.BlockSpec((tm, nq), row), pl.BlockSpec((tm, lora), row),
                   pl.BlockSpec((lora, tm), lambda i: (0, i)), pl.BlockSpec((tm, rd), row)),
        compiler_params=_params(("arbitrary",), blocks, extra=4 * _nbytes((tm, d), F32)),
        name="mla_down",
    )(x2, g, wdq, qn, wdkv, kvn, cosf, sinf)


def _mla_q_up_kernel(cq_ref, w_ref, cos_ref, sin_ref, q_ref, *, scale):
    res = jnp.dot(cq_ref[...], w_ref[0], preferred_element_type=F32)
    hpg = q_ref.shape[1]
    cos2 = cos_ref[...]
    sin2 = sin_ref[...]
    for hh in range(hpg):
        q_ref[0, hh, :, :MLA_NOPE] = (res[:, hh * MLA_NOPE:(hh + 1) * MLA_NOPE] * scale).astype(BF16)
    low = lax.broadcasted_iota(jnp.int32, cos2.shape, 1) < MLA_ROPE
    for p in range(hpg // 2):
        base = hpg * MLA_NOPE + p * 2 * MLA_ROPE
        r = _rope_pair(res[:, base:base + 2 * MLA_ROPE], cos2, sin2) * scale
        q_ref[0, 2 * p, :, MLA_NOPE:] = jnp.where(low, r, 0.0).astype(BF16)
        q_ref[0, 2 * p + 1, :, MLA_NOPE:] = jnp.where(low, 0.0, r).astype(BF16)


def _mla_q_up(cq, w3, cos2, sin2, batch, seq, scale):
    m, k = cq.shape
    groups, _, gn = w3.shape
    hpg = gn // MLA_QK
    heads = groups * hpg
    tm = 1024
    nsb = seq // tm
    blocks = [((tm, k), BF16), ((k, gn), BF16), ((hpg, tm, MLA_QK_PAD), BF16)]
    return pl.pallas_call(
        functools.partial(_mla_q_up_kernel, scale=scale),
        out_shape=jax.ShapeDtypeStruct((batch, heads, seq, MLA_QK_PAD), BF16),
        grid=(m // tm, groups),
        in_specs=[pl.BlockSpec((tm, k), lambda i, g: (i, 0)),
                  pl.BlockSpec((1, k, gn), lambda i, g: (g, 0, 0)),
                  pl.BlockSpec((tm, 2 * MLA_ROPE), lambda i, g: (i, 0)),
                  pl.BlockSpec((tm, 2 * MLA_ROPE), lambda i, g: (i, 0))],
        out_specs=pl.BlockSpec((1, hpg, tm, MLA_QK_PAD), lambda i, g: (i // nsb, g, i % nsb, 0)),
        compiler_params=_params(("arbitrary", "arbitrary"), blocks, extra=2 * _nbytes((tm, gn), F32)),
        name="mla_q_up",
    )(cq, w3, cos2, sin2)


def _mla_kv_up_kernel(c_ref, ct_ref, wk_ref, wvt_ref, kr_ref, k_ref, vt_ref):
    kn = jnp.dot(c_ref[...], wk_ref[0], preferred_element_type=F32)
    vt = jnp.dot(wvt_ref[0], ct_ref[...], preferred_element_type=F32)
    kr = kr_ref[...]
    for hh in range(k_ref.shape[1]):
        k_ref[0, hh, :, :MLA_NOPE] = kn[:, hh * MLA_NOPE:(hh + 1) * MLA_NOPE].astype(BF16)
        k_ref[0, hh, :, MLA_NOPE:] = kr
        vt_ref[0, hh] = vt[hh * MLA_V:(hh + 1) * MLA_V, :].astype(BF16)


def _mla_kv_up(c, ct, wk3, wvt3, kr, batch, seq):
    m, k = c.shape
    groups, _, gk = wk3.shape
    hpg = gk // MLA_NOPE
    heads = groups * hpg
    tm = 1024
    nsb = seq // tm
    blocks = [((tm, k), BF16), ((k, tm), BF16), ((k, gk), BF16), ((hpg * MLA_V, k), BF16),
              ((hpg, tm, MLA_QK_PAD), BF16), ((hpg, tm, MLA_V), BF16)]
    omap = lambda i, g: (i // nsb, g, i % nsb, 0)
    vmap = lambda i, g: (i // nsb, g, 0, i % nsb)
    return pl.pallas_call(
        _mla_kv_up_kernel,
        out_shape=(jax.ShapeDtypeStruct((batch, heads, seq, MLA_QK_PAD), BF16),
                   jax.ShapeDtypeStruct((batch, heads, MLA_V, seq), BF16)),
        grid=(m // tm, groups),
        in_specs=[pl.BlockSpec((tm, k), lambda i, g: (i, 0)),
                  pl.BlockSpec((k, tm), lambda i, g: (0, i)),
                  pl.BlockSpec((1, k, gk), lambda i, g: (g, 0, 0)),
                  pl.BlockSpec((1, hpg * MLA_V, k), lambda i, g: (g, 0, 0)),
                  pl.BlockSpec((tm, 2 * MLA_ROPE), lambda i, g: (i, 0))],
        out_specs=(pl.BlockSpec((1, hpg, tm, MLA_QK_PAD), omap), pl.BlockSpec((1, hpg, MLA_V, tm), vmap)),
        compiler_params=_params(("arbitrary", "arbitrary"), blocks, extra=4 * _nbytes((tm, gk), F32)),
        name="mla_kv_up",
    )(c, ct, wk3, wvt3, kr)


def _cast_riders(weights, layer, nsteps, step_of):
    in_specs, out_shapes, out_specs, blocks = [], [], [], []
    for w in weights:
        _, rows, cols = w.shape
        slab = rows // nsteps
        assert slab * nsteps == rows and slab % (2 * SUBLANES) == 0, (w.shape, nsteps)
        in_specs.append(pl.BlockSpec((None, slab, cols), lambda *g: (layer, step_of(*g), 0)))
        out_shapes.append(jax.ShapeDtypeStruct((rows, cols), BF16))
        out_specs.append(pl.BlockSpec((slab, cols), lambda *g: (step_of(*g), 0)))
        blocks += [((slab, cols), F32), ((slab, cols), BF16)]
    return in_specs, out_shapes, out_specs, blocks


def _run_cast_riders(src_refs, dst_refs):
    for src, dst in zip(src_refs, dst_refs):
        dst[...] = src[...].astype(dst.dtype)


def _mla_attn_kernel(*refs, tk, n_riders):
    q_ref, k_ref, vt_ref = refs[:3]
    o_ref = refs[3 + n_riders]
    sa_ref, sb_ref, m_ref, l_ref, acc_ref = refs[4 + 2 * n_riders:]
    _run_cast_riders(refs[3:3 + n_riders], refs[4 + n_riders:4 + 2 * n_riders])
    i = pl.program_id(2)
    hp = q_ref.shape[1]
    dv = acc_ref.shape[1]
    m_ref[...] = jnp.full(m_ref.shape, -jnp.inf, F32)
    l_ref[...] = jnp.zeros(l_ref.shape, F32)
    acc_ref[...] = jnp.zeros(acc_ref.shape, F32)

    def scores(j, s_ref):
        start = pl.multiple_of(j * tk, tk)
        for hh in range(hp):
            k = k_ref[0, hh, pl.ds(start, tk), :]
            s_ref[hh] = lax.dot_general(k, q_ref[0, hh], (((1,), (1,)), ((), ())),
                                        preferred_element_type=F32)

    def update(j, s_ref, masked):
        start = pl.multiple_of(j * tk, tk)
        for hh in range(hp):
            st = s_ref[hh]
            if masked:
                kj = lax.broadcasted_iota(jnp.int32, st.shape, 0)
                qi = lax.broadcasted_iota(jnp.int32, st.shape, 1)
                st = jnp.where(kj <= qi, st, -jnp.inf)
            vt = vt_ref[0, hh, :, pl.ds(start, tk)]
            m_prev = m_ref[hh]
            m_new = jnp.maximum(m_prev, jnp.max(st, axis=0, keepdims=True))
            alpha = jnp.exp2(m_prev - m_new)
            pt = jnp.exp2(st - m_new)
            l_ref[hh] = alpha * l_ref[hh] + jnp.sum(pt, axis=0, keepdims=True)
            acc_ref[hh] = alpha * acc_ref[hh] + jnp.dot(vt, pt.astype(BF16), preferred_element_type=F32)
            m_ref[hh] = m_new

    def finish():
        for hh in range(hp):
            o = acc_ref[hh] / l_ref[hh]
            o_ref[:, hh * dv:(hh + 1) * dv] = o.T.astype(o_ref.dtype)

    scores(0, sa_ref)

    def pair(jj, carry):
        j = 2 * jj
        scores(j + 1, sb_ref)
        update(j, sa_ref, False)
        scores(j + 2, sa_ref)
        update(j + 1, sb_ref, False)
        return carry

    lax.fori_loop(0, i // 2, pair, 0)

    @pl.when(i % 2 == 0)
    def _():
        update(i, sa_ref, True)
        finish()

    @pl.when(i % 2 == 1)
    def _():
        scores(i, sb_ref)
        update(i - 1, sa_ref, False)
        update(i, sb_ref, True)
        finish()


def _mla_attention(q, k, vt, hp, ride_weights, ride_layer):
    batch, heads, seq, dqk = q.shape
    dv = vt.shape[2]
    tq = 512
    nh, nq = heads // hp, seq // tq
    r_in, r_shapes, r_out, r_blocks = _cast_riders(ride_weights, ride_layer, batch * nh * nq,
                                                   lambda b, h, i: (b * nh + h) * nq + i)
    blocks = [((hp, tq, 256), BF16), ((hp, seq, 256), BF16), ((hp, dv, seq), BF16), ((tq, hp * dv), BF16)]
    return pl.pallas_call(
        functools.partial(_mla_attn_kernel, tk=tq, n_riders=len(ride_weights)),
        out_shape=(jax.ShapeDtypeStruct((batch * seq, heads * dv), BF16), *r_shapes),
        grid=(batch, nh, nq),
        in_specs=[pl.BlockSpec((1, hp, tq, dqk), lambda b, h, i: (b, h, i, 0)),
                  pl.BlockSpec((1, hp, seq, dqk), lambda b, h, i: (b, h, 0, 0)),
                  pl.BlockSpec((1, hp, dv, seq), lambda b, h, i: (b, h, 0, 0)), *r_in],
        out_specs=(pl.BlockSpec((tq, hp * dv), lambda b, h, i: (b * nq + i, h)), *r_out),
        scratch_shapes=[pltpu.VMEM((hp, tq, tq), F32), pltpu.VMEM((hp, tq, tq), F32),
                        pltpu.VMEM((hp, 1, tq), F32), pltpu.VMEM((hp, 1, tq), F32),
                        pltpu.VMEM((hp, dv, tq), F32)],
        compiler_params=_params(("arbitrary", "arbitrary", "arbitrary"), blocks + r_blocks,
                                extra=8 * hp * _nbytes((tq, tq), F32)),
        name="mla_attention",
    )(q, k, vt, *ride_weights)


def _matmul_kernel(x_ref, w_ref, o_ref, *, scale):
    acc = jnp.dot(x_ref[...], w_ref[...], preferred_element_type=F32)
    if scale is not None:
        acc = acc * scale
    o_ref[...] = acc.astype(o_ref.dtype)


def _matmul(x, w, out_dtype, tm, tn, scale=None, name="matmul"):
    m, k = x.shape
    n = w.shape[1]
    blocks = [((tm, k), x.dtype), ((k, tn), w.dtype), ((tm, tn), out_dtype)]
    return pl.pallas_call(
        functools.partial(_matmul_kernel, scale=scale),
        out_shape=jax.ShapeDtypeStruct((m, n), out_dtype),
        grid=(m // tm, n // tn),
        in_specs=[pl.BlockSpec((tm, k), lambda i, j: (i, 0)), pl.BlockSpec((k, tn), lambda i, j: (0, j))],
        out_specs=pl.BlockSpec((tm, tn), lambda i, j: (i, j)),
        compiler_params=_params(("arbitrary", "arbitrary"), blocks, extra=2 * _nbytes((tm, tn), F32)),
        name=name,
    )(x, w)


def _residual_norm_kernel(*refs, n_next, interleave):
    a_ref, h_ref, gp_ref = refs[:3]
    gn_refs = refs[3:3 + n_next]
    perm_ref = refs[3 + n_next] if interleave else None
    out_refs = refs[3 + n_next + (1 if interleave else 0):]
    h_new = h_ref[...] + _rms(a_ref[...].astype(F32), gp_ref[...])
    out_refs[0][...] = h_new
    for gn_ref, o_ref in zip(gn_refs, out_refs[1:]):
        hn = _rms(h_new, gn_ref[...]).astype(o_ref.dtype)
        if interleave:
            hn = jnp.dot(perm_ref[...], hn, preferred_element_type=F32).astype(o_ref.dtype)
        o_ref[...] = hn


def _residual_norm(a, h, g_post, g_next, perm=None):
    m, d = a.shape
    n_next = len(g_next)
    tm = PERM_UNIT
    row = lambda i: (i, 0)
    fixed = lambda i: (0, 0)
    interleave = perm is not None
    out_shape = [jax.ShapeDtypeStruct((m, d), F32)] + [jax.ShapeDtypeStruct((m, d), BF16)] * n_next
    out_specs = [pl.BlockSpec((tm, d), row)] * (1 + n_next)
    blocks = [((tm, d), F32)] * (4 + n_next)
    perm_specs = [pl.BlockSpec((tm, tm), fixed)] if interleave else []
    perm_args = [perm] if interleave else []
    return pl.pallas_call(
        functools.partial(_residual_norm_kernel, n_next=n_next, interleave=interleave),
        out_shape=tuple(out_shape),
        grid=(m // tm,),
        in_specs=[pl.BlockSpec((tm, d), row), pl.BlockSpec((tm, d), row)]
        + [pl.BlockSpec((1, d), fixed)] * (1 + n_next) + perm_specs,
        out_specs=tuple(out_specs),
        compiler_params=_params(("arbitrary",), blocks),
        name="residual_norm",
    )(a, h, g_post, *g_next, *perm_args)


def _interleave_permutation():
    r = jnp.arange(PERM_UNIT)
    t = (r % SUBLANES) * PERM_GROUPS + r // SUBLANES
    return (t[:, None] == jnp.arange(PERM_UNIT)[None, :]).astype(BF16)


def _ffn_up_kernel(x_ref, wg_ref, wv_ref, cwg_ref, cwv_ref, cbg_ref, cbv_ref, pt_ref, wdn_ref, o_ref,
                   wdn_bf_ref, carry_ref, *, blocks_per_seq):
    _run_cast_riders([wdn_ref], [wdn_bf_ref])
    i = pl.program_id(0)
    j = pl.program_id(1)
    tn = wg_ref.shape[1]
    tm = x_ref.shape[0]
    grp = SUBLANES

    @pl.when((i % blocks_per_seq) == 0)
    def _():
        carry_ref[j] = jnp.zeros(carry_ref.shape[1:], F32)

    w = jnp.concatenate([wg_ref[...], wv_ref[...]], axis=1)
    cw = jnp.concatenate([cwg_ref[...] * 0.5, cwv_ref[...]], axis=1)
    cb = jnp.concatenate([cbg_ref[...] * 0.5, cbv_ref[...]], axis=1)
    first_sublane = lax.broadcasted_iota(jnp.int32, (grp, 2 * tn), 0) == 0
    tail = carry_ref[j]

    def project(c):
        return jnp.dot(x_ref[c * PERM_UNIT:(c + 1) * PERM_UNIT, :], w, preferred_element_type=F32)

    nunits = tm // PERM_UNIT
    u_next = project(0)
    for c in range(nunits):
        rows = slice(c * PERM_UNIT, (c + 1) * PERM_UNIT)
        u = u_next
        if c + 1 < nunits:
            u_next = project(c + 1)
        new_tail = u[PERM_UNIT - 2 * grp:, :]
        wrapped = jnp.where(jnp.concatenate([first_sublane, first_sublane], axis=0),
                            jnp.concatenate([pltpu.roll(tail[:grp], 1, 0), pltpu.roll(tail[grp:], 1, 0)], axis=0),
                            jnp.concatenate([pltpu.roll(new_tail[:grp], 1, 0), pltpu.roll(new_tail[grp:], 1, 0)],
                                            axis=0))
        ext = jnp.concatenate([wrapped, u], axis=0)
        acc = cw[0:1, :] * ext[0:PERM_UNIT] + cw[1:2, :] * ext[grp:grp + PERM_UNIT] + cw[2:3, :] * u
        conv = cb + acc
        half_gate, val = conv[:, :tn], conv[:, tn:]
        act = (half_gate * (1.0 + jnp.tanh(half_gate)) * val).astype(o_ref.dtype)
        o_ref[rows, :] = jnp.dot(pt_ref[...], act, preferred_element_type=F32).astype(o_ref.dtype)
        tail = new_tail
    carry_ref[j] = tail


def _ffn_up(hn, w_in, conv_w, conv_b, perm_t, w_out_all, layer, seq):
    m, d = hn.shape
    d_ff = w_in.shape[1] // 2
    tn = 256
    nj = d_ff // tn
    tm = 2048
    r_in, r_shapes, r_out, r_blocks = _cast_riders([w_out_all], layer, (m // tm) * nj, lambda i, j: i * nj + j)
    blocks = [((tm, d), BF16), ((d, tn), BF16), ((d, tn), BF16), ((tm, tn), BF16)] + r_blocks
    scratch = _nbytes((nj, 2 * SUBLANES, 2 * tn), F32)
    gate_col = lambda i, j: (0, j)
    val_col = lambda i, j: (0, j + nj)
    return pl.pallas_call(
        functools.partial(_ffn_up_kernel, blocks_per_seq=seq // tm),
        out_shape=(jax.ShapeDtypeStruct((m, d_ff), BF16), *r_shapes),
        grid=(m // tm, nj),
        in_specs=[pl.BlockSpec((tm, d), lambda i, j: (i, 0)),
                  pl.BlockSpec((d, tn), gate_col), pl.BlockSpec((d, tn), val_col),
                  pl.BlockSpec((CONV_W, tn), gate_col), pl.BlockSpec((CONV_W, tn), val_col),
                  pl.BlockSpec((1, tn), gate_col), pl.BlockSpec((1, tn), val_col),
                  pl.BlockSpec((PERM_UNIT, PERM_UNIT), lambda i, j: (0, 0)), *r_in],
        out_specs=(pl.BlockSpec((tm, tn), lambda i, j: (i, j)), *r_out),
        scratch_shapes=[pltpu.VMEM((nj, 2 * SUBLANES, 2 * tn), F32)],
        compiler_params=_params(("arbitrary", "arbitrary"), blocks,
                                extra=scratch + 10 * _nbytes((PERM_UNIT, 2 * tn), F32)),
        name="ffn_up",
    )(hn, w_in, w_in, conv_w, conv_w, conv_b, conv_b, perm_t, w_out_all)


def _swa_kv_kernel(x_ref, w_ref, k_ref, vt_ref):
    res = jnp.dot(x_ref[...], w_ref[...], preferred_element_type=F32)
    groups = k_ref.shape[0]
    hd = SWA_HEAD_DIM
    for t in range(groups):
        k_ref[t] = res[:, t * hd:(t + 1) * hd].astype(k_ref.dtype)
    vt = res[:, groups * hd:].T
    for t in range(groups):
        vt_ref[t] = vt[t * hd:(t + 1) * hd, :].astype(vt_ref.dtype)


def _swa_kv(x, w):
    m, k = x.shape
    n = w.shape[1]
    groups = n // (2 * SWA_HEAD_DIM)
    tm = 512
    blocks = [((tm, k), BF16), ((k, n), BF16), ((groups, tm, 128), BF16), ((groups, SWA_HEAD_DIM, tm), BF16)]
    return pl.pallas_call(
        _swa_kv_kernel,
        out_shape=(jax.ShapeDtypeStruct((groups, m, SWA_HEAD_DIM), BF16),
                   jax.ShapeDtypeStruct((groups, SWA_HEAD_DIM, m), BF16)),
        grid=(m // tm,),
        in_specs=[pl.BlockSpec((tm, k), lambda i: (i, 0)), pl.BlockSpec((k, n), lambda i: (0, 0))],
        out_specs=(pl.BlockSpec((groups, tm, SWA_HEAD_DIM), lambda i: (0, i, 0)),
                   pl.BlockSpec((groups, SWA_HEAD_DIM, tm), lambda i: (0, 0, i))),
        compiler_params=_params(("arbitrary",), blocks, extra=3 * _nbytes((tm, n), F32)),
        name="swa_kv",
    )(x, w)


def _swa_attn_kernel(*refs, rep, n_riders):
    bias_ref, sink_ref, q_ref, kp_ref, kc_ref, vtp_ref, vtc_ref = refs[:7]
    o_ref = refs[7 + n_riders]
    _run_cast_riders(refs[7:7 + n_riders], refs[8 + n_riders:8 + 2 * n_riders])
    n = pl.program_id(2)
    w = WINDOW
    hd = SWA_HEAD_DIM
    nq = q_ref.shape[0] // w
    kfull = jnp.concatenate([kp_ref[0], kc_ref[0]], axis=0)
    vtfull = jnp.concatenate([vtp_ref[0], vtc_ref[0]], axis=1)
    sink = sink_ref[0]
    first = jnp.minimum(n, 1)
    for c in range(nq):
        qt = q_ref[c * w:(c + 1) * w, :].astype(F32).T
        qt = jnp.concatenate([qt[r * hd:(r + 1) * hd, :] for r in range(rep)], axis=1).astype(BF16)
        bias = bias_ref[0, first] if c == 0 else bias_ref[0, 1]
        st = jnp.dot(kfull[c * w:(c + 2) * w], qt, preferred_element_type=F32) + bias
        mx = jnp.maximum(jnp.max(st, axis=0, keepdims=True), sink)
        e = jnp.exp2(st - mx)
        denom = jnp.sum(e, axis=0, keepdims=True) + jnp.exp2(sink - mx)
        ot = jnp.dot(vtfull[:, c * w:(c + 2) * w], e.astype(BF16), preferred_element_type=F32)
        ot = ot * (1.0 / denom)
        o = jnp.concatenate([ot[:, r * w:(r + 1) * w] for r in range(rep)], axis=0)
        o_ref[c * w:(c + 1) * w, :] = o.T.astype(o_ref.dtype)


def _swa_attention(q, k, vt, bias, sink_lanes, batch, seq, ride_weights, ride_layer):
    m, dq = q.shape
    groups = k.shape[0]
    rep = dq // (groups * SWA_HEAD_DIM)
    w = WINDOW
    nq = 8
    tq = nq * w
    nb = seq // tq
    gw = rep * SWA_HEAD_DIM
    prev_blk = lambda b, n: jnp.maximum((b * nb + n) * nq - 1, b * nb * nq)
    r_in, r_shapes, r_out, r_blocks = _cast_riders(ride_weights, ride_layer, groups * batch * nb,
                                                   lambda g, b, n: (g * batch + b) * nb + n)
    blocks = [((2, 2 * w, rep * w), F32), ((tq, gw), BF16), ((tq, gw), BF16)] + r_blocks
    return pl.pallas_call(
        functools.partial(_swa_attn_kernel, rep=rep, n_riders=len(ride_weights)),
        out_shape=(jax.ShapeDtypeStruct((m, dq), BF16), *r_shapes),
        grid=(groups, batch, nb),
        in_specs=[pl.BlockSpec((1, 2, 2 * w, rep * w), lambda g, b, n: (g, 0, 0, 0)),
                  pl.BlockSpec((1, 1, rep * w), lambda g, b, n: (g, 0, 0)),
                  pl.BlockSpec((tq, gw), lambda g, b, n: (b * nb + n, g)),
                  pl.BlockSpec((1, w, SWA_HEAD_DIM), lambda g, b, n: (g, prev_blk(b, n), 0)),
                  pl.BlockSpec((1, tq, SWA_HEAD_DIM), lambda g, b, n: (g, b * nb + n, 0)),
                  pl.BlockSpec((1, SWA_HEAD_DIM, w), lambda g, b, n: (g, 0, prev_blk(b, n))),
                  pl.BlockSpec((1, SWA_HEAD_DIM, tq), lambda g, b, n: (g, 0, b * nb + n)), *r_in],
        out_specs=(pl.BlockSpec((tq, gw), lambda g, b, n: (b * nb + n, g)), *r_out),
        compiler_params=_params(("arbitrary", "arbitrary", "arbitrary"), blocks,
                                extra=6 * nq * _nbytes((2 * w, rep * w), F32)),
        name="swa_attention",
    )(bias, sink_lanes, q, k, k, vt, vt, *ride_weights)


def _swa_bias(swa_heads, groups):
    w = WINDOW
    rep = swa_heads // groups
    slopes = 2.0 ** (-8.0 * jnp.arange(1, swa_heads + 1, dtype=F32) / swa_heads)
    kj = jnp.arange(2 * w)[:, None]
    qi = jnp.arange(w)[None, :]
    dist = qi + w - kj
    in_window = (dist >= 0) & (dist < w)
    valid = jnp.stack([in_window & (kj >= w), in_window])
    alibi = slopes.reshape(groups, 1, 1, rep, 1) * dist.astype(F32)[None, None, :, None, :]
    bias = jnp.where(valid[None, :, :, None, :], -alibi * LOG2_E, -jnp.inf)
    return bias.reshape(groups, 2, 2 * w, rep * w)


def kernel(x, positions, norm_mix_pre, norm_mix_post, norm_ffn_pre, norm_ffn_post, mla_w_dq, mla_q_norm,
           mla_w_uq, mla_w_dkv, mla_kv_norm, mla_w_ukv, mla_w_o, shared_kv_norm, swa_w_k, swa_w_v, swa_w_q,
           swa_sinks, swa_w_o, ffn_w_in, ffn_conv_w, ffn_conv_b, ffn_w_out):
    batch, seq, d = x.shape
    m = batch * seq
    depth = norm_mix_pre.shape[0]
    n_a = mla_w_dq.shape[0]
    heads = mla_w_o.shape[1] // MLA_V
    swa_heads = swa_w_q.shape[2] // SWA_HEAD_DIM
    hpg = 4

    half = MLA_ROPE // 2
    freqs = ROPE_THETA ** (-jnp.arange(half, dtype=F32) / half)
    cos2, sin2 = _rope_tables(positions.reshape(m, 1), jnp.tile(freqs, 4)[None, :])

    row = lambda v: v.reshape(1, -1)
    h = x.reshape(m, d)
    assert depth == 2 and n_a == 1, "wired for one MLA layer followed by one sliding-window layer"

    perm = _interleave_permutation()

    def ffn(l, h, hn_ffn, w_in, g_next):
        act, w_out = _ffn_up(hn_ffn, w_in, ffn_conv_w[l], row(ffn_conv_b[l]), perm.T, ffn_w_out, l, seq)
        f = _matmul(act, w_out, BRANCH_DTYPE, 512, 512, name="ffn_down")
        return _residual_norm(f, h, row(norm_ffn_post[l]), g_next)

    w_uq = mla_w_uq[0].astype(BF16).reshape(-1, heads // hpg, hpg, MLA_QK)
    w_uq3 = jnp.concatenate([w_uq[..., :MLA_NOPE].reshape(-1, heads // hpg, hpg * MLA_NOPE),
                             w_uq[..., MLA_NOPE:].reshape(-1, heads // hpg, hpg * MLA_ROPE)],
                            axis=2).transpose(1, 0, 2)
    w_ukv = mla_w_ukv[0].astype(BF16).reshape(-1, heads // hpg, hpg, MLA_NOPE + MLA_V)
    w_uk3 = w_ukv[..., :MLA_NOPE].reshape(-1, heads // hpg, hpg * MLA_NOPE).transpose(1, 0, 2)
    w_uvt3 = w_ukv[..., MLA_NOPE:].reshape(-1, heads // hpg, hpg * MLA_V).transpose(1, 2, 0)
    lora = mla_kv_norm.shape[1]
    w_dkv = mla_w_dkv[0].astype(BF16)
    w_dkv2 = jnp.concatenate([w_dkv, w_dkv[:, lora:]], axis=1)
    cq, c, ct, kr = _mla_down(h, row(norm_mix_pre[0]), mla_w_dq[0].astype(BF16), row(mla_q_norm[0]),
                              w_dkv2, row(mla_kv_norm[0]), cos2, sin2)
    q = _mla_q_up(cq, w_uq3, cos2, sin2, batch, seq, MLA_QK ** -0.5 * LOG2_E)
    k, vt = _mla_kv_up(c, ct, w_uk3, w_uvt3, kr, batch, seq)
    o, w_in0, mla_wo, swa_wq, swa_wo = _mla_attention(q, k, vt, 4, [ffn_w_in, mla_w_o, swa_w_q, swa_w_o], 0)
    a = _matmul(o, mla_wo, BRANCH_DTYPE, 2048, 512, name="mla_out")
    h, hn_ffn = _residual_norm(a, h, row(norm_mix_post[0]), [row(norm_ffn_pre[0])], perm)
    h, hn_mix, hn_kv = ffn(0, h, hn_ffn, w_in0, [row(norm_mix_pre[1]), row(shared_kv_norm)])

    w_kv = jnp.concatenate([swa_w_k, swa_w_v], axis=1).astype(BF16)
    k_shared, vt_shared = _swa_kv(hn_kv, w_kv)
    groups = k_shared.shape[0]
    sink_lanes = jnp.repeat(swa_sinks[0].reshape(groups, 1, swa_heads // groups) * LOG2_E, WINDOW, axis=2)
    q = _matmul(hn_mix, swa_wq, BF16, 2048, 512, scale=SWA_HEAD_DIM ** -0.5 * LOG2_E, name="swa_q")
    o, w_in1 = _swa_attention(q, k_shared, vt_shared, _swa_bias(swa_heads, groups), sink_lanes,
                              batch, seq, [ffn_w_in], 1)
    a = _matmul(o, swa_wo, BRANCH_DTYPE, 2048, 512, name="swa_out")
    h, hn_ffn = _residual_norm(a, h, row(norm_mix_post[1]), [row(norm_ffn_pre[1])], perm)
    (h,) = ffn(1, h, hn_ffn, w_in1, [])
    return h.reshape(batch, seq, d)
```
